```python
import math
import jax
import jax.numpy as jnp
from jax import lax
import numpy as np

D_MODEL = 1024
BATCH = 8
SEQ = 4096
DEPTH = 4

D_A = D_MODEL // 2
H_A = 4
DH_A = D_A // H_A
CHUNK_A = 64
D_B = D_MODEL // 2
HY_BANDS = 16
HY_EMB = 2 * HY_BANDS + 1
HY_HID = 64
HY_TARGET = 1e-2
HY_FAST_PCT = 0.3
HY_SLOW_PCT = 1.5
D_C = D_MODEL
HEAD_C = 64
H_C = D_C // HEAD_C
LORA_W = 64
LORA_A = 64
LNX_EPS = 64e-5
SHORT_CONV = 3
NORM_EPS = 1e-6
HEAD_NORM_EPS = 1e-5
N_SHIFT_C = 3 * D_C + 2 * LORA_W + 2 * LORA_A
N_IN = 2 * D_A + 4 * D_B + N_SHIFT_C + D_C + 3 * D_MODEL

kernel_name = 'hybrid_mlstm_hyena_rwkv7_encoder'


def rmsnorm(x, g):
    xf = x.astype(jnp.float32)
    y = xf * lax.rsqrt(jnp.mean(xf * xf, axis=-1, keepdims=True) + NORM_EPS)
    return (y * g.astype(jnp.float32)).astype(x.dtype)


def head_norm(h, eps):
    h = h.astype(jnp.float32)
    mu = jnp.mean(h, axis=-1, keepdims=True)
    var = jnp.mean(jnp.square(h - mu), axis=-1, keepdims=True)
    return (h - mu) * lax.rsqrt(var + eps)


def split_cols(t, sizes):
    offs = np.cumsum(sizes)[:-1].tolist()
    return jnp.split(t, offs, axis=-1)


def short_conv(u, w, b):
    ch = u.shape[-1]
    y = lax.conv_general_dilated(u, w[:, None, :].astype(u.dtype), window_strides=(1,),
                                 padding=[(SHORT_CONV // 2, SHORT_CONV // 2)],
                                 dimension_numbers=('NWC', 'WIO', 'NWC'),
                                 feature_group_count=ch)
    return y + b


def neighbour_shift(u, mu):
    up = jnp.pad(u, ((0, 0), (1, 1), (0, 0)))
    nb = 0.5 * (up[:, :-2] + up[:, 2:])
    return u + mu * (nb - u)


def mlstm_chunkwise(q, k, v, log_i, log_f):
    B, S, H, Dh = q.shape
    L = CHUNK_A
    NC = S // L
    f32 = jnp.float32
    k = k * (Dh ** -0.5)

    def blocks(t):
        return t.reshape(B, NC, L, H, -1).transpose(1, 0, 3, 2, 4)

    qc, kc, vc = blocks(q), blocks(k), blocks(v)
    li = blocks(log_i[..., None])[..., 0]
    bc = jnp.cumsum(blocks(log_f[..., None])[..., 0], axis=-1)
    lower = jnp.tril(jnp.ones((L, L), dtype=bool))

    def step(carry, inp):
        C, n, m = carry
        qj, kj, vj, lij, bj = inp
        qf, kf, vf = qj.astype(f32), kj.astype(f32), vj.astype(f32)
        d = jnp.where(lower, bj[..., :, None] - bj[..., None, :] + lij[..., None, :], -jnp.inf)
        inter = bj + m[..., None]
        m_t = jnp.maximum(inter, jnp.max(d, axis=-1))
        w_intra = jnp.exp(d - m_t[..., None])
        w_inter = jnp.exp(inter - m_t)
        s = jnp.einsum('bhtd,bhsd->bhts', qf, kf) * w_intra
        num = (jnp.einsum('bhts,bhsd->bhtd', s, vf)
               + w_inter[..., None] * jnp.einsum('bhde,bhte->bhtd', C, qf))
        den = jnp.sum(s, axis=-1) + w_inter * jnp.einsum('bhe,bhte->bht', n, qf)
        h = num / jnp.maximum(jnp.abs(den), jnp.exp(-m_t))[..., None]
        g = bj[..., -1]
        a = g[..., None] - bj + lij
        m_new = jnp.maximum(g + m, jnp.max(a, axis=-1))
        decay = jnp.exp(g + m - m_new)
        wk = jnp.exp(a - m_new[..., None])
        C = decay[..., None, None] * C + jnp.einsum('bhsd,bhse->bhde', vf * wk[..., None], kf)
        n = decay[..., None] * n + jnp.einsum('bhs,bhse->bhe', wk, kf)
        return (C, n, m_new), h

    init = (jnp.zeros((B, H, Dh, Dh), f32), jnp.zeros((B, H, Dh), f32),
            jnp.full((B, H), -jnp.inf, f32))
    _, hs = lax.scan(step, init, (qc, kc, vc, li, bc))
    return hs.transpose(1, 0, 3, 2, 4).reshape(B, S, H, Dh)


def mlstm_branch(xa, za, conv_w, conv_b, wq, wk, wv, gate_w, gate_b, norm_g, skip):
    B, S, _ = xa.shape
    xc = jax.nn.silu(short_conv(xa, conv_w, conv_b))
    q = jnp.einsum('bshd,hde->bshe', xc.reshape(B, S, H_A, DH_A), wq)
    k = jnp.einsum('bshd,hde->bshe', xc.reshape(B, S, H_A, DH_A), wk)
    v = jnp.einsum('bshd,hde->bshe', xa.reshape(B, S, H_A, DH_A), wv)
    gin = jnp.concatenate([q.reshape(B, S, D_A), k.reshape(B, S, D_A), v.reshape(B, S, D_A)], axis=-1)
    gates = (gin @ gate_w + gate_b).astype(jnp.float32).reshape(B, S, 2, 2, H_A)
    log_i = gates[:, :, :, 0]
    log_f = jax.nn.log_sigmoid(gates[:, :, :, 1])
    h_fwd = mlstm_chunkwise(q, k, v, log_i[:, :, 0], log_f[:, :, 0])
    rev = lambda t: jnp.flip(t, axis=1)
    h_bwd = rev(mlstm_chunkwise(rev(q), rev(k), rev(v), rev(log_i[:, :, 1]), rev(log_f[:, :, 1])))
    h = head_norm(h_fwd + h_bwd, HEAD_NORM_EPS).reshape(B, S, D_A) * norm_g
    return ((h + skip * xc) * jax.nn.silu(za)).astype(xa.dtype)


def hyena_positional_features(L):
    t = jnp.linspace(0.0, 1.0, L, dtype=jnp.float32)
    w = 2.0 * math.pi * jnp.arange(L, dtype=jnp.float32) / L
    f = jnp.linspace(1e-4, HY_BANDS - 1, HY_BANDS, dtype=jnp.float32)
    z = w[:, None] * f[None, :]
    feats = jnp.concatenate([t[:, None], jnp.cos(z), -jnp.sin(z)], axis=-1)
    return feats, t


def hyena_filter(feats, t, w1, b1, w2, b2, w3, b3, freq, w_out, decay):
    L = feats.shape[0]
    act = lambda z: jnp.sin(freq * z)
    hid = act(feats @ w1 + b1)
    hid = act(hid @ w2 + b2)
    hid = act(hid @ w3 + b3)
    filt = (hid @ w_out).reshape(L, 2, D_B)
    return filt * jnp.exp(-t[:, None, None] * jnp.abs(decay))


def bidirectional_fftconv(z, h_fwd, h_bwd):
    L = z.shape[1]
    n = 2 * L
    k_full = jnp.concatenate([h_fwd, jnp.zeros_like(h_fwd[:1]), h_bwd[:0:-1]], axis=0)
    k_f = jnp.fft.rfft(k_full.astype(jnp.float32), n=n, axis=0)
    z_f = jnp.fft.rfft(z.astype(jnp.float32), n=n, axis=1)
    return jnp.fft.irfft(z_f * k_f[None], n=n, axis=1)[:, :L]


def hyena_branch(u, zb, conv_w, conv_b, filt, bias):
    uc = short_conv(u, conv_w, conv_b)
    v, x0, x1 = jnp.split(uc, 3, axis=-1)
    z = (v * x1).astype(jnp.float32)
    z = bidirectional_fftconv(z, filt[:, 0], filt[:, 1]) + bias * z
    return (x0 * z * jax.nn.silu(zb)).astype(u.dtype)


def rwkv7_scan(r, w, k, v, kk, a, reverse):
    B, S, H, N = r.shape

    def step(state, inp):
        r_t, w_t, k_t, v_t, kk_t, a_t = inp
        sa = jnp.einsum('bhij,bhj->bhi', state, -kk_t)
        state = (state * w_t[:, :, None, :] + sa[..., None] * (kk_t * a_t)[:, :, None, :]
                 + v_t[..., None] * k_t[:, :, None, :])
        return state, jnp.einsum('bhij,bhj->bhi', state, r_t)

    xs = tuple(jnp.moveaxis(t.astype(jnp.float32), 1, 0) for t in (r, w, k, v, kk, a))
    _, ys = lax.scan(step, jnp.zeros((B, H, N, N), jnp.float32), xs, reverse=reverse)
    return jnp.moveaxis(ys, 0, 1)


def rwkv_branch(cu, zc, mu, w0, w2, a0, a2, kk_s, ka, rk, lnx_g, lnx_b):
    B, S, _ = cu.shape
    f32 = jnp.float32
    u = neighbour_shift(cu, mu)
    r, k, v, lw, la = split_cols(u, (D_C, D_C, D_C, 2 * LORA_W, 2 * LORA_A))
    lw = lw.reshape(B, S, 2, LORA_W)
    la = la.reshape(B, S, 2, LORA_A)
    w_log = -jax.nn.softplus(-(w0 + jnp.einsum('bsdr,drc->bsdc', jnp.tanh(lw), w2)).astype(f32)) - 0.5
    decay = jnp.exp(-jnp.exp(w_log))
    a = jax.nn.sigmoid((a0 + jnp.einsum('bsdr,drc->bsdc', la, a2)).astype(f32))
    heads = lambda t: t.reshape(t.shape[:-1] + (H_C, HEAD_C))
    kk = heads((k * kk_s).astype(f32))
    kk = kk / jnp.maximum(jnp.sqrt(jnp.sum(kk * kk, axis=-1, keepdims=True)), 1e-12)
    k_dir = heads(k[:, :, None, :].astype(f32) * (1.0 + (a - 1.0) * ka))
    a_h, dec_h = heads(a), heads(decay)
    r_h, v_h = heads(r), heads(v)
    y_fwd = rwkv7_scan(r_h, dec_h[:, :, 0], k_dir[:, :, 0], v_h, kk, a_h[:, :, 0], reverse=False)
    y_bwd = rwkv7_scan(r_h, dec_h[:, :, 1], k_dir[:, :, 1], v_h, kk, a_h[:, :, 1], reverse=True)
    y = head_norm(y_fwd + y_bwd, LNX_EPS) * heads(lnx_g) + heads(lnx_b)
    k_bonus = 0.5 * (k_dir[:, :, 0] + k_dir[:, :, 1])
    bonus = jnp.sum(r_h * k_bonus * rk, axis=-1, keepdims=True) * v_h
    y = (y + bonus).reshape(B, S, D_C)
    return (y * jax.nn.silu(zc)).astype(cu.dtype)


def setup_inputs(seed: int = 0) -> dict:
    key = jax.random.key(seed)
    ks = iter(jax.random.split(key, 64))
    nrm = lambda shape, s: s * jax.random.normal(next(ks), shape, jnp.float32)
    uni = lambda shape, lo, hi: jax.random.uniform(next(ks), shape, jnp.float32, lo, hi)
    Dp = DEPTH
    D = D_MODEL
    hy_dmax = -math.log(HY_TARGET) / HY_FAST_PCT
    hy_dmin = -math.log(HY_TARGET) / HY_SLOW_PCT
    return {
        'x': nrm((BATCH, SEQ, D), 1.0),
        'c': nrm((BATCH, D), 1.0),
        'ada_w': nrm((Dp, D, 3 * D), 0.5 * D ** -0.5),
        'ada_b': nrm((Dp, 3 * D), 0.01),
        'pre_g': 1.0 + nrm((Dp, D), 0.05),
        'post_g': 1.0 + nrm((Dp, D), 0.05),
        'w_in': nrm((Dp, D, N_IN), D ** -0.5),
        'ml_conv_w': nrm((Dp, SHORT_CONV, D_A), SHORT_CONV ** -0.5),
        'ml_conv_b': nrm((Dp, D_A), 0.01),
        'ml_wq': nrm((Dp, H_A, DH_A, DH_A), DH_A ** -0.5),
        'ml_wk': nrm((Dp, H_A, DH_A, DH_A), DH_A ** -0.5),
        'ml_wv': nrm((Dp, H_A, DH_A, DH_A), DH_A ** -0.5),
        'ml_gate_w': nrm((Dp, 3 * D_A, 4 * H_A), 0.3 * (3 * D_A) ** -0.5),
        'ml_gate_b': jnp.concatenate([nrm((Dp, H_A), 0.1), uni((Dp, H_A), 3.0, 6.0),
                                      nrm((Dp, H_A), 0.1), uni((Dp, H_A), 3.0, 6.0)], axis=-1),
        'ml_norm_g': 1.0 + nrm((Dp, D_A), 0.05),
        'ml_skip': 1.0 + nrm((Dp, D_A), 0.05),
        'hy_conv_w': nrm((Dp, SHORT_CONV, 3 * D_B), SHORT_CONV ** -0.5),
        'hy_conv_b': nrm((Dp, 3 * D_B), 0.01),
        'hy_w1': nrm((Dp, HY_EMB, HY_HID), HY_EMB ** -0.5),
        'hy_b1': nrm((Dp, HY_HID), 0.1),
        'hy_w2': nrm((Dp, HY_HID, HY_HID), HY_HID ** -0.5),
        'hy_b2': nrm((Dp, HY_HID), 0.1),
        'hy_w3': nrm((Dp, HY_HID, HY_HID), HY_HID ** -0.5),
        'hy_b3': nrm((Dp, HY_HID), 0.1),
        'hy_freq': 1.0 + nrm((Dp, HY_HID), 0.1),
        'hy_w_out': nrm((Dp, HY_HID, 2 * D_B), 0.004),
        'hy_decay': uni((Dp, 2, D_B), hy_dmin, hy_dmax),
        'hy_bias': nrm((Dp, D_B), 1.0),
        'rw_mu': uni((Dp, N_SHIFT_C), 0.0, 1.0),
        'rw_w0': uni((Dp, 2, D_C), -6.5, -1.5),
        'rw_w2': nrm((Dp, 2, LORA_W, D_C), 0.1 * LORA_W ** -0.5),
        'rw_a0': nrm((Dp, 2, D_C), 0.1),
        'rw_a2': nrm((Dp, 2, LORA_A, D_C), 0.1 * LORA_A ** -0.5),
        'rw_kk': 0.85 + nrm((Dp, D_C), 0.05),
        'rw_ka': 1.0 + nrm((Dp, D_C), 0.05),
        'rw_rk': nrm((Dp, H_C, HEAD_C), 0.1),
        'rw_lnx_g': 1.0 + nrm((Dp, D_C), 0.05),
        'rw_lnx_b': nrm((Dp, D_C), 0.01),
        'w_branch_a': nrm((Dp, D_A, D), D_A ** -0.5),
        'w_branch_b': nrm((Dp, D_B, D), D_B ** -0.5),
        'w_branch_c': nrm((Dp, D_C, D), D_C ** -0.5),
        'w_out': nrm((Dp, D, D), D ** -0.5),
    }


def reference(x, c, ada_w, ada_b, pre_g, post_g, w_in,
              ml_conv_w, ml_conv_b, ml_wq, ml_wk, ml_wv, ml_gate_w, ml_gate_b, ml_norm_g, ml_skip,
              hy_conv_w, hy_conv_b, hy_w1, hy_b1, hy_w2, hy_b2, hy_w3, hy_b3, hy_freq, hy_w_out,
              hy_decay, hy_bias,
              rw_mu, rw_w0, rw_w2, rw_a0, rw_a2, rw_kk, rw_ka, rw_rk, rw_lnx_g, rw_lnx_b,
              w_branch_a, w_branch_b, w_branch_c, w_out):
    B, S, D = x.shape
    feats, t = hyena_positional_features(S)
    cond = jax.nn.silu(c)
    for l in range(DEPTH):
        mod = cond @ ada_w[l] + ada_b[l]
        shift, scale, gate = jnp.split(mod[:, None, :], 3, axis=-1)
        h = rmsnorm(x, pre_g[l]) * (1.0 + scale) + shift
        proj = h @ w_in[l]
        a_x, a_z, b_u, b_z, c_u, c_z, g_all = split_cols(
            proj, (D_A, D_A, 3 * D_B, D_B, N_SHIFT_C, D_C, 3 * D_MODEL))
        y_a = mlstm_branch(a_x, a_z, ml_conv_w[l], ml_conv_b[l], ml_wq[l], ml_wk[l], ml_wv[l],
                           ml_gate_w[l], ml_gate_b[l], ml_norm_g[l], ml_skip[l])
        filt = hyena_filter(feats, t, hy_w1[l], hy_b1[l], hy_w2[l], hy_b2[l], hy_w3[l], hy_b3[l],
                            hy_freq[l], hy_w_out[l], hy_decay[l])
        y_b = hyena_branch(b_u, b_z, hy_conv_w[l], hy_conv_b[l], filt, hy_bias[l])
        y_c = rwkv_branch(c_u, c_z, rw_mu[l], rw_w0[l], rw_w2[l], rw_a0[l], rw_a2[l], rw_kk[l],
                          rw_ka[l], rw_rk[l], rw_lnx_g[l], rw_lnx_b[l])
        g_a, g_b, g_c = jnp.split(jax.nn.sigmoid(g_all), 3, axis=-1)
        merged = (g_a * (y_a @ w_branch_a[l]) + g_b * (y_b @ w_branch_b[l])
                  + g_c * (y_c @ w_branch_c[l]))
        out = merged @ w_out[l]
        x = x + gate * rmsnorm(out, post_g[l])
    return x
```

```python
import functools
import math

import numpy as np
import jax
import jax.numpy as jnp
from jax import lax
from jax.experimental import pallas as pl
from jax.experimental.pallas import tpu as pltpu

D_MODEL = 1024
DEPTH = 4
D_A = 512
H_A = 4
DH_A = 128
CHUNK_A = 64
D_B = 512
HY_BANDS = 16
HY_HID = 64
D_C = 1024
HEAD_C = 64
H_C = D_C // HEAD_C
LORA = 64
CHUNK_C = 64
LNX_EPS = 64e-5
NORM_EPS = 1e-6
HEAD_NORM_EPS = 1e-5

LANES = 128
SUBLANES = 8
VMEM_LIMIT = 56 * 1024 * 1024

F32 = jnp.float32
BF16 = jnp.bfloat16
HIGHEST = lax.Precision.HIGHEST

N_H = D_A + 3 * D_B + 3 * D_C + 4 * LORA
N_G = D_A + D_B + D_C + 3 * D_MODEL
H_AX, H_BV, H_BX0, H_BX1, H_CR, H_CK, H_CV, H_LW, H_LA = (
    0, 512, 1024, 1536, 2048, 3072, 4096, 5120, 5248)
G_AZ, G_BZ, G_CZ, G_GA, G_GB, G_GC = 0, 512, 1024, 2048, 3072, 4096


def _ref_column_perm():
    r = lambda a, n: np.arange(a, a + n)
    a_x, a_z = r(0, 512), r(512, 512)
    b_v, b_x0, b_x1, b_z = r(1024, 512), r(1536, 512), r(2048, 512), r(2560, 512)
    c_r, c_k, c_v = r(3072, 1024), r(4096, 1024), r(5120, 1024)
    c_lw, c_la = r(6144, 128), r(6272, 128)
    c_z = r(6400, 1024)
    g_a, g_b, g_c = r(7424, 1024), r(8448, 1024), r(9472, 1024)
    h = np.concatenate([a_x, b_v, b_x0, b_x1, c_r, c_k, c_v, c_lw, c_la])
    g = np.concatenate([a_z, b_z, c_z, g_a, g_b, g_c])
    return h, g


def _cparams(sem):
    return pltpu.CompilerParams(dimension_semantics=sem, vmem_limit_bytes=VMEM_LIMIT)


def _silu(x):
    return x * (1.0 / (1.0 + jnp.exp(-x)))


def _sigmoid(x):
    return 1.0 / (1.0 + jnp.exp(-x))


def _bdot(a, b):
    return jnp.dot(a.astype(BF16), b.astype(BF16), preferred_element_type=F32)


def _bdot_nt(a, b):
    return lax.dot_general(a.astype(BF16), b.astype(BF16), (((1,), (1,)), ((), ())),
                           preferred_element_type=F32)


def _hdot(a, b):
    return jnp.dot(a, b, preferred_element_type=F32, precision=HIGHEST)


def _mod_kernel(c_ref, w_ref, b_ref, o_ref):
    cond = _silu(c_ref[...])
    o_ref[0] = _hdot(cond, w_ref[0]) + b_ref[0]


def _modulation(c, ada_w, ada_b):
    depth, d, n3 = ada_w.shape
    bsz = c.shape[0]
    nt = n3 // d
    return pl.pallas_call(
        _mod_kernel,
        grid=(depth, nt),
        in_specs=[
            pl.BlockSpec((bsz, d), lambda l, j: (0, 0)),
            pl.BlockSpec((1, d, d), lambda l, j: (l, 0, j)),
            pl.BlockSpec((1, 1, d), lambda l, j: (l, 0, j)),
        ],
        out_specs=pl.BlockSpec((1, bsz, d), lambda l, j: (l, 0, j)),
        out_shape=jax.ShapeDtypeStruct((depth, bsz, n3), F32),
        compiler_params=_cparams(("arbitrary", "arbitrary")),
        name="adaln_mod",
    )(c, ada_w, ada_b.reshape(depth, 1, n3))


def _inproj_kernel(x_ref, g_ref, shift_ref, scale_ref, w_ref, o_ref, h_ref):
    @pl.when(pl.program_id(1) == 0)
    def _():
        x = x_ref[...]
        y = x * lax.rsqrt(jnp.mean(x * x, axis=-1, keepdims=True) + NORM_EPS)
        h = y * g_ref[...] * (1.0 + scale_ref[0]) + shift_ref[0]
        h_ref[...] = h.astype(BF16)

    o_ref[...] = jnp.dot(h_ref[...], w_ref[...], preferred_element_type=F32)


def _inproj(x2d, seq, pre_g, shift, scale, w_bf16, tm, tn):
    rows, d = x2d.shape
    n = w_bf16.shape[1]
    per_b = seq // tm
    return pl.pallas_call(
        _inproj_kernel,
        grid=(rows // tm, n // tn),
        in_specs=[
            pl.BlockSpec((tm, d), lambda i, j: (i, 0)),
            pl.BlockSpec((1, d), lambda i, j: (0, 0)),
            pl.BlockSpec((1, 1, d), lambda i, j: (i // per_b, 0, 0)),
            pl.BlockSpec((1, 1, d), lambda i, j: (i // per_b, 0, 0)),
            pl.BlockSpec((d, tn), lambda i, j: (0, j)),
        ],
        out_specs=pl.BlockSpec((tm, tn), lambda i, j: (i, j)),
        out_shape=jax.ShapeDtypeStruct((rows, n), F32),
        scratch_shapes=[pltpu.VMEM((tm, d), BF16)],
        compiler_params=_cparams(("arbitrary", "arbitrary")),
        name="inproj",
    )(x2d, pre_g.reshape(1, d), shift, scale, w_bf16)


def _log_sigmoid(x):
    return jnp.minimum(x, 0.0) - jnp.log(1.0 + jnp.exp(-jnp.abs(x)))


def _prep_kernel(main_ref, prev_ref, next_ref,
                 mcw_ref, mcb_ref, wq_ref, wk_ref, wv_ref, gw_ref, gb_ref,
                 hcw_ref, hcb_ref, mu_ref,
                 q_ref, k_ref, v_ref, xc_ref, gcol_ref, z_ref, x0_ref, u_ref, ul_ref):
    i = pl.program_id(1)
    ts = main_ref.shape[0]
    has_prev = jnp.where(i > 0, 1.0, 0.0).astype(F32)
    has_next = jnp.where(i < pl.num_programs(1) - 1, 1.0, 0.0).astype(F32)

    def neighbours(c0, cw):
        x = main_ref[:, c0:c0 + cw]
        row = lax.broadcasted_iota(jnp.int32, (ts, cw), 0)
        p_row = prev_ref[SUBLANES - 1:SUBLANES, c0:c0 + cw] * has_prev
        n_row = next_ref[0:1, c0:c0 + cw] * has_next
        xp = jnp.where(row == 0, p_row, pltpu.roll(x, 1, 0))
        xn = jnp.where(row == ts - 1, n_row, pltpu.roll(x, ts - 1, 0))
        return xp, x, xn

    def conv(c0, cw, w_ref, b_ref, w0):
        xp, x, xn = neighbours(c0, cw)
        w = w_ref[:, w0:w0 + cw]
        return xp * w[0:1] + x * w[1:2] + xn * w[2:3] + b_ref[:, w0:w0 + cw], x

    conv_a, xa = conv(H_AX, D_A, mcw_ref, mcb_ref, 0)
    xc = _silu(conv_a)
    xc_ref[0] = xc
    gates = jnp.zeros((ts, LANES), F32) + gb_ref[...]
    for h in range(H_A):
        sl = slice(h * DH_A, (h + 1) * DH_A)
        qh = _bdot(xc[:, sl], wq_ref[h])
        kh = _bdot(xc[:, sl], wk_ref[h])
        vh = _bdot(xa[:, sl], wv_ref[h])
        gates += (_bdot(qh, gw_ref[h * DH_A:(h + 1) * DH_A])
                  + _bdot(kh, gw_ref[D_A + h * DH_A:D_A + (h + 1) * DH_A])
                  + _bdot(vh, gw_ref[2 * D_A + h * DH_A:2 * D_A + (h + 1) * DH_A]))
        q_ref[0, :, sl] = qh.astype(BF16)
        k_ref[0, :, sl] = (kh * (DH_A ** -0.5)).astype(BF16)
        v_ref[0, :, sl] = vh.astype(BF16)
    col = lax.broadcasted_iota(jnp.int32, (ts, LANES), 1)
    rmod = lax.broadcasted_iota(jnp.int32, (ts, LANES), 0) % CHUNK_A
    lf = _log_sigmoid(gates)
    cf = lf
    cb = lf
    sh = 1
    while sh < CHUNK_A:
        cf = cf + jnp.where(rmod >= sh, pltpu.roll(cf, sh, 0), 0.0)
        cb = cb + jnp.where(rmod < CHUNK_A - sh, pltpu.roll(cb, ts - sh, 0), 0.0)
        sh *= 2
    is_ff = (col >= H_A) & (col < 2 * H_A)
    is_fb = (col >= 3 * H_A) & (col < 4 * H_A)
    gcol_ref[0] = jnp.where(is_ff, cf, jnp.where(is_fb, cb, gates))

    cv, _ = conv(H_BV, D_B, hcw_ref, hcb_ref, 0)
    cx0, _ = conv(H_BX0, D_B, hcw_ref, hcb_ref, D_B)
    cx1, _ = conv(H_BX1, D_B, hcw_ref, hcb_ref, 2 * D_B)
    z_ref[0] = cv * cx1
    x0_ref[0] = cx0

    cw = 512
    for j in range(3 * D_C // cw):
        xp, x, xn = neighbours(H_CR + j * cw, cw)
        mu = mu_ref[:, j * cw:(j + 1) * cw]
        u_ref[0, :, j * cw:(j + 1) * cw] = x + mu * (0.5 * (xp + xn) - x)
    xp, x, xn = neighbours(H_LW, 4 * LORA)
    mu = mu_ref[:, 3 * D_C:3 * D_C + 4 * LORA]
    ul_ref[0] = x + mu * (0.5 * (xp + xn) - x)


def _prep(proj_h, bsz, seq, ts, ml_conv_w, ml_conv_b, wq, wk, wv, gate_w_pad, gate_b_pad,
          hy_conv_w, hy_conv_b, rw_mu):
    ns = seq // ts
    hb = ts // SUBLANES
    last_hb = bsz * seq // SUBLANES - 1
    full = lambda a: pl.BlockSpec(a.shape, lambda b, i: (0,) * a.ndim)
    seq_spec = lambda w: pl.BlockSpec((1, ts, w), lambda b, i: (b, i, 0))
    params = (ml_conv_w, ml_conv_b, wq, wk, wv, gate_w_pad, gate_b_pad, hy_conv_w, hy_conv_b, rw_mu)
    outs = [(D_A, BF16), (D_A, BF16), (D_A, BF16), (D_A, F32), (LANES, F32),
            (D_B, F32), (D_B, F32), (3 * D_C, F32), (4 * LORA, F32)]
    return pl.pallas_call(
        _prep_kernel,
        grid=(bsz, ns),
        in_specs=[
            pl.BlockSpec((ts, N_H), lambda b, i: (b * ns + i, 0)),
            pl.BlockSpec((SUBLANES, N_H), lambda b, i: (jnp.maximum((b * ns + i) * hb - 1, 0), 0)),
            pl.BlockSpec((SUBLANES, N_H), lambda b, i: (jnp.minimum((b * ns + i + 1) * hb, last_hb), 0)),
        ] + [full(p) for p in params],
        out_specs=[seq_spec(w) for w, _ in outs],
        out_shape=[jax.ShapeDtypeStruct((bsz, seq, w), dt) for w, dt in outs],
        compiler_params=_cparams(("arbitrary", "arbitrary")),
        name="prep",
    )(proj_h, proj_h, proj_h, *params)


def _mlstm_kernel(q_ref, k_ref, v_ref, xc_ref, za_ref, gcol_ref, ng_ref, sk_ref, o_ref,
                  h_ref, ct_ref, n_ref, m_ref):
    head = pl.program_id(1)
    seq = q_ref.shape[1]
    L = CHUNK_A
    nc = seq // L
    row = lax.broadcasted_iota(jnp.int32, (L, L), 0)
    colm = lax.broadcasted_iota(jnp.int32, (L, L), 1)
    lane = lax.broadcasted_iota(jnp.int32, (L, LANES), 1)
    rowl = lax.broadcasted_iota(jnp.int32, (L, 1), 0)

    def run(direction):
        fwd = direction == 0
        tri = (colm <= row) if fwd else (colm >= row)
        li_lane = head + 2 * H_A * direction
        b_lane = li_lane + H_A
        end_row = L - 1 if fwd else 0
        ct_ref[...] = jnp.zeros_like(ct_ref)
        n_ref[...] = jnp.zeros_like(n_ref)
        m_ref[...] = jnp.full_like(m_ref, -jnp.inf)

        def chunk(i, carry):
            j = i if fwd else nc - 1 - i
            c0 = pl.multiple_of(j * L, L)
            q = q_ref[0, pl.ds(c0, L), :]
            k = k_ref[0, pl.ds(c0, L), :]
            v = v_ref[0, pl.ds(c0, L), :]
            g = gcol_ref[0, pl.ds(c0, L), :]
            li = jnp.sum(jnp.where(lane == li_lane, g, 0.0), axis=-1, keepdims=True)
            b = jnp.sum(jnp.where(lane == b_lane, g, 0.0), axis=-1, keepdims=True)
            xl = jnp.where(lane == 0, b, jnp.where(lane == 1, 1.0, 0.0))
            yl = jnp.where(lane == 0, 1.0, jnp.where(lane == 1, li - b, 0.0))
            d = lax.dot_general(xl, yl, (((1,), (1,)), ((), ())), precision=HIGHEST,
                                preferred_element_type=F32)
            d = jnp.where(tri, d, -jnp.inf)
            m = m_ref[...]
            inter = b + m
            m_t = jnp.maximum(inter, jnp.max(d, axis=-1, keepdims=True))
            w_intra = jnp.exp(d - m_t)
            w_inter = jnp.exp(inter - m_t)
            s = _bdot_nt(q, k) * w_intra
            ct = ct_ref[...]
            n = n_ref[...]
            num = _bdot(s, v) + w_inter * _bdot(q, ct)
            den = (jnp.sum(s, axis=-1, keepdims=True)
                   + w_inter * jnp.sum(q.astype(F32) * n, axis=-1, keepdims=True))
            h = num / jnp.maximum(jnp.abs(den), jnp.exp(-m_t))
            if fwd:
                h_ref[pl.ds(c0, L), :] = h
            else:
                h_ref[pl.ds(c0, L), :] += h
            gtot = jnp.sum(jnp.where(rowl == end_row, b, 0.0), axis=0, keepdims=True)
            a = gtot - b + li
            m_new = jnp.maximum(gtot + m, jnp.max(a, axis=0, keepdims=True))
            decay = jnp.exp(gtot + m - m_new)
            wk = jnp.exp(a - m_new)
            vw = (v.astype(F32) * wk).astype(BF16)
            ct_ref[...] = decay * ct + lax.dot_general(
                k, vw, (((0,), (0,)), ((), ())), preferred_element_type=F32)
            n_ref[...] = decay * n + jnp.sum(k.astype(F32) * wk, axis=0, keepdims=True)
            m_ref[...] = m_new
            return carry

        lax.fori_loop(0, nc, chunk, 0)

    run(0)
    run(1)

    tile = 256
    def finish(i, carry):
        r0 = pl.multiple_of(i * tile, tile)
        hh = h_ref[pl.ds(r0, tile), :]
        mu = jnp.mean(hh, axis=-1, keepdims=True)
        hc = hh - mu
        var = jnp.mean(hc * hc, axis=-1, keepdims=True)
        hn = hc * lax.rsqrt(var + HEAD_NORM_EPS) * ng_ref[...]
        out = (hn + sk_ref[...] * xc_ref[0, pl.ds(r0, tile), :]) * _silu(za_ref[pl.ds(r0, tile), :])
        o_ref[0, pl.ds(r0, tile), :] = out
        return carry

    lax.fori_loop(0, seq // tile, finish, 0)


def _mlstm(q, k, v, xc, proj_g, gcol, norm_g, skip):
    bsz, seq, _ = q.shape
    hs = lambda dt_unused=None: pl.BlockSpec((1, seq, DH_A), lambda b, h: (b, 0, h))
    return pl.pallas_call(
        _mlstm_kernel,
        grid=(bsz, H_A),
        in_specs=[
            hs(), hs(), hs(), hs(),
            pl.BlockSpec((seq, DH_A), lambda b, h: (b, G_AZ // DH_A + h)),
            pl.BlockSpec((1, seq, LANES), lambda b, h: (b, 0, 0)),
            pl.BlockSpec((1, DH_A), lambda b, h: (0, h)),
            pl.BlockSpec((1, DH_A), lambda b, h: (0, h)),
        ],
        out_specs=pl.BlockSpec((1, seq, DH_A), lambda b, h: (b, 0, h)),
        out_shape=jax.ShapeDtypeStruct((bsz, seq, D_A), F32),
        scratch_shapes=[pltpu.VMEM((seq, DH_A), F32), pltpu.VMEM((DH_A, DH_A), F32),
                        pltpu.VMEM((1, DH_A), F32), pltpu.VMEM((1, 1), F32)],
        compiler_params=_cparams(("arbitrary", "arbitrary")),
        name="mlstm",
    )(q, k, v, xc, proj_g, gcol, norm_g.reshape(1, D_A), skip.reshape(1, D_A))


FFT_N2 = 128
HY_CT = 128


def _split_bf16(x):
    hi = x.astype(BF16)
    lo = (x - hi.astype(F32)).astype(BF16)
    return hi, lo


def _dot3(a_hi, a_lo, x):
    x_hi, x_lo = _split_bf16(x)
    d = lambda a, b: jnp.dot(a, b, preferred_element_type=F32)
    return d(a_hi, x_hi) + (d(a_lo, x_hi) + d(a_hi, x_lo))


def _dft_constants(seq):
    n = 2 * seq
    n2 = FFT_N2
    n1 = n // n2
    f1 = np.arange(n1)[:, None]
    s1 = np.arange(n1)[None, :]
    th1 = 2.0 * np.pi * f1 * s1 / n1
    fwd1 = np.concatenate([np.cos(th1), -np.sin(th1)], axis=0)
    inv1 = np.concatenate([np.cos(th1), -np.sin(th1)], axis=1)[:n1 // 2] / n
    a = np.arange(n2)
    th2 = 2.0 * np.pi * a[:, None] * a[None, :] / n2
    c2, s2 = np.cos(th2), np.sin(th2)
    fwd2 = np.block([[c2, s2], [-s2, c2]])
    inv2 = np.block([[c2, -s2], [s2, c2]])
    tht = 2.0 * np.pi * a[:, None] * np.arange(n1)[None, :] / n
    tw = np.zeros((2, n2, LANES), np.float64)
    tw[0, :, :n1] = np.cos(tht)
    tw[1, :, :n1] = -np.sin(tht)

    def hl(m):
        m32 = jnp.asarray(m, F32)
        hi = m32.astype(BF16)
        lo = (m32 - hi.astype(F32)).astype(BF16)
        return jnp.stack([hi, lo])

    return dict(fwd1=hl(fwd1), inv1=hl(inv1), fwd2=hl(fwd2), inv2=hl(inv2),
                tw=jnp.asarray(tw, F32), n1=n1)


def _fft_stage_a(load_rows, k_rows, fwd1_ref, a_ref, n1):
    n2 = FFT_N2
    f_hi = fwd1_ref[0, :, :k_rows]
    f_lo = fwd1_ref[1, :, :k_rows]

    def body(s2, carry):
        m = _dot3(f_hi, f_lo, load_rows(s2))
        a_ref[pl.ds(s2, n1, stride=2 * n2), :] = m[:n1]
        a_ref[pl.ds(n2 + s2, n1, stride=2 * n2), :] = m[n1:]
        return carry

    lax.fori_loop(0, n2, body, 0)


def _twiddle_cols(tw_ref, f1):
    lane = lax.broadcasted_iota(jnp.int32, (FFT_N2, LANES), 1)
    sel = lane == f1
    twr = jnp.sum(jnp.where(sel, tw_ref[0], 0.0), axis=-1, keepdims=True)
    twi = jnp.sum(jnp.where(sel, tw_ref[1], 0.0), axis=-1, keepdims=True)
    return twr, twi


def _hyena_filter_kernel(feat_ref, w1_ref, b1_ref, w2_ref, b2_ref, w3_ref, b3_ref, fr_ref,
                         wo_ref, dec_ref, fwd1_ref, fwd2_ref, tw_ref, o_ref, kt_ref, a_ref):
    n = feat_ref.shape[0]
    seq = n // 2
    n2 = FFT_N2
    n1 = n // n2
    tile = 512
    freq = fr_ref[...]

    def gen(i, carry):
        r0 = pl.multiple_of(i * tile, tile)
        feats = feat_ref[pl.ds(r0, tile), :]
        hid = jnp.sin(freq * (_hdot(feats, w1_ref[...]) + b1_ref[...]))
        hid = jnp.sin(freq * (_hdot(hid, w2_ref[...]) + b2_ref[...]))
        hid = jnp.sin(freq * (_hdot(hid, w3_ref[...]) + b3_ref[...]))
        second = r0 >= seq
        wo = jnp.where(second, wo_ref[1, 0], wo_ref[0, 0])
        dec = jnp.where(second, dec_ref[1, 0], dec_ref[0, 0])
        t = feats[:, 0:1]
        filt = _hdot(hid, wo) * jnp.exp(-t * jnp.abs(dec))
        rows = r0 + lax.broadcasted_iota(jnp.int32, (tile, 1), 0)
        kt_ref[pl.ds(r0, tile), :] = jnp.where(rows == seq, 0.0, filt)
        return carry

    lax.fori_loop(0, n // tile, gen, 0)

    _fft_stage_a(lambda s2: kt_ref[pl.ds(s2, n1, stride=n2), :], n1, fwd1_ref, a_ref, n1)

    def slab(f1, carry):
        r0 = pl.multiple_of(f1 * 2 * n2, 2 * n2)
        ar = a_ref[pl.ds(r0, n2), :]
        ai = a_ref[pl.ds(r0 + n2, n2), :]
        twr, twi = _twiddle_cols(tw_ref, f1)
        x = jnp.concatenate([ar * twr - ai * twi, ar * twi + ai * twr], axis=0)
        o_ref[0, pl.ds(r0, 2 * n2), :] = _dot3(fwd2_ref[0], fwd2_ref[1], x)
        return carry

    lax.fori_loop(0, n1, slab, 0)


def _hyena_filter_spectrum(consts, feats_full, w1p, b1, w2, b2, w3, b3, freq, w_out, decay):
    n = feats_full.shape[0]
    nct = D_B // HY_CT
    full = lambda a: pl.BlockSpec(a.shape, lambda j: (0,) * a.ndim)
    wo = w_out.reshape(HY_HID, 2, nct, HY_CT).transpose(1, 2, 0, 3)
    dec = decay.reshape(2, nct, 1, HY_CT)
    small = (w1p, b1.reshape(1, -1), w2, b2.reshape(1, -1), w3, b3.reshape(1, -1), freq.reshape(1, -1))
    return pl.pallas_call(
        _hyena_filter_kernel,
        grid=(nct,),
        in_specs=[full(feats_full)] + [full(a) for a in small] + [
            pl.BlockSpec((2, 1, HY_HID, HY_CT), lambda j: (0, j, 0, 0)),
            pl.BlockSpec((2, 1, 1, HY_CT), lambda j: (0, j, 0, 0)),
            full(consts['fwd1']), full(consts['fwd2']), full(consts['tw']),
        ],
        out_specs=pl.BlockSpec((1, 2 * n, HY_CT), lambda j: (j, 0, 0)),
        out_shape=jax.ShapeDtypeStruct((nct, 2 * n, HY_CT), F32),
        scratch_shapes=[pltpu.VMEM((n, HY_CT), F32), pltpu.VMEM((2 * n, HY_CT), F32)],
        compiler_params=_cparams(("arbitrary",)),
        name="hyena_filter",
    )(feats_full, *small, wo, dec, consts['fwd1'], consts['fwd2'], consts['tw'])


def _hyena_conv_kernel(z_ref, x0_ref, zb_ref, ks_ref, bias_ref, fwd1_ref, inv1_ref, fwd2_ref,
                       inv2_ref, tw_ref, o_ref, a_ref, y_ref):
    seq = z_ref.shape[1]
    n2 = FFT_N2
    n1 = 2 * seq // n2
    nh = n1 // 2

    _fft_stage_a(lambda s2: z_ref[0, pl.ds(s2, nh, stride=n2), :], nh, fwd1_ref, a_ref, n1)

    def slab(f1, carry):
        r0 = pl.multiple_of(f1 * 2 * n2, 2 * n2)
        ar = a_ref[pl.ds(r0, n2), :]
        ai = a_ref[pl.ds(r0 + n2, n2), :]
        twr, twi = _twiddle_cols(tw_ref, f1)
        x = jnp.concatenate([ar * twr - ai * twi, ar * twi + ai * twr], axis=0)
        xf = _dot3(fwd2_ref[0], fwd2_ref[1], x)
        xr, xi = xf[:n2], xf[n2:]
        kr = ks_ref[0, pl.ds(r0, n2), :]
        ki = ks_ref[0, pl.ds(r0 + n2, n2), :]
        y = jnp.concatenate([xr * kr - xi * ki, xr * ki + xi * kr], axis=0)
        bb = _dot3(inv2_ref[0], inv2_ref[1], y)
        br, bi = bb[:n2], bb[n2:]
        a_ref[pl.ds(r0, n2), :] = br * twr + bi * twi
        a_ref[pl.ds(r0 + n2, n2), :] = bi * twr - br * twi
        return carry

    lax.fori_loop(0, n1, slab, 0)

    i_hi = inv1_ref[0]
    i_lo = inv1_ref[1]

    def stage_d(s2, carry):
        br = a_ref[pl.ds(s2, n1, stride=2 * n2), :]
        bi = a_ref[pl.ds(n2 + s2, n1, stride=2 * n2), :]
        y_ref[pl.ds(s2, nh, stride=n2), :] = _dot3(i_hi, i_lo, jnp.concatenate([br, bi], axis=0))
        return carry

    lax.fori_loop(0, n2, stage_d, 0)

    tile = 256
    def finish(i, carry):
        r0 = pl.multiple_of(i * tile, tile)
        z = z_ref[0, pl.ds(r0, tile), :]
        y = y_ref[pl.ds(r0, tile), :] + bias_ref[...] * z
        o_ref[0, pl.ds(r0, tile), :] = x0_ref[0, pl.ds(r0, tile), :] * y * _silu(zb_ref[pl.ds(r0, tile), :])
        return carry

    lax.fori_loop(0, seq // tile, finish, 0)


def _hyena_conv(consts, z, x0, proj_g, kspec, bias):
    bsz, seq, _ = z.shape
    nct = D_B // HY_CT
    full = lambda a: pl.BlockSpec(a.shape, lambda j, b: (0,) * a.ndim)
    cs = pl.BlockSpec((1, seq, HY_CT), lambda j, b: (b, 0, j))
    mats = (consts['fwd1'], consts['inv1'], consts['fwd2'], consts['inv2'], consts['tw'])
    return pl.pallas_call(
        _hyena_conv_kernel,
        grid=(nct, bsz),
        in_specs=[
            cs, cs,
            pl.BlockSpec((seq, HY_CT), lambda j, b: (b, G_BZ // HY_CT + j)),
            pl.BlockSpec((1, 4 * seq, HY_CT), lambda j, b: (j, 0, 0)),
            pl.BlockSpec((1, HY_CT), lambda j, b: (0, j)),
        ] + [full(m) for m in mats],
        out_specs=cs,
        out_shape=jax.ShapeDtypeStruct((bsz, seq, D_B), F32),
        scratch_shapes=[pltpu.VMEM((4 * seq, HY_CT), F32), pltpu.VMEM((seq, HY_CT), F32)],
        compiler_params=_cparams(("arbitrary", "arbitrary")),
        name="hyena_conv",
    )(z, x0, proj_g, kspec, bias.reshape(1, D_B), *mats)


def _rwkv_kernel(r_ref, k_ref, v_ref, ul_ref, cz_ref, w0_ref, w2_ref, a0_ref, a2_ref,
                 kks_ref, ka_ref, rk_ref, lg_ref, lb_ref, o_ref, y_ref, kb_ref, s_ref):
    seq = r_ref.shape[1]
    L = CHUNK_C
    nc = seq // L
    P2 = 2 * L
    lane = lax.broadcasted_iota(jnp.int32, (L, LANES), 1)
    head0 = lane < HEAD_C
    prow = lax.broadcasted_iota(jnp.int32, (P2, P2), 0)
    pcol = lax.broadcasted_iota(jnp.int32, (P2, P2), 1)
    same = (prow // L) == (pcol // L)
    rt, ct = prow % L, pcol % L
    eye = jnp.where(prow == pcol, 1.0, 0.0).astype(F32)
    blk = jnp.where((prow // HEAD_C) == (pcol // HEAD_C), 1.0, 0.0).astype(F32)
    crow = lax.broadcasted_iota(jnp.int32, (L, L), 0)
    ccol = lax.broadcasted_iota(jnp.int32, (L, L), 1)

    def stack(x):
        return jnp.concatenate([jnp.where(head0, x, 0.0), jnp.where(head0, 0.0, x)], axis=0)

    def run(d):
        fwd = d == 0
        strict = same & ((ct < rt) if fwd else (ct > rt))
        incl = same & ((ct <= rt) if fwd else (ct >= rt))
        cum = jnp.where((ccol <= crow) if fwd else (ccol >= crow), 1.0, 0.0).astype(F32)
        end_row = L - 1 if fwd else 0
        s_ref[...] = jnp.zeros_like(s_ref)

        def chunk(i, carry):
            j = i if fwd else nc - 1 - i
            c0 = pl.multiple_of(j * L, L)
            r = r_ref[0, pl.ds(c0, L), :]
            k = k_ref[0, pl.ds(c0, L), :]
            v = v_ref[0, pl.ds(c0, L), :]
            lwla = ul_ref[0, pl.ds(c0, L), :]
            lw, la = lwla[:, :LANES], lwla[:, LANES:]
            xw = w0_ref[d:d + 1, :] + _bdot(jnp.tanh(lw), w2_ref[d])
            lwt = -jnp.exp(_log_sigmoid(xw) - 0.5)
            a = _sigmoid(a0_ref[d:d + 1, :] + _bdot(la, a2_ref[d]))
            kkv = k * kks_ref[...]
            ss = _hdot(kkv * kkv, blk)
            kk = kkv / jnp.maximum(jnp.sqrt(ss), 1e-12)
            kd = k * (1.0 + (a - 1.0) * ka_ref[...])
            be = kk * a
            g = _hdot(cum, lwt)
            gprev = g - lwt
            eng = jnp.exp(-g)
            egc = jnp.exp(g[end_row:end_row + 1, :])
            at = stack(-kk * jnp.exp(gprev))
            rt_ = r * jnp.exp(g)
            bh, kh = be * eng, kd * eng
            lhs = jnp.concatenate([at, stack(rt_)], axis=0)
            rhs = jnp.concatenate([stack(bh), stack(kh)], axis=0)
            g1 = _bdot_nt(lhs, rhs)
            aab = jnp.where(strict, g1[:P2, :P2], 0.0)
            aak = jnp.where(strict, g1[:P2, P2:], 0.0)
            arb = jnp.where(incl, g1[P2:, :P2], 0.0)
            ark = jnp.where(incl, g1[P2:, P2:], 0.0)
            t = eye + aab
            p = aab
            sq = 2
            while sq < L:
                p = _bdot(p, p)
                t = t + _bdot(t, p)
                sq *= 2
            vs = stack(v)
            w = _bdot(t, at)
            u = _bdot(t, _bdot(aak, vs))
            s = s_ref[...]
            sa = _bdot_nt(w, s) + u
            z = _bdot(arb, sa) + _bdot(ark, vs)
            y = _bdot_nt(rt_, s) + z[:L] + z[L:]
            upd_l = jnp.concatenate([sa, vs], axis=0).astype(BF16)
            upd_r = jnp.concatenate([stack(bh * egc), stack(kh * egc)], axis=0).astype(BF16)
            s_ref[...] = s * egc + lax.dot_general(
                upd_l, upd_r, (((0,), (0,)), ((), ())), preferred_element_type=F32)
            if fwd:
                y_ref[pl.ds(c0, L), :] = y
                kb_ref[pl.ds(c0, L), :] = kd
            else:
                y_ref[pl.ds(c0, L), :] += y
                kb_ref[pl.ds(c0, L), :] += kd
            return carry

        lax.fori_loop(0, nc, chunk, 0)

    run(0)
    run(1)

    tile = 256
    inv_n = 1.0 / HEAD_C
    def finish(i, carry):
        r0 = pl.multiple_of(i * tile, tile)
        y = y_ref[pl.ds(r0, tile), :]
        mu = _hdot(y, blk) * inv_n
        yc = y - mu
        var = _hdot(yc * yc, blk) * inv_n
        yn = yc * lax.rsqrt(var + LNX_EPS) * lg_ref[...] + lb_ref[...]
        r = r_ref[0, pl.ds(r0, tile), :]
        v = v_ref[0, pl.ds(r0, tile), :]
        kbon = 0.5 * kb_ref[pl.ds(r0, tile), :]
        bonus = _hdot(r * kbon * rk_ref[...], blk) * v
        o_ref[0, pl.ds(r0, tile), :] = (yn + bonus) * _silu(cz_ref[pl.ds(r0, tile), :])
        return carry

    lax.fori_loop(0, seq // tile, finish, 0)


def _rwkv(u, ul, proj_g, w0, w2pad, a0, a2pad, kk_s, ka, rk, lnx_g, lnx_b):
    bsz, seq, _ = u.shape
    npair = D_C // LANES
    col = lambda off: pl.BlockSpec((1, seq, LANES), lambda b, p: (b, 0, off + p))
    vec = pl.BlockSpec((1, LANES), lambda b, p: (0, p))
    two = pl.BlockSpec((2, LANES), lambda b, p: (0, p))
    lora = pl.BlockSpec((2, LANES, LANES), lambda b, p: (0, 0, p))
    row = lambda a: a.reshape(1, D_C)
    return pl.pallas_call(
        _rwkv_kernel,
        grid=(bsz, npair),
        in_specs=[
            col(0), col(npair), col(2 * npair),
            pl.BlockSpec((1, seq, 2 * LANES), lambda b, p: (b, 0, 0)),
            pl.BlockSpec((seq, LANES), lambda b, p: (b, G_CZ // LANES + p)),
            two, lora, two, lora, vec, vec, vec, vec, vec,
        ],
        out_specs=pl.BlockSpec((1, seq, LANES), lambda b, p: (b, 0, p)),
        out_shape=jax.ShapeDtypeStruct((bsz, seq, D_C), F32),
        scratch_shapes=[pltpu.VMEM((seq, LANES), F32), pltpu.VMEM((seq, LANES), F32),
                        pltpu.VMEM((LANES, LANES), F32)],
        compiler_params=_cparams(("arbitrary", "arbitrary")),
        name="rwkv7",
    )(u, u, u, ul, proj_g, w0, w2pad, a0, a2pad, row(kk_s), row(ka), row(rk), row(lnx_g), row(lnx_b))


def _out_kernel(ya_ref, yb_ref, yc_ref, ga_ref, gb_ref, gc_ref, x_ref, gate_ref, pg_ref,
                wa_ref, wb_ref, wc_ref, wo_ref, o_ref):
    merged = (_sigmoid(ga_ref[...]) * _bdot(ya_ref[...], wa_ref[...])
              + _sigmoid(gb_ref[...]) * _bdot(yb_ref[...], wb_ref[...])
              + _sigmoid(gc_ref[...]) * _bdot(yc_ref[...], wc_ref[...]))
    out = _bdot(merged, wo_ref[...])
    y = out * lax.rsqrt(jnp.mean(out * out, axis=-1, keepdims=True) + NORM_EPS) * pg_ref[...]
    o_ref[...] = x_ref[...] + gate_ref[0] * y


def _merge_out(ya, yb, yc, proj_g, x2d, seq, gate, post_g, wa, wb, wc, wo, tm):
    rows, d = x2d.shape
    per_b = seq // tm
    full = lambda a: pl.BlockSpec(a.shape, lambda i: (0,) * a.ndim)
    gcol = lambda off: pl.BlockSpec((tm, d), lambda i: (i, off // d))
    return pl.pallas_call(
        _out_kernel,
        grid=(rows // tm,),
        in_specs=[
            pl.BlockSpec((tm, D_A), lambda i: (i, 0)),
            pl.BlockSpec((tm, D_B), lambda i: (i, 0)),
            pl.BlockSpec((tm, D_C), lambda i: (i, 0)),
            gcol(G_GA), gcol(G_GB), gcol(G_GC),
            pl.BlockSpec((tm, d), lambda i: (i, 0)),
            pl.BlockSpec((1, 1, d), lambda i: (i // per_b, 0, 0)),
            pl.BlockSpec((1, d), lambda i: (0, 0)),
            full(wa), full(wb), full(wc), full(wo),
        ],
        out_specs=pl.BlockSpec((tm, d), lambda i: (i, 0)),
        out_shape=jax.ShapeDtypeStruct((rows, d), F32),
        compiler_params=_cparams(("arbitrary",)),
        name="merge_out",
    )(ya.reshape(rows, D_A), yb.reshape(rows, D_B), yc.reshape(rows, D_C), proj_g, proj_g, proj_g,
      x2d, gate, post_g.reshape(1, d), wa, wb, wc, wo)


def _hyena_feats(seq):
    n = 2 * seq
    t = jnp.linspace(0.0, 1.0, seq, dtype=F32)
    w = 2.0 * math.pi * jnp.arange(seq, dtype=F32) / seq
    f = jnp.linspace(1e-4, HY_BANDS - 1, HY_BANDS, dtype=F32)
    zz = w[:, None] * f[None, :]
    feats = jnp.concatenate([t[:, None], jnp.cos(zz), -jnp.sin(zz)], axis=-1)
    pos = np.concatenate([np.arange(seq), [0], np.arange(seq - 1, 0, -1)])
    full = jnp.zeros((n, LANES), F32).at[:, :feats.shape[1]].set(feats[pos])
    return full


def _tiles(seq):
    return min(seq, 1024), min(seq, 256), min(seq, 512)


def kernel(x, c, ada_w, ada_b, pre_g, post_g, w_in, ml_conv_w, ml_conv_b, ml_wq, ml_wk, ml_wv, ml_gate_w, ml_gate_b, ml_norm_g, ml_skip, hy_conv_w, hy_conv_b, hy_w1, hy_b1, hy_w2, hy_b2, hy_w3, hy_b3, hy_freq, hy_w_out, hy_decay, hy_bias, rw_mu, rw_w0, rw_w2, rw_a0, rw_a2, rw_kk, rw_ka, rw_rk, rw_lnx_g, rw_lnx_b, w_branch_a, w_branch_b, w_branch_c, w_out):
    bsz, seq, d = x.shape
    depth = ada_w.shape[0]
    tm_in, ts_prep, tm_out = _tiles(seq)
    perm_h, perm_g = _ref_column_perm()
    consts = _dft_constants(seq)
    feats_full = _hyena_feats(seq)
    mod = _modulation(c, ada_w, ada_b)
    x2d = x.reshape(bsz * seq, d)
    n_gate = 4 * H_A
    for l in range(depth):
        shift = mod[l, :, None, 0:d]
        scale = mod[l, :, None, d:2 * d]
        gate = mod[l, :, None, 2 * d:3 * d]
        w_h = w_in[l][:, perm_h].astype(BF16)
        w_g = w_in[l][:, perm_g].astype(BF16)
        proj_h = _inproj(x2d, seq, pre_g[l], shift, scale, w_h, tm_in, N_H // 3)
        proj_g = _inproj(x2d, seq, pre_g[l], shift, scale, w_g, tm_in, N_G // 4)

        gw = jnp.zeros((3 * D_A, LANES), F32).at[:, :n_gate].set(ml_gate_w[l])
        gb = jnp.zeros((1, LANES), F32).at[0, :n_gate].set(ml_gate_b[l])
        q, k, v, xc, gcol, z, x0, u, ul = _prep(
            proj_h, bsz, seq, ts_prep, ml_conv_w[l], ml_conv_b[l].reshape(1, D_A),
            ml_wq[l], ml_wk[l], ml_wv[l], gw, gb,
            hy_conv_w[l], hy_conv_b[l].reshape(1, 3 * D_B), rw_mu[l].reshape(1, -1))

        y_a = _mlstm(q, k, v, xc, proj_g, gcol, ml_norm_g[l], ml_skip[l])

        w1p = jnp.zeros((LANES, HY_HID), F32).at[:hy_w1.shape[1]].set(hy_w1[l])
        kspec = _hyena_filter_spectrum(consts, feats_full, w1p, hy_b1[l], hy_w2[l], hy_b2[l],
                                       hy_w3[l], hy_b3[l], hy_freq[l], hy_w_out[l], hy_decay[l])
        y_b = _hyena_conv(consts, z, x0, proj_g, kspec, hy_bias[l])

        w2pad = (jnp.zeros((2, LANES, D_C), F32).at[0, :LORA].set(rw_w2[l, 0])
                 .at[1, LORA:].set(rw_w2[l, 1]))
        a2pad = (jnp.zeros((2, LANES, D_C), F32).at[0, :LORA].set(rw_a2[l, 0])
                 .at[1, LORA:].set(rw_a2[l, 1]))
        y_c = _rwkv(u, ul, proj_g, rw_w0[l], w2pad, rw_a0[l], a2pad, rw_kk[l], rw_ka[l],
                    rw_rk[l].reshape(-1), rw_lnx_g[l], rw_lnx_b[l])

        x2d = _merge_out(y_a, y_b, y_c, proj_g, x2d, seq, gate, post_g[l],
                         w_branch_a[l].astype(BF16), w_branch_b[l].astype(BF16),
                         w_branch_c[l].astype(BF16), w_out[l].astype(BF16), tm_out)
    return x2d.reshape(bsz, seq, d)
```

```python
import functools
import math

import numpy as np
import jax
import jax.numpy as jnp
from jax import lax
from jax.experimental import pallas as pl
from jax.experimental.pallas import tpu as pltpu

D_MODEL = 1024
DEPTH = 4
D_A = 512
H_A = 4
DH_A = 128
CHUNK_A = 64
ML_GROUP = 16
D_B = 512
HY_BANDS = 16
HY_HID = 64
D_C = 1024
HEAD_C = 64
H_C = D_C // HEAD_C
LORA = 64
CHUNK_C = 64
RW_GROUP = 16
LNX_EPS = 64e-5
NORM_EPS = 1e-6
HEAD_NORM_EPS = 1e-5

LANES = 128
SUBLANES = 8
VMEM_LIMIT = 56 * 1024 * 1024

F32 = jnp.float32
BF16 = jnp.bfloat16
HIGHEST = lax.Precision.HIGHEST

N_H = D_A + 3 * D_B + 3 * D_C + 4 * LORA
N_G = D_A + D_B + D_C + 3 * D_MODEL
H_AX, H_BV, H_BX0, H_BX1, H_CR, H_CK, H_CV, H_LW, H_LA = (
    0, 512, 1024, 1536, 2048, 3072, 4096, 5120, 5248)
G_AZ, G_BZ, G_CZ, G_GA, G_GB, G_GC = 0, 512, 1024, 2048, 3072, 4096


def _ref_column_perm():
    r = lambda a, n: np.arange(a, a + n)
    a_x, a_z = r(0, 512), r(512, 512)
    b_v, b_x0, b_x1, b_z = r(1024, 512), r(1536, 512), r(2048, 512), r(2560, 512)
    c_r, c_k, c_v = r(3072, 1024), r(4096, 1024), r(5120, 1024)
    c_lw, c_la = r(6144, 128), r(6272, 128)
    c_z = r(6400, 1024)
    g_a, g_b, g_c = r(7424, 1024), r(8448, 1024), r(9472, 1024)
    h = np.concatenate([a_x, b_v, b_x0, b_x1, c_r, c_k, c_v, c_lw, c_la])
    g = np.concatenate([a_z, b_z, c_z, g_a, g_b, g_c])
    return h, g


def _cparams(sem):
    return pltpu.CompilerParams(dimension_semantics=sem, vmem_limit_bytes=VMEM_LIMIT)


def _silu(x):
    return x * (1.0 / (1.0 + jnp.exp(-x)))


def _sigmoid(x):
    return 1.0 / (1.0 + jnp.exp(-x))


def _bdot(a, b):
    return jnp.dot(a.astype(BF16), b.astype(BF16), preferred_element_type=F32)


def _bdot_nt(a, b):
    return lax.dot_general(a.astype(BF16), b.astype(BF16), (((1,), (1,)), ((), ())),
                           preferred_element_type=F32)


def _round_robin(gens):
    gens = list(gens)
    while gens:
        alive = []
        for g in gens:
            try:
                next(g)
                alive.append(g)
            except StopIteration:
                pass
        gens = alive


def _hdot(a, b):
    return jnp.dot(a, b, preferred_element_type=F32, precision=HIGHEST)


def _mod_kernel(c_ref, w_ref, b_ref, o_ref):
    cond = _silu(c_ref[...])
    o_ref[0] = _hdot(cond, w_ref[0]) + b_ref[0]


def _modulation(c, ada_w, ada_b):
    depth, d, n3 = ada_w.shape
    bsz = c.shape[0]
    nt = n3 // d
    return pl.pallas_call(
        _mod_kernel,
        grid=(depth, nt),
        in_specs=[
            pl.BlockSpec((bsz, d), lambda l, j: (0, 0)),
            pl.BlockSpec((1, d, d), lambda l, j: (l, 0, j)),
            pl.BlockSpec((1, 1, d), lambda l, j: (l, 0, j)),
        ],
        out_specs=pl.BlockSpec((1, bsz, d), lambda l, j: (l, 0, j)),
        out_shape=jax.ShapeDtypeStruct((depth, bsz, n3), F32),
        compiler_params=_cparams(("arbitrary", "arbitrary")),
        name="adaln_mod",
    )(c, ada_w, ada_b.reshape(depth, 1, n3))


def _inproj_kernel(x_ref, g_ref, shift_ref, scale_ref, w_ref, o_ref, h_ref):
    @pl.when(pl.program_id(1) == 0)
    def _():
        x = x_ref[...]
        y = x * lax.rsqrt(jnp.mean(x * x, axis=-1, keepdims=True) + NORM_EPS)
        h = y * g_ref[...] * (1.0 + scale_ref[0]) + shift_ref[0]
        h_ref[...] = h.astype(BF16)

    o_ref[...] = jnp.dot(h_ref[...], w_ref[...], preferred_element_type=F32)


def _inproj(x2d, seq, pre_g, shift, scale, w_bf16, tm, tn):
    rows, d = x2d.shape
    n = w_bf16.shape[1]
    per_b = seq // tm
    return pl.pallas_call(
        _inproj_kernel,
        grid=(rows // tm, n // tn),
        in_specs=[
            pl.BlockSpec((tm, d), lambda i, j: (i, 0)),
            pl.BlockSpec((1, d), lambda i, j: (0, 0)),
            pl.BlockSpec((1, 1, d), lambda i, j: (i // per_b, 0, 0)),
            pl.BlockSpec((1, 1, d), lambda i, j: (i // per_b, 0, 0)),
            pl.BlockSpec((d, tn), lambda i, j: (0, j)),
        ],
        out_specs=pl.BlockSpec((tm, tn), lambda i, j: (i, j)),
        out_shape=jax.ShapeDtypeStruct((rows, n), F32),
        scratch_shapes=[pltpu.VMEM((tm, d), BF16)],
        compiler_params=_cparams(("arbitrary", "arbitrary")),
        name="inproj",
    )(x2d, pre_g.reshape(1, d), shift, scale, w_bf16)


def _log_sigmoid(x):
    return jnp.minimum(x, 0.0) - jnp.log(1.0 + jnp.exp(-jnp.abs(x)))


def _prep_kernel(main_ref, prev_ref, next_ref,
                 mcw_ref, mcb_ref, wq_ref, wk_ref, wv_ref, gw_ref, gb_ref,
                 hcw_ref, hcb_ref, mu_ref,
                 q_ref, k_ref, v_ref, xc_ref, gcol_ref, z_ref, x0_ref, u_ref, ul_ref):
    i = pl.program_id(1)
    ts = main_ref.shape[0]
    has_prev = jnp.where(i > 0, 1.0, 0.0).astype(F32)
    has_next = jnp.where(i < pl.num_programs(1) - 1, 1.0, 0.0).astype(F32)

    def neighbours(c0, cw):
        x = main_ref[:, c0:c0 + cw]
        row = lax.broadcasted_iota(jnp.int32, (ts, cw), 0)
        p_row = prev_ref[SUBLANES - 1:SUBLANES, c0:c0 + cw] * has_prev
        n_row = next_ref[0:1, c0:c0 + cw] * has_next
        xp = jnp.where(row == 0, p_row, pltpu.roll(x, 1, 0))
        xn = jnp.where(row == ts - 1, n_row, pltpu.roll(x, ts - 1, 0))
        return xp, x, xn

    def conv(c0, cw, w_ref, b_ref, w0):
        xp, x, xn = neighbours(c0, cw)
        w = w_ref[:, w0:w0 + cw]
        return xp * w[0:1] + x * w[1:2] + xn * w[2:3] + b_ref[:, w0:w0 + cw], x

    conv_a, xa = conv(H_AX, D_A, mcw_ref, mcb_ref, 0)
    xc = _silu(conv_a)
    xc_ref[0] = xc
    gates = jnp.zeros((ts, LANES), F32) + gb_ref[...]
    for h in range(H_A):
        sl = slice(h * DH_A, (h + 1) * DH_A)
        qh = _bdot(xc[:, sl], wq_ref[h])
        kh = _bdot(xc[:, sl], wk_ref[h])
        vh = _bdot(xa[:, sl], wv_ref[h])
        gates += (_bdot(qh, gw_ref[h * DH_A:(h + 1) * DH_A])
                  + _bdot(kh, gw_ref[D_A + h * DH_A:D_A + (h + 1) * DH_A])
                  + _bdot(vh, gw_ref[2 * D_A + h * DH_A:2 * D_A + (h + 1) * DH_A]))
        q_ref[0, :, sl] = qh.astype(BF16)
        k_ref[0, :, sl] = (kh * (DH_A ** -0.5)).astype(BF16)
        v_ref[0, :, sl] = vh.astype(BF16)
    col = lax.broadcasted_iota(jnp.int32, (ts, LANES), 1)
    rmod = lax.broadcasted_iota(jnp.int32, (ts, LANES), 0) % CHUNK_A
    lf = _log_sigmoid(gates)
    cf = lf
    cb = lf
    sh = 1
    while sh < CHUNK_A:
        cf = cf + jnp.where(rmod >= sh, pltpu.roll(cf, sh, 0), 0.0)
        cb = cb + jnp.where(rmod < CHUNK_A - sh, pltpu.roll(cb, ts - sh, 0), 0.0)
        sh *= 2
    is_ff = (col >= H_A) & (col < 2 * H_A)
    is_fb = (col >= 3 * H_A) & (col < 4 * H_A)
    gcol_ref[0] = jnp.where(is_ff, cf, jnp.where(is_fb, cb, gates))

    cv, _ = conv(H_BV, D_B, hcw_ref, hcb_ref, 0)
    cx0, _ = conv(H_BX0, D_B, hcw_ref, hcb_ref, D_B)
    cx1, _ = conv(H_BX1, D_B, hcw_ref, hcb_ref, 2 * D_B)
    z_ref[0] = cv * cx1
    x0_ref[0] = cx0

    cw = 512
    for j in range(3 * D_C // cw):
        xp, x, xn = neighbours(H_CR + j * cw, cw)
        mu = mu_ref[:, j * cw:(j + 1) * cw]
        u_ref[0, :, j * cw:(j + 1) * cw] = x + mu * (0.5 * (xp + xn) - x)
    xp, x, xn = neighbours(H_LW, 4 * LORA)
    mu = mu_ref[:, 3 * D_C:3 * D_C + 4 * LORA]
    ul_ref[0] = x + mu * (0.5 * (xp + xn) - x)


def _prep(proj_h, bsz, seq, ts, ml_conv_w, ml_conv_b, wq, wk, wv, gate_w_pad, gate_b_pad,
          hy_conv_w, hy_conv_b, rw_mu):
    ns = seq // ts
    hb = ts // SUBLANES
    last_hb = bsz * seq // SUBLANES - 1
    full = lambda a: pl.BlockSpec(a.shape, lambda b, i: (0,) * a.ndim)
    seq_spec = lambda w: pl.BlockSpec((1, ts, w), lambda b, i: (b, i, 0))
    params = (ml_conv_w, ml_conv_b, wq, wk, wv, gate_w_pad, gate_b_pad, hy_conv_w, hy_conv_b, rw_mu)
    outs = [(D_A, BF16), (D_A, BF16), (D_A, BF16), (D_A, F32), (LANES, F32),
            (D_B, F32), (D_B, F32), (3 * D_C, F32), (4 * LORA, F32)]
    return pl.pallas_call(
        _prep_kernel,
        grid=(bsz, ns),
        in_specs=[
            pl.BlockSpec((ts, N_H), lambda b, i: (b * ns + i, 0)),
            pl.BlockSpec((SUBLANES, N_H), lambda b, i: (jnp.maximum((b * ns + i) * hb - 1, 0), 0)),
            pl.BlockSpec((SUBLANES, N_H), lambda b, i: (jnp.minimum((b * ns + i + 1) * hb, last_hb), 0)),
        ] + [full(p) for p in params],
        out_specs=[seq_spec(w) for w, _ in outs],
        out_shape=[jax.ShapeDtypeStruct((bsz, seq, w), dt) for w, dt in outs],
        compiler_params=_cparams(("arbitrary", "arbitrary")),
        name="prep",
    )(proj_h, proj_h, proj_h, *params)


def _mlstm_kernel(q_ref, k_ref, v_ref, xc_ref, za_ref, gcol_ref, ng_ref, sk_ref, o_ref,
                  h_ref, ct_ref, n_ref, m_ref, num_ref, den_ref, bb_ref, mb_ref, kv_ref, sc_ref,
                  sel_ref):
    head = pl.program_id(1)
    seq = q_ref.shape[1]
    L = CHUNK_A
    nc = seq // L
    row = lax.broadcasted_iota(jnp.int32, (L, L), 0)
    colm = lax.broadcasted_iota(jnp.int32, (L, L), 1)
    lane = lax.broadcasted_iota(jnp.int32, (L, LANES), 1)

    grp = num_ref.shape[1]
    srow = lax.broadcasted_iota(jnp.int32, (SUBLANES, LANES), 0)
    ones = jnp.ones((LANES, LANES), BF16)
    ones2 = jnp.ones((2 * LANES, LANES), BF16)

    def dot(a, b):
        return jnp.dot(a, b, preferred_element_type=F32)

    def split3(x):
        x1 = x.astype(BF16)
        e1 = x - x1.astype(F32)
        x2 = e1.astype(BF16)
        return x1, x2, (e1 - x2.astype(F32)).astype(BF16)

    srow_i = lax.broadcasted_iota(jnp.int32, (3 * LANES, 2 * LANES), 0) % LANES
    scol_i = lax.broadcasted_iota(jnp.int32, (3 * LANES, 2 * LANES), 1)
    for d in range(2):
        li_lane = head + 2 * H_A * d
        want = jnp.where(scol_i < LANES, li_lane + H_A, li_lane)
        sel_ref[d] = jnp.where(srow_i == want, 1.0, 0.0).astype(BF16)

    def prepare(j, d, slot):
        fwd = d == 0
        tri = (colm <= row) if fwd else (colm >= row)
        c0 = pl.multiple_of(j * L, L)
        q = q_ref[0, pl.ds(c0, L), :]
        k = k_ref[0, pl.ds(c0, L), :]
        v = v_ref[0, pl.ds(c0, L), :]
        qk = _bdot_nt(q, k)
        bl = dot(jnp.concatenate(split3(gcol_ref[0, pl.ds(c0, L), :]), axis=1), sel_ref[d])
        yield
        b, li = bl[:, :LANES], bl[:, LANES:]
        x0, x1, x2 = split3(b)
        y0, y1, y2 = split3(li - b)
        xl = jnp.where(lane == 0, x0, jnp.where(lane == 1, x1, jnp.where(
            lane == 2, x2, jnp.where(lane < 6, 1.0, 0.0).astype(BF16))))
        yl = jnp.where(lane == 3, y0, jnp.where(lane == 4, y1, jnp.where(
            lane == 5, y2, jnp.where(lane < 3, 1.0, 0.0).astype(BF16))))
        dm = _bdot_nt(xl, yl)
        gtot = b[L - 1:L, :] if fwd else b[0:1, :]
        a = gtot - b + li
        a_max = jnp.max(a, axis=0, keepdims=True)
        wk = jnp.exp(a - a_max)
        vw = (v.astype(F32) * wk).astype(BF16)
        kv = lax.dot_general(k, vw, (((0,), (0,)), ((), ())), preferred_element_type=F32)
        kn = jnp.sum(k.astype(F32) * wk, axis=0, keepdims=True)
        sc_ref[d, slot] = jnp.where(srow == 0, kn, jnp.where(srow == 1, gtot, a_max))
        bb_ref[d, slot] = b
        yield
        dm = jnp.where(tri, dm, -jnp.inf)
        m_loc = jnp.max(dm, axis=-1, keepdims=True)
        s = qk * jnp.exp(dm - m_loc)
        num = _bdot(s, v)
        den = dot(jnp.concatenate(_split_bf16(s), axis=1), ones)
        mb_ref[d, slot] = jnp.broadcast_to(m_loc, (L, LANES))
        kv_ref[d, slot] = kv
        yield
        num_ref[d, slot] = num
        den_ref[d, slot] = den

    def advance(d, todo):
        ct, n, m = ct_ref[d], n_ref[d], m_ref[d]
        pending = []
        for j, slot in todo:
            c0 = pl.multiple_of(j * L, L)
            q = q_ref[0, pl.ds(c0, L), :]
            qc = _bdot(q, ct)
            qn = dot(jnp.concatenate(_split_bf16(q.astype(F32) * n), axis=1), ones2)
            sc = sc_ref[d, slot]
            kn, gtot, a_max = sc[0:1, :], sc[1:2, :], sc[2:3, :]
            m_new = jnp.maximum(gtot + m, a_max)
            decay = jnp.exp(gtot + m - m_new)
            beta = jnp.exp(a_max - m_new)
            pending.append((c0, slot, qc, qn, m))
            ct = decay * ct + beta * kv_ref[d, slot]
            n = decay * n + beta * kn
            m = m_new
        ct_ref[d], n_ref[d], m_ref[d] = ct, n, m
        yield
        for c0, slot, qc, qn, m_in in pending:
            inter = bb_ref[d, slot] + m_in
            m_loc = mb_ref[d, slot]
            m_t = jnp.maximum(inter, m_loc)
            c_intra = jnp.exp(m_loc - m_t)
            c_inter = jnp.exp(inter - m_t)
            num = c_intra * num_ref[d, slot] + c_inter * qc
            den = c_intra * den_ref[d, slot] + c_inter * qn
            h_ref[pl.ds(c0, L), :] += num / jnp.maximum(jnp.abs(den), jnp.exp(-m_t))

    ct_ref[...] = jnp.zeros_like(ct_ref)
    n_ref[...] = jnp.zeros_like(n_ref)
    m_ref[...] = jnp.full_like(m_ref, -jnp.inf)
    h_ref[...] = jnp.zeros_like(h_ref)
    lock = 4 if grp % 4 == 0 else 1

    def group(gi, carry):
        def phase_a(i, c):
            gens = []
            for u in range(lock):
                jf = gi * grp + i * lock + u
                gens += [prepare(jf, 0, i * lock + u), prepare(nc - 1 - jf, 1, i * lock + u)]
            _round_robin(gens)
            return c

        def phase_b(i, c):
            jf = [gi * grp + i * lock + u for u in range(lock)]
            _round_robin([advance(0, [(j, i * lock + u) for u, j in enumerate(jf)]),
                          advance(1, [(nc - 1 - j, i * lock + u) for u, j in enumerate(jf)])])
            return c

        lax.fori_loop(0, grp // lock, phase_a, 0)
        lax.fori_loop(0, grp // lock, phase_b, 0)
        return carry

    lax.fori_loop(0, nc // grp, group, 0)

    tile = 256
    def finish(i, carry):
        r0 = pl.multiple_of(i * tile, tile)
        hh = h_ref[pl.ds(r0, tile), :]
        mu = jnp.mean(hh, axis=-1, keepdims=True)
        hc = hh - mu
        var = jnp.mean(hc * hc, axis=-1, keepdims=True)
        hn = hc * lax.rsqrt(var + HEAD_NORM_EPS) * ng_ref[...]
        out = (hn + sk_ref[...] * xc_ref[0, pl.ds(r0, tile), :]) * _silu(za_ref[pl.ds(r0, tile), :])
        o_ref[0, pl.ds(r0, tile), :] = out
        return carry

    lax.fori_loop(0, seq // tile, finish, 0)


def _mlstm(q, k, v, xc, proj_g, gcol, norm_g, skip):
    bsz, seq, _ = q.shape
    grp = min(ML_GROUP, seq // CHUNK_A)
    hs = lambda: pl.BlockSpec((1, seq, DH_A), lambda b, h: (b, 0, h))
    return pl.pallas_call(
        _mlstm_kernel,
        grid=(bsz, H_A),
        in_specs=[
            hs(), hs(), hs(), hs(),
            pl.BlockSpec((seq, DH_A), lambda b, h: (b, G_AZ // DH_A + h)),
            pl.BlockSpec((1, seq, LANES), lambda b, h: (b, 0, 0)),
            pl.BlockSpec((1, DH_A), lambda b, h: (0, h)),
            pl.BlockSpec((1, DH_A), lambda b, h: (0, h)),
        ],
        out_specs=pl.BlockSpec((1, seq, DH_A), lambda b, h: (b, 0, h)),
        out_shape=jax.ShapeDtypeStruct((bsz, seq, D_A), F32),
        scratch_shapes=[pltpu.VMEM((seq, DH_A), F32), pltpu.VMEM((2, DH_A, DH_A), F32),
                        pltpu.VMEM((2, 1, DH_A), F32), pltpu.VMEM((2, 1, LANES), F32),
                        pltpu.VMEM((2, grp, CHUNK_A, DH_A), F32),
                        pltpu.VMEM((2, grp, CHUNK_A, LANES), F32),
                        pltpu.VMEM((2, grp, CHUNK_A, LANES), F32),
                        pltpu.VMEM((2, grp, CHUNK_A, LANES), F32),
                        pltpu.VMEM((2, grp, DH_A, DH_A), F32),
                        pltpu.VMEM((2, grp, SUBLANES, LANES), F32),
                        pltpu.VMEM((2, 3 * LANES, 2 * LANES), BF16)],
        compiler_params=_cparams(("arbitrary", "arbitrary")),
        name="mlstm",
    )(q, k, v, xc, proj_g, gcol, norm_g.reshape(1, D_A), skip.reshape(1, D_A))


FFT_N2 = 128
HY_CT = 128
HY_PASSES = 1
FFT_ROWS_UNROLL = 8
FFT_SLAB_UNROLL = 2


def _split_bf16(x):
    hi = x.astype(BF16)
    lo = (x - hi.astype(F32)).astype(BF16)
    return hi, lo


def _dot3(a_hi, a_lo, x):
    x_hi, x_lo = _split_bf16(x)
    d = lambda a, b: jnp.dot(a, b, preferred_element_type=F32)
    if HY_PASSES == 1:
        return d(a_hi, x_hi)
    return d(a_hi, x_hi) + (d(a_lo, x_hi) + d(a_hi, x_lo))


def _dft_constants(seq):
    n = 2 * seq
    n2 = FFT_N2
    n1 = n // n2
    f1 = np.arange(n1)[:, None]
    s1 = np.arange(n1)[None, :]
    th1 = 2.0 * np.pi * f1 * s1 / n1
    fwd1 = np.concatenate([np.cos(th1), -np.sin(th1)], axis=0)
    inv1 = np.concatenate([np.cos(th1), -np.sin(th1)], axis=1)[:n1 // 2] / n
    a = np.arange(n2)
    th2 = 2.0 * np.pi * a[:, None] * a[None, :] / n2
    c2, s2 = np.cos(th2), np.sin(th2)
    fwd2 = np.block([[c2, s2], [-s2, c2]])
    inv2 = np.block([[c2, -s2], [s2, c2]])
    tht = 2.0 * np.pi * a[:, None] * np.arange(n1)[None, :] / n
    tw = np.zeros((2, n2, LANES), np.float64)
    tw[0, :, :n1] = np.cos(tht)
    tw[1, :, :n1] = -np.sin(tht)

    def hl(m):
        m32 = jnp.asarray(m, F32)
        hi = m32.astype(BF16)
        lo = (m32 - hi.astype(F32)).astype(BF16)
        return jnp.stack([hi, lo])

    return dict(fwd1=hl(fwd1), inv1=hl(inv1), fwd2=hl(fwd2), inv2=hl(inv2),
                tw=jnp.asarray(tw, F32), n1=n1)


def _fft_stage_a(load_rows, k_rows, fwd1_ref, a_ref, n1):
    n2 = FFT_N2
    f_hi = fwd1_ref[0, :, :k_rows]
    f_lo = fwd1_ref[1, :, :k_rows]

    def one(s2):
        m = _dot3(f_hi, f_lo, load_rows(s2))
        yield
        a_ref[pl.ds(s2, n1, stride=2 * n2), :] = m[:n1]
        a_ref[pl.ds(n2 + s2, n1, stride=2 * n2), :] = m[n1:]

    def body(i, carry):
        _round_robin([one(i * FFT_ROWS_UNROLL + u) for u in range(FFT_ROWS_UNROLL)])
        return carry

    lax.fori_loop(0, n2 // FFT_ROWS_UNROLL, body, 0)


def _twiddle_cols(tw_ref, f1):
    lane = lax.broadcasted_iota(jnp.int32, (FFT_N2, LANES), 1)
    sel = lane == f1
    twr = jnp.sum(jnp.where(sel, tw_ref[0], 0.0), axis=-1, keepdims=True)
    twi = jnp.sum(jnp.where(sel, tw_ref[1], 0.0), axis=-1, keepdims=True)
    return twr, twi


def _hyena_filter_kernel(feat_ref, w1_ref, b1_ref, w2_ref, b2_ref, w3_ref, b3_ref, fr_ref,
                         wo_ref, dec_ref, fwd1_ref, fwd2_ref, tw_ref, o_ref, kt_ref, a_ref):
    n = feat_ref.shape[0]
    seq = n // 2
    n2 = FFT_N2
    n1 = n // n2
    tile = 512
    freq = fr_ref[...]

    def gen(i, carry):
        r0 = pl.multiple_of(i * tile, tile)
        feats = feat_ref[pl.ds(r0, tile), :]
        hid = jnp.sin(freq * (_hdot(feats, w1_ref[...]) + b1_ref[...]))
        hid = jnp.sin(freq * (_hdot(hid, w2_ref[...]) + b2_ref[...]))
        hid = jnp.sin(freq * (_hdot(hid, w3_ref[...]) + b3_ref[...]))
        second = r0 >= seq
        wo = jnp.where(second, wo_ref[1, 0], wo_ref[0, 0])
        dec = jnp.where(second, dec_ref[1, 0], dec_ref[0, 0])
        t = feats[:, 0:1]
        filt = _hdot(hid, wo) * jnp.exp(-t * jnp.abs(dec))
        rows = r0 + lax.broadcasted_iota(jnp.int32, (tile, 1), 0)
        kt_ref[pl.ds(r0, tile), :] = jnp.where(rows == seq, 0.0, filt)
        return carry

    lax.fori_loop(0, n // tile, gen, 0)

    _fft_stage_a(lambda s2: kt_ref[pl.ds(s2, n1, stride=n2), :], n1, fwd1_ref, a_ref, n1)

    def slab(f1):
        r0 = pl.multiple_of(f1 * 2 * n2, 2 * n2)
        ar = a_ref[pl.ds(r0, n2), :]
        ai = a_ref[pl.ds(r0 + n2, n2), :]
        twr, twi = _twiddle_cols(tw_ref, f1)
        x = jnp.concatenate([ar * twr - ai * twi, ar * twi + ai * twr], axis=0)
        xf = _dot3(fwd2_ref[0], fwd2_ref[1], x)
        yield
        o_ref[0, pl.ds(r0, 2 * n2), :] = xf

    def slabs(i, carry):
        _round_robin([slab(i * FFT_SLAB_UNROLL + u) for u in range(FFT_SLAB_UNROLL)])
        return carry

    lax.fori_loop(0, n1 // FFT_SLAB_UNROLL, slabs, 0)


def _hyena_filter_spectrum(consts, feats_full, w1p, b1, w2, b2, w3, b3, freq, w_out, decay):
    n = feats_full.shape[0]
    nct = D_B // HY_CT
    full = lambda a: pl.BlockSpec(a.shape, lambda j: (0,) * a.ndim)
    wo = w_out.reshape(HY_HID, 2, nct, HY_CT).transpose(1, 2, 0, 3)
    dec = decay.reshape(2, nct, 1, HY_CT)
    small = (w1p, b1.reshape(1, -1), w2, b2.reshape(1, -1), w3, b3.reshape(1, -1), freq.reshape(1, -1))
    return pl.pallas_call(
        _hyena_filter_kernel,
        grid=(nct,),
        in_specs=[full(feats_full)] + [full(a) for a in small] + [
            pl.BlockSpec((2, 1, HY_HID, HY_CT), lambda j: (0, j, 0, 0)),
            pl.BlockSpec((2, 1, 1, HY_CT), lambda j: (0, j, 0, 0)),
            full(consts['fwd1']), full(consts['fwd2']), full(consts['tw']),
        ],
        out_specs=pl.BlockSpec((1, 2 * n, HY_CT), lambda j: (j, 0, 0)),
        out_shape=jax.ShapeDtypeStruct((nct, 2 * n, HY_CT), F32),
        scratch_shapes=[pltpu.VMEM((n, HY_CT), F32), pltpu.VMEM((2 * n, HY_CT), F32)],
        compiler_params=_cparams(("arbitrary",)),
        name="hyena_filter",
    )(feats_full, *small, wo, dec, consts['fwd1'], consts['fwd2'], consts['tw'])


def _hyena_conv_kernel(z_ref, x0_ref, zb_ref, ks_ref, bias_ref, fwd1_ref, inv1_ref, fwd2_ref,
                       inv2_ref, tw_ref, o_ref, a_ref, y_ref):
    seq = z_ref.shape[1]
    n2 = FFT_N2
    n1 = 2 * seq // n2
    nh = n1 // 2

    _fft_stage_a(lambda s2: z_ref[0, pl.ds(s2, nh, stride=n2), :], nh, fwd1_ref, a_ref, n1)

    def slab(f1):
        r0 = pl.multiple_of(f1 * 2 * n2, 2 * n2)
        ar = a_ref[pl.ds(r0, n2), :]
        ai = a_ref[pl.ds(r0 + n2, n2), :]
        twr, twi = _twiddle_cols(tw_ref, f1)
        x = jnp.concatenate([ar * twr - ai * twi, ar * twi + ai * twr], axis=0)
        xf = _dot3(fwd2_ref[0], fwd2_ref[1], x)
        yield
        xr, xi = xf[:n2], xf[n2:]
        kr = ks_ref[0, pl.ds(r0, n2), :]
        ki = ks_ref[0, pl.ds(r0 + n2, n2), :]
        y = jnp.concatenate([xr * kr - xi * ki, xr * ki + xi * kr], axis=0)
        bb = _dot3(inv2_ref[0], inv2_ref[1], y)
        yield
        br, bi = bb[:n2], bb[n2:]
        a_ref[pl.ds(r0, n2), :] = br * twr + bi * twi
        a_ref[pl.ds(r0 + n2, n2), :] = bi * twr - br * twi

    def slabs(i, carry):
        _round_robin([slab(i * FFT_SLAB_UNROLL + u) for u in range(FFT_SLAB_UNROLL)])
        return carry

    lax.fori_loop(0, n1 // FFT_SLAB_UNROLL, slabs, 0)

    i_hi = inv1_ref[0]
    i_lo = inv1_ref[1]

    def stage_d(s2):
        br = a_ref[pl.ds(s2, n1, stride=2 * n2), :]
        bi = a_ref[pl.ds(n2 + s2, n1, stride=2 * n2), :]
        y = _dot3(i_hi, i_lo, jnp.concatenate([br, bi], axis=0))
        yield
        y_ref[pl.ds(s2, nh, stride=n2), :] = y

    def stage_ds(i, carry):
        _round_robin([stage_d(i * FFT_ROWS_UNROLL + u) for u in range(FFT_ROWS_UNROLL)])
        return carry

    lax.fori_loop(0, n2 // FFT_ROWS_UNROLL, stage_ds, 0)

    tile = 256
    def finish(i, carry):
        r0 = pl.multiple_of(i * tile, tile)
        z = z_ref[0, pl.ds(r0, tile), :]
        y = y_ref[pl.ds(r0, tile), :] + bias_ref[...] * z
        o_ref[0, pl.ds(r0, tile), :] = x0_ref[0, pl.ds(r0, tile), :] * y * _silu(zb_ref[pl.ds(r0, tile), :])
        return carry

    lax.fori_loop(0, seq // tile, finish, 0)


def _hyena_conv(consts, z, x0, proj_g, kspec, bias):
    bsz, seq, _ = z.shape
    nct = D_B // HY_CT
    full = lambda a: pl.BlockSpec(a.shape, lambda j, b: (0,) * a.ndim)
    cs = pl.BlockSpec((1, seq, HY_CT), lambda j, b: (b, 0, j))
    mats = (consts['fwd1'], consts['inv1'], consts['fwd2'], consts['inv2'], consts['tw'])
    return pl.pallas_call(
        _hyena_conv_kernel,
        grid=(nct, bsz),
        in_specs=[
            cs, cs,
            pl.BlockSpec((seq, HY_CT), lambda j, b: (b, G_BZ // HY_CT + j)),
            pl.BlockSpec((1, 4 * seq, HY_CT), lambda j, b: (j, 0, 0)),
            pl.BlockSpec((1, HY_CT), lambda j, b: (0, j)),
        ] + [full(m) for m in mats],
        out_specs=cs,
        out_shape=jax.ShapeDtypeStruct((bsz, seq, D_B), F32),
        scratch_shapes=[pltpu.VMEM((4 * seq, HY_CT), F32), pltpu.VMEM((seq, HY_CT), F32)],
        compiler_params=_cparams(("arbitrary", "arbitrary")),
        name="hyena_conv",
    )(z, x0, proj_g, kspec, bias.reshape(1, D_B), *mats)


def _rwkv_kernel(r_ref, k_ref, v_ref, ul_ref, cz_ref, w0_ref, w2_ref, a0_ref, a2_ref,
                 kks_ref, ka_ref, rk_ref, lg_ref, lb_ref, o_ref, y_ref, kb_ref, s_ref,
                 wr_ref, u_ref, arb_ref, y0_ref, bt_ref, kv_ref, egc_ref):
    seq = r_ref.shape[1]
    L = CHUNK_C
    nc = seq // L
    P2 = 2 * L
    lane = lax.broadcasted_iota(jnp.int32, (L, LANES), 1)
    head0 = lane < HEAD_C
    prow = lax.broadcasted_iota(jnp.int32, (P2, P2), 0)
    pcol = lax.broadcasted_iota(jnp.int32, (P2, P2), 1)
    same = (prow // L) == (pcol // L)
    rt, ct = prow % L, pcol % L
    eye = jnp.where(prow == pcol, 1.0, 0.0).astype(F32)
    blk = jnp.where((prow // HEAD_C) == (pcol // HEAD_C), 1.0, 0.0).astype(F32)
    crow = lax.broadcasted_iota(jnp.int32, (L, L), 0)
    ccol = lax.broadcasted_iota(jnp.int32, (L, L), 1)

    def stack(x):
        return jnp.concatenate([jnp.where(head0, x, 0.0), jnp.where(head0, 0.0, x)], axis=0)

    blk_b = blk.astype(BF16)
    grp = wr_ref.shape[1]

    def dot(a, b):
        return jnp.dot(a, b, preferred_element_type=F32)

    def each(f, *cols):
        return [f(*xs) for xs in zip(*cols)]

    def prepare(probs):
        dirs = [d for _, d, _ in probs]
        fwd = [d == 0 for d in dirs]
        strict = [same & ((ct < rt) if f else (ct > rt)) for f in fwd]
        incl = [same & ((ct <= rt) if f else (ct >= rt)) for f in fwd]
        cum = [jnp.where((ccol <= crow) if f else (ccol >= crow), 1.0, 0.0).astype(BF16) for f in fwd]
        c0 = [pl.multiple_of(j * L, L) for j, _, _ in probs]
        r = [r_ref[0, pl.ds(c, L), :] for c in c0]
        k = [k_ref[0, pl.ds(c, L), :] for c in c0]
        v = [v_ref[0, pl.ds(c, L), :] for c in c0]
        lwla = [ul_ref[0, pl.ds(c, L), :] for c in c0]
        xw = [w0_ref[d:d + 1, :] + _bdot(jnp.tanh(x[:, :LANES]), w2_ref[d]) for x, d in zip(lwla, dirs)]
        lwt = [-jnp.exp(_log_sigmoid(x) - 0.5) for x in xw]
        a = [_sigmoid(a0_ref[d:d + 1, :] + _bdot(x[:, LANES:], a2_ref[d])) for x, d in zip(lwla, dirs)]
        kkv = [x * kks_ref[...] for x in k]
        ss = [dot((x * x).astype(BF16), blk_b) for x in kkv]
        kk = each(lambda x, s: x / jnp.maximum(jnp.sqrt(s), 1e-12), kkv, ss)
        kd = each(lambda x, y: x * (1.0 + (y - 1.0) * ka_ref[...]), k, a)
        for c, x in zip(c0, kd):
            kb_ref[pl.ds(c, L), :] += x
        be = each(lambda x, y: x * y, kk, a)

        def cumsum(cm, x):
            l1 = x.astype(BF16)
            e1 = x - l1.astype(F32)
            l2 = e1.astype(BF16)
            l3 = (e1 - l2.astype(F32)).astype(BF16)
            return dot(jnp.concatenate([cm, cm, cm], axis=1), jnp.concatenate([l1, l2, l3], axis=0))

        g = each(cumsum, cum, lwt)
        eng = [jnp.exp(-x) for x in g]
        egc = [jnp.exp(x[L - 1:L, :] if f else x[0:1, :]) for x, f in zip(g, fwd)]
        at = each(lambda x, y, z: stack(-x * jnp.exp(y - z)).astype(BF16), kk, g, lwt)
        rt_ = each(lambda x, y: (x * jnp.exp(y)).astype(BF16), r, g)
        bh = each(lambda x, y: x * y, be, eng)
        kh = each(lambda x, y: x * y, kd, eng)
        lhs = each(lambda x, y: jnp.concatenate([x, stack(y)], axis=0), at, rt_)
        rhs = each(lambda x, y: jnp.concatenate([stack(x), stack(y)], axis=0).astype(BF16), bh, kh)
        g1 = each(lambda x, y: lax.dot_general(x, y, (((1,), (1,)), ((), ())),
                                               preferred_element_type=F32), lhs, rhs)
        aab = each(lambda m, x: jnp.where(m, x[:P2, :P2], 0.0), strict, g1)
        aak_ark = each(lambda ms, mi, x: jnp.concatenate(
            [jnp.where(ms, x[:P2, P2:], 0.0), jnp.where(mi, x[P2:, P2:], 0.0)], axis=0).astype(BF16),
            strict, incl, g1)
        for (_, d, slot), mi, x in zip(probs, incl, g1):
            arb_ref[d, slot] = jnp.where(mi, x[P2:, :P2], 0.0).astype(BF16)
        t = [eye + x for x in aab]
        p = [x.astype(BF16) for x in aab]
        p = [dot(x, x).astype(BF16) for x in p]
        sq = 2
        while sq < L:
            if 2 * sq >= L:
                t = each(lambda x, y: y + dot(x, y.astype(BF16)), p, t)
            else:
                pt = each(lambda x, y: dot(x, jnp.concatenate([x, y.astype(BF16)], axis=1)), p, t)
                p = [x[:, :P2].astype(BF16) for x in pt]
                t = each(lambda y, x: y + x[:, P2:], t, pt)
            sq *= 2
        vs = [stack(x).astype(BF16) for x in v]
        av = each(dot, aak_ark, vs)
        wu = each(lambda x, y, z: dot(x.astype(BF16), jnp.concatenate([y, z[:P2].astype(BF16)], axis=1)),
                  t, at, av)
        kv = each(lambda x, y, z: lax.dot_general(
            x, stack(y * z).astype(BF16), (((0,), (0,)), ((), ())), preferred_element_type=F32),
            vs, kh, egc)
        for i, (_, d, slot) in enumerate(probs):
            wr_ref[d, slot] = jnp.concatenate([wu[i][:, :P2].astype(BF16), rt_[i]], axis=0)
            u_ref[d, slot] = wu[i][:, P2:].astype(BF16)
            y0_ref[d, slot] = av[i][P2:P2 + L] + av[i][P2 + L:]
            bt_ref[d, slot] = stack(bh[i] * egc[i]).astype(BF16)
            kv_ref[d, slot] = kv[i].astype(BF16)
            egc_ref[d, slot] = egc[i]

    def advance(probs):
        s = [s_ref[d] for _, d, _ in probs]
        x = [lax.dot_general(wr_ref[d, slot], y.astype(BF16), (((1,), (1,)), ((), ())),
                             preferred_element_type=F32) for (_, d, slot), y in zip(probs, s)]
        sa = [(y[:P2] + u_ref[d, slot].astype(F32)).astype(BF16) for (_, d, slot), y in zip(probs, x)]
        upd = [lax.dot_general(y, bt_ref[d, slot], (((0,), (0,)), ((), ())),
                               preferred_element_type=F32) for (_, d, slot), y in zip(probs, sa)]
        z = [dot(arb_ref[d, slot], y) for (_, d, slot), y in zip(probs, sa)]
        for i, (j, d, slot) in enumerate(probs):
            s_ref[d] = s[i] * egc_ref[d, slot] + kv_ref[d, slot].astype(F32) + upd[i]
            c0 = pl.multiple_of(j * L, L)
            y_ref[pl.ds(c0, L), :] += x[i][P2:] + z[i][:L] + z[i][L:] + y0_ref[d, slot]

    s_ref[...] = jnp.zeros_like(s_ref)
    y_ref[...] = jnp.zeros_like(y_ref)
    kb_ref[...] = jnp.zeros_like(kb_ref)
    lock = 4 if grp % 4 == 0 else 1

    def group(gi, carry):
        def phase_a(i, c):
            probs = []
            for q in range(lock):
                jf = gi * grp + i * lock + q
                probs += [(jf, 0, i * lock + q), (nc - 1 - jf, 1, i * lock + q)]
            prepare(probs)
            return c

        def phase_b(i, c):
            jf = gi * grp + i
            advance([(jf, 0, i), (nc - 1 - jf, 1, i)])
            return c

        lax.fori_loop(0, grp // lock, phase_a, 0)
        lax.fori_loop(0, grp, phase_b, 0)
        return carry

    lax.fori_loop(0, nc // grp, group, 0)

    tile = 256
    inv_n = 1.0 / HEAD_C
    blk2 = jnp.concatenate([blk_b, blk_b], axis=0)

    def head_sum(x):
        hi, lo = _split_bf16(x)
        return dot(jnp.concatenate([hi, lo], axis=1), blk2)

    def finish(i, carry):
        r0 = pl.multiple_of(i * tile, tile)
        y = y_ref[pl.ds(r0, tile), :]
        mu = head_sum(y) * inv_n
        yc = y - mu
        var = head_sum(yc * yc) * inv_n
        yn = yc * lax.rsqrt(var + LNX_EPS) * lg_ref[...] + lb_ref[...]
        r = r_ref[0, pl.ds(r0, tile), :]
        v = v_ref[0, pl.ds(r0, tile), :]
        kbon = 0.5 * kb_ref[pl.ds(r0, tile), :]
        bonus = head_sum(r * kbon * rk_ref[...]) * v
        o_ref[0, pl.ds(r0, tile), :] = (yn + bonus) * _silu(cz_ref[pl.ds(r0, tile), :])
        return carry

    lax.fori_loop(0, seq // tile, finish, 0)


def _rwkv(u, ul, proj_g, w0, w2pad, a0, a2pad, kk_s, ka, rk, lnx_g, lnx_b):
    bsz, seq, _ = u.shape
    npair = D_C // LANES
    grp = min(RW_GROUP, seq // CHUNK_C)
    col = lambda off: pl.BlockSpec((1, seq, LANES), lambda b, p: (b, 0, off + p))
    vec = pl.BlockSpec((1, LANES), lambda b, p: (0, p))
    two = pl.BlockSpec((2, LANES), lambda b, p: (0, p))
    lora = pl.BlockSpec((2, LANES, LANES), lambda b, p: (0, 0, p))
    row = lambda a: a.reshape(1, D_C)
    return pl.pallas_call(
        _rwkv_kernel,
        grid=(bsz, npair),
        in_specs=[
            col(0), col(npair), col(2 * npair),
            pl.BlockSpec((1, seq, 2 * LANES), lambda b, p: (b, 0, 0)),
            pl.BlockSpec((seq, LANES), lambda b, p: (b, G_CZ // LANES + p)),
            two, lora, two, lora, vec, vec, vec, vec, vec,
        ],
        out_specs=pl.BlockSpec((1, seq, LANES), lambda b, p: (b, 0, p)),
        out_shape=jax.ShapeDtypeStruct((bsz, seq, D_C), F32),
        scratch_shapes=[pltpu.VMEM((seq, LANES), F32), pltpu.VMEM((seq, LANES), F32),
                        pltpu.VMEM((2, LANES, LANES), F32),
                        pltpu.VMEM((2, grp, 3 * CHUNK_C, LANES), BF16),
                        pltpu.VMEM((2, grp, LANES, LANES), BF16),
                        pltpu.VMEM((2, grp, LANES, LANES), BF16),
                        pltpu.VMEM((2, grp, CHUNK_C, LANES), F32),
                        pltpu.VMEM((2, grp, LANES, LANES), BF16),
                        pltpu.VMEM((2, grp, LANES, LANES), BF16),
                        pltpu.VMEM((2, grp, 1, LANES), F32)],
        compiler_params=_cparams(("arbitrary", "arbitrary")),
        name="rwkv7",
    )(u, u, u, ul, proj_g, w0, w2pad, a0, a2pad, row(kk_s), row(ka), row(rk), row(lnx_g), row(lnx_b))


def _out_kernel(ya_ref, yb_ref, yc_ref, ga_ref, gb_ref, gc_ref, x_ref, gate_ref, pg_ref,
                wa_ref, wb_ref, wc_ref, wo_ref, o_ref):
    merged = (_sigmoid(ga_ref[...]) * _bdot(ya_ref[...], wa_ref[...])
              + _sigmoid(gb_ref[...]) * _bdot(yb_ref[...], wb_ref[...])
              + _sigmoid(gc_ref[...]) * _bdot(yc_ref[...], wc_ref[...]))
    out = _bdot(merged, wo_ref[...])
    y = out * lax.rsqrt(jnp.mean(out * out, axis=-1, keepdims=True) + NORM_EPS) * pg_ref[...]
    o_ref[...] = x_ref[...] + gate_ref[0] * y


def _merge_out(ya, yb, yc, proj_g, x2d, seq, gate, post_g, wa, wb, wc, wo, tm):
    rows, d = x2d.shape
    per_b = seq // tm
    full = lambda a: pl.BlockSpec(a.shape, lambda i: (0,) * a.ndim)
    gcol = lambda off: pl.BlockSpec((tm, d), lambda i: (i, off // d))
    return pl.pallas_call(
        _out_kernel,
        grid=(rows // tm,),
        in_specs=[
            pl.BlockSpec((tm, D_A), lambda i: (i, 0)),
            pl.BlockSpec((tm, D_B), lambda i: (i, 0)),
            pl.BlockSpec((tm, D_C), lambda i: (i, 0)),
            gcol(G_GA), gcol(G_GB), gcol(G_GC),
            pl.BlockSpec((tm, d), lambda i: (i, 0)),
            pl.BlockSpec((1, 1, d), lambda i: (i // per_b, 0, 0)),
            pl.BlockSpec((1, d), lambda i: (0, 0)),
            full(wa), full(wb), full(wc), full(wo),
        ],
        out_specs=pl.BlockSpec((tm, d), lambda i: (i, 0)),
        out_shape=jax.ShapeDtypeStruct((rows, d), F32),
        compiler_params=_cparams(("arbitrary",)),
        name="merge_out",
    )(ya.reshape(rows, D_A), yb.reshape(rows, D_B), yc.reshape(rows, D_C), proj_g, proj_g, proj_g,
      x2d, gate, post_g.reshape(1, d), wa, wb, wc, wo)


def _hyena_feats(seq):
    n = 2 * seq
    t = jnp.linspace(0.0, 1.0, seq, dtype=F32)
    w = 2.0 * math.pi * jnp.arange(seq, dtype=F32) / seq
    f = jnp.linspace(1e-4, HY_BANDS - 1, HY_BANDS, dtype=F32)
    zz = w[:, None] * f[None, :]
    feats = jnp.concatenate([t[:, None], jnp.cos(zz), -jnp.sin(zz)], axis=-1)
    pos = np.concatenate([np.arange(seq), [0], np.arange(seq - 1, 0, -1)])
    full = jnp.zeros((n, LANES), F32).at[:, :feats.shape[1]].set(feats[pos])
    return full


def _tiles(seq):
    return min(seq, 1024), min(seq, 256), min(seq, 512)


def kernel(x, c, ada_w, ada_b, pre_g, post_g, w_in, ml_conv_w, ml_conv_b, ml_wq, ml_wk, ml_wv, ml_gate_w, ml_gate_b, ml_norm_g, ml_skip, hy_conv_w, hy_conv_b, hy_w1, hy_b1, hy_w2, hy_b2, hy_w3, hy_b3, hy_freq, hy_w_out, hy_decay, hy_bias, rw_mu, rw_w0, rw_w2, rw_a0, rw_a2, rw_kk, rw_ka, rw_rk, rw_lnx_g, rw_lnx_b, w_branch_a, w_branch_b, w_branch_c, w_out):
    bsz, seq, d = x.shape
    depth = ada_w.shape[0]
    tm_in, ts_prep, tm_out = _tiles(seq)
    perm_h, perm_g = _ref_column_perm()
    consts = _dft_constants(seq)
    feats_full = _hyena_feats(seq)
    mod = _modulation(c, ada_w, ada_b)
    x2d = x.reshape(bsz * seq, d)
    n_gate = 4 * H_A
    for l in range(depth):
        shift = mod[l, :, None, 0:d]
        scale = mod[l, :, None, d:2 * d]
        gate = mod[l, :, None, 2 * d:3 * d]
        w_h = w_in[l][:, perm_h].astype(BF16)
        w_g = w_in[l][:, perm_g].astype(BF16)
        proj_h = _inproj(x2d, seq, pre_g[l], shift, scale, w_h, tm_in, N_H // 3)
        proj_g = _inproj(x2d, seq, pre_g[l], shift, scale, w_g, tm_in, N_G // 4)

        gw = jnp.zeros((3 * D_A, LANES), F32).at[:, :n_gate].set(ml_gate_w[l])
        gb = jnp.zeros((1, LANES), F32).at[0, :n_gate].set(ml_gate_b[l])
        q, k, v, xc, gcol, z, x0, u, ul = _prep(
            proj_h, bsz, seq, ts_prep, ml_conv_w[l], ml_conv_b[l].reshape(1, D_A),
            ml_wq[l], ml_wk[l], ml_wv[l], gw, gb,
            hy_conv_w[l], hy_conv_b[l].reshape(1, 3 * D_B), rw_mu[l].reshape(1, -1))

        y_a = _mlstm(q, k, v, xc, proj_g, gcol, ml_norm_g[l], ml_skip[l])

        w1p = jnp.zeros((LANES, HY_HID), F32).at[:hy_w1.shape[1]].set(hy_w1[l])
        kspec = _hyena_filter_spectrum(consts, feats_full, w1p, hy_b1[l], hy_w2[l], hy_b2[l],
                                       hy_w3[l], hy_b3[l], hy_freq[l], hy_w_out[l], hy_decay[l])
        y_b = _hyena_conv(consts, z, x0, proj_g, kspec, hy_bias[l])

        w2pad = (jnp.zeros((2, LANES, D_C), F32).at[0, :LORA].set(rw_w2[l, 0])
                 .at[1, LORA:].set(rw_w2[l, 1]))
        a2pad = (jnp.zeros((2, LANES, D_C), F32).at[0, :LORA].set(rw_a2[l, 0])
                 .at[1, LORA:].set(rw_a2[l, 1]))
        y_c = _rwkv(u, ul, proj_g, rw_w0[l], w2pad, rw_a0[l], a2pad, rw_kk[l], rw_ka[l],
                    rw_rk[l].reshape(-1), rw_lnx_g[l], rw_lnx_b[l])

        x2d = _merge_out(y_a, y_b, y_c, proj_g, x2d, seq, gate, post_g[l],
                         w_branch_a[l].astype(BF16), w_branch_b[l].astype(BF16),
                         w_branch_c[l].astype(BF16), w_out[l].astype(BF16), tm_out)
    return x2d.reshape(bsz, seq, d)
```

```python
import functools
import math

import numpy as np
import jax
import jax.numpy as jnp
from jax import lax
from jax.experimental import pallas as pl
from jax.experimental.pallas import tpu as pltpu

D_MODEL = 1024
DEPTH = 4
D_A = 512
H_A = 4
DH_A = 128
CHUNK_A = 64
ML_GROUP = 16
D_B = 512
HY_BANDS = 16
HY_HID = 64
D_C = 1024
HEAD_C = 64
H_C = D_C // HEAD_C
LORA = 64
CHUNK_C = 64
RW_GROUP = 8
LNX_EPS = 64e-5
NORM_EPS = 1e-6
HEAD_NORM_EPS = 1e-5

LANES = 128
SUBLANES = 8
VMEM_LIMIT = 56 * 1024 * 1024

F32 = jnp.float32
BF16 = jnp.bfloat16
HIGHEST = lax.Precision.HIGHEST

N_H = D_A + 3 * D_B + 3 * D_C + 4 * LORA
N_G = D_A + D_B + D_C + 3 * D_MODEL
H_AX, H_BV, H_BX0, H_BX1, H_CR, H_CK, H_CV, H_LW, H_LA = (
    0, 512, 1024, 1536, 2048, 3072, 4096, 5120, 5248)
G_AZ, G_BZ, G_CZ, G_GA, G_GB, G_GC = 0, 512, 1024, 2048, 3072, 4096


def _ref_column_perm():
    r = lambda a, n: np.arange(a, a + n)
    a_x, a_z = r(0, 512), r(512, 512)
    b_v, b_x0, b_x1, b_z = r(1024, 512), r(1536, 512), r(2048, 512), r(2560, 512)
    c_r, c_k, c_v = r(3072, 1024), r(4096, 1024), r(5120, 1024)
    c_lw, c_la = r(6144, 128), r(6272, 128)
    c_z = r(6400, 1024)
    g_a, g_b, g_c = r(7424, 1024), r(8448, 1024), r(9472, 1024)
    h = np.concatenate([a_x, b_v, b_x0, b_x1, c_r, c_k, c_v, c_lw, c_la])
    g = np.concatenate([a_z, b_z, c_z, g_a, g_b, g_c])
    return h, g


def _cparams(sem):
    return pltpu.CompilerParams(dimension_semantics=sem, vmem_limit_bytes=VMEM_LIMIT)


def _silu(x):
    return x * (1.0 / (1.0 + jnp.exp(-x)))


def _sigmoid(x):
    return 1.0 / (1.0 + jnp.exp(-x))


def _bdot(a, b):
    return jnp.dot(a.astype(BF16), b.astype(BF16), preferred_element_type=F32)


def _bdot_nt(a, b):
    return lax.dot_general(a.astype(BF16), b.astype(BF16), (((1,), (1,)), ((), ())),
                           preferred_element_type=F32)


def _round_robin(gens):
    gens = list(gens)
    while gens:
        alive = []
        for g in gens:
            try:
                next(g)
                alive.append(g)
            except StopIteration:
                pass
        gens = alive


def _hdot(a, b):
    return jnp.dot(a, b, preferred_element_type=F32, precision=HIGHEST)


def _mod_kernel(c_ref, w_ref, b_ref, o_ref):
    cond = _silu(c_ref[...])
    o_ref[0] = _hdot(cond, w_ref[0]) + b_ref[0]


def _modulation(c, ada_w, ada_b):
    depth, d, n3 = ada_w.shape
    bsz = c.shape[0]
    nt = n3 // d
    return pl.pallas_call(
        _mod_kernel,
        grid=(depth, nt),
        in_specs=[
            pl.BlockSpec((bsz, d), lambda l, j: (0, 0)),
            pl.BlockSpec((1, d, d), lambda l, j: (l, 0, j)),
            pl.BlockSpec((1, 1, d), lambda l, j: (l, 0, j)),
        ],
        out_specs=pl.BlockSpec((1, bsz, d), lambda l, j: (l, 0, j)),
        out_shape=jax.ShapeDtypeStruct((depth, bsz, n3), F32),
        compiler_params=_cparams(("arbitrary", "arbitrary")),
        name="adaln_mod",
    )(c, ada_w, ada_b.reshape(depth, 1, n3))


def _inproj_kernel(x_ref, g_ref, shift_ref, scale_ref, w_ref, o_ref, h_ref):
    @pl.when(pl.program_id(1) == 0)
    def _():
        x = x_ref[...]
        y = x * lax.rsqrt(jnp.mean(x * x, axis=-1, keepdims=True) + NORM_EPS)
        h = y * g_ref[...] * (1.0 + scale_ref[0]) + shift_ref[0]
        h_ref[...] = h.astype(BF16)

    o_ref[...] = jnp.dot(h_ref[...], w_ref[...], preferred_element_type=F32)


def _inproj(x2d, seq, pre_g, shift, scale, w_bf16, tm, tn):
    rows, d = x2d.shape
    n = w_bf16.shape[1]
    per_b = seq // tm
    return pl.pallas_call(
        _inproj_kernel,
        grid=(rows // tm, n // tn),
        in_specs=[
            pl.BlockSpec((tm, d), lambda i, j: (i, 0)),
            pl.BlockSpec((1, d), lambda i, j: (0, 0)),
            pl.BlockSpec((1, 1, d), lambda i, j: (i // per_b, 0, 0)),
            pl.BlockSpec((1, 1, d), lambda i, j: (i // per_b, 0, 0)),
            pl.BlockSpec((d, tn), lambda i, j: (0, j)),
        ],
        out_specs=pl.BlockSpec((tm, tn), lambda i, j: (i, j)),
        out_shape=jax.ShapeDtypeStruct((rows, n), F32),
        scratch_shapes=[pltpu.VMEM((tm, d), BF16)],
        compiler_params=_cparams(("arbitrary", "arbitrary")),
        name="inproj",
    )(x2d, pre_g.reshape(1, d), shift, scale, w_bf16)


def _log_sigmoid(x):
    return jnp.minimum(x, 0.0) - jnp.log(1.0 + jnp.exp(-jnp.abs(x)))


def _prep_kernel(main_ref, prev_ref, next_ref,
                 mcw_ref, mcb_ref, wq_ref, wk_ref, wv_ref, gw_ref, gb_ref,
                 hcw_ref, hcb_ref, mu_ref,
                 q_ref, k_ref, v_ref, xc_ref, gcol_ref, z_ref, x0_ref, u_ref, ul_ref):
    i = pl.program_id(1)
    ts = main_ref.shape[0]
    has_prev = jnp.where(i > 0, 1.0, 0.0).astype(F32)
    has_next = jnp.where(i < pl.num_programs(1) - 1, 1.0, 0.0).astype(F32)

    def neighbours(c0, cw):
        x = main_ref[:, c0:c0 + cw]
        row = lax.broadcasted_iota(jnp.int32, (ts, cw), 0)
        p_row = prev_ref[SUBLANES - 1:SUBLANES, c0:c0 + cw] * has_prev
        n_row = next_ref[0:1, c0:c0 + cw] * has_next
        xp = jnp.where(row == 0, p_row, pltpu.roll(x, 1, 0))
        xn = jnp.where(row == ts - 1, n_row, pltpu.roll(x, ts - 1, 0))
        return xp, x, xn

    def conv(c0, cw, w_ref, b_ref, w0):
        xp, x, xn = neighbours(c0, cw)
        w = w_ref[:, w0:w0 + cw]
        return xp * w[0:1] + x * w[1:2] + xn * w[2:3] + b_ref[:, w0:w0 + cw], x

    conv_a, xa = conv(H_AX, D_A, mcw_ref, mcb_ref, 0)
    xc = _silu(conv_a)
    xc_ref[0] = xc
    gates = jnp.zeros((ts, LANES), F32) + gb_ref[...]
    for h in range(H_A):
        sl = slice(h * DH_A, (h + 1) * DH_A)
        qh = _bdot(xc[:, sl], wq_ref[h])
        kh = _bdot(xc[:, sl], wk_ref[h])
        vh = _bdot(xa[:, sl], wv_ref[h])
        gates += (_bdot(qh, gw_ref[h * DH_A:(h + 1) * DH_A])
                  + _bdot(kh, gw_ref[D_A + h * DH_A:D_A + (h + 1) * DH_A])
                  + _bdot(vh, gw_ref[2 * D_A + h * DH_A:2 * D_A + (h + 1) * DH_A]))
        q_ref[0, :, sl] = qh.astype(BF16)
        k_ref[0, :, sl] = (kh * (DH_A ** -0.5)).astype(BF16)
        v_ref[0, :, sl] = vh.astype(BF16)
    col = lax.broadcasted_iota(jnp.int32, (ts, LANES), 1)
    rmod = lax.broadcasted_iota(jnp.int32, (ts, LANES), 0) % CHUNK_A
    lf = _log_sigmoid(gates)
    cf = lf
    cb = lf
    sh = 1
    while sh < CHUNK_A:
        cf = cf + jnp.where(rmod >= sh, pltpu.roll(cf, sh, 0), 0.0)
        cb = cb + jnp.where(rmod < CHUNK_A - sh, pltpu.roll(cb, ts - sh, 0), 0.0)
        sh *= 2
    is_ff = (col >= H_A) & (col < 2 * H_A)
    is_fb = (col >= 3 * H_A) & (col < 4 * H_A)
    gcol_ref[0] = jnp.where(is_ff, cf, jnp.where(is_fb, cb, gates))

    cv, _ = conv(H_BV, D_B, hcw_ref, hcb_ref, 0)
    cx0, _ = conv(H_BX0, D_B, hcw_ref, hcb_ref, D_B)
    cx1, _ = conv(H_BX1, D_B, hcw_ref, hcb_ref, 2 * D_B)
    z_ref[0] = cv * cx1
    x0_ref[0] = cx0

    cw = 512
    for j in range(3 * D_C // cw):
        xp, x, xn = neighbours(H_CR + j * cw, cw)
        mu = mu_ref[:, j * cw:(j + 1) * cw]
        u_ref[0, :, j * cw:(j + 1) * cw] = x + mu * (0.5 * (xp + xn) - x)
    xp, x, xn = neighbours(H_LW, 4 * LORA)
    mu = mu_ref[:, 3 * D_C:3 * D_C + 4 * LORA]
    ul_ref[0] = x + mu * (0.5 * (xp + xn) - x)


def _prep(proj_h, bsz, seq, ts, ml_conv_w, ml_conv_b, wq, wk, wv, gate_w_pad, gate_b_pad,
          hy_conv_w, hy_conv_b, rw_mu):
    ns = seq // ts
    hb = ts // SUBLANES
    last_hb = bsz * seq // SUBLANES - 1
    full = lambda a: pl.BlockSpec(a.shape, lambda b, i: (0,) * a.ndim)
    seq_spec = lambda w: pl.BlockSpec((1, ts, w), lambda b, i: (b, i, 0))
    params = (ml_conv_w, ml_conv_b, wq, wk, wv, gate_w_pad, gate_b_pad, hy_conv_w, hy_conv_b, rw_mu)
    outs = [(D_A, BF16), (D_A, BF16), (D_A, BF16), (D_A, F32), (LANES, F32),
            (D_B, F32), (D_B, F32), (3 * D_C, F32), (4 * LORA, F32)]
    return pl.pallas_call(
        _prep_kernel,
        grid=(bsz, ns),
        in_specs=[
            pl.BlockSpec((ts, N_H), lambda b, i: (b * ns + i, 0)),
            pl.BlockSpec((SUBLANES, N_H), lambda b, i: (jnp.maximum((b * ns + i) * hb - 1, 0), 0)),
            pl.BlockSpec((SUBLANES, N_H), lambda b, i: (jnp.minimum((b * ns + i + 1) * hb, last_hb), 0)),
        ] + [full(p) for p in params],
        out_specs=[seq_spec(w) for w, _ in outs],
        out_shape=[jax.ShapeDtypeStruct((bsz, seq, w), dt) for w, dt in outs],
        compiler_params=_cparams(("arbitrary", "arbitrary")),
        name="prep",
    )(proj_h, proj_h, proj_h, *params)


def _mlstm_kernel(q_ref, k_ref, v_ref, xc_ref, za_ref, gcol_ref, ng_ref, sk_ref, o_ref,
                  h_ref, ct_ref, n_ref, m_ref, num_ref, den_ref, bb_ref, mb_ref, kv_ref, sc_ref,
                  sel_ref):
    head = pl.program_id(1)
    seq = q_ref.shape[1]
    L = CHUNK_A
    nc = seq // L
    row = lax.broadcasted_iota(jnp.int32, (L, L), 0)
    colm = lax.broadcasted_iota(jnp.int32, (L, L), 1)
    lane = lax.broadcasted_iota(jnp.int32, (L, LANES), 1)

    grp = num_ref.shape[1]
    srow = lax.broadcasted_iota(jnp.int32, (SUBLANES, LANES), 0)
    ones = jnp.ones((LANES, LANES), BF16)
    ones2 = jnp.ones((2 * LANES, LANES), BF16)

    def dot(a, b):
        return jnp.dot(a, b, preferred_element_type=F32)

    def split3(x):
        x1 = x.astype(BF16)
        e1 = x - x1.astype(F32)
        x2 = e1.astype(BF16)
        return x1, x2, (e1 - x2.astype(F32)).astype(BF16)

    srow_i = lax.broadcasted_iota(jnp.int32, (3 * LANES, 2 * LANES), 0) % LANES
    scol_i = lax.broadcasted_iota(jnp.int32, (3 * LANES, 2 * LANES), 1)
    for d in range(2):
        li_lane = head + 2 * H_A * d
        want = jnp.where(scol_i < LANES, li_lane + H_A, li_lane)
        sel_ref[d] = jnp.where(srow_i == want, 1.0, 0.0).astype(BF16)

    def prepare(j, d, slot):
        fwd = d == 0
        tri = (colm <= row) if fwd else (colm >= row)
        c0 = pl.multiple_of(j * L, L)
        q = q_ref[0, pl.ds(c0, L), :]
        k = k_ref[0, pl.ds(c0, L), :]
        v = v_ref[0, pl.ds(c0, L), :]
        qk = _bdot_nt(q, k)
        bl = dot(jnp.concatenate(split3(gcol_ref[0, pl.ds(c0, L), :]), axis=1), sel_ref[d])
        yield
        b, li = bl[:, :LANES], bl[:, LANES:]
        x0, x1, x2 = split3(b)
        y0, y1, y2 = split3(li - b)
        xl = jnp.where(lane == 0, x0, jnp.where(lane == 1, x1, jnp.where(
            lane == 2, x2, jnp.where(lane < 6, 1.0, 0.0).astype(BF16))))
        yl = jnp.where(lane == 3, y0, jnp.where(lane == 4, y1, jnp.where(
            lane == 5, y2, jnp.where(lane < 3, 1.0, 0.0).astype(BF16))))
        dm = _bdot_nt(xl, yl)
        gtot = b[L - 1:L, :] if fwd else b[0:1, :]
        a = gtot - b + li
        a_max = jnp.max(a, axis=0, keepdims=True)
        wk = jnp.exp(a - a_max)
        vw = (v.astype(F32) * wk).astype(BF16)
        kv = lax.dot_general(k, vw, (((0,), (0,)), ((), ())), preferred_element_type=F32)
        kn = jnp.sum(k.astype(F32) * wk, axis=0, keepdims=True)
        sc_ref[d, slot] = jnp.where(srow == 0, kn, jnp.where(srow == 1, gtot, a_max))
        bb_ref[d, slot] = b
        yield
        dm = jnp.where(tri, dm, -jnp.inf)
        m_loc = jnp.max(dm, axis=-1, keepdims=True)
        s = qk * jnp.exp(dm - m_loc)
        num = _bdot(s, v)
        den = dot(jnp.concatenate(_split_bf16(s), axis=1), ones)
        mb_ref[d, slot] = jnp.broadcast_to(m_loc, (L, LANES))
        kv_ref[d, slot] = kv
        yield
        num_ref[d, slot] = num
        den_ref[d, slot] = den

    def advance(d, todo):
        ct, n, m = ct_ref[d], n_ref[d], m_ref[d]
        pending = []
        for j, slot in todo:
            c0 = pl.multiple_of(j * L, L)
            q = q_ref[0, pl.ds(c0, L), :]
            qc = _bdot(q, ct)
            qn = dot(jnp.concatenate(_split_bf16(q.astype(F32) * n), axis=1), ones2)
            sc = sc_ref[d, slot]
            kn, gtot, a_max = sc[0:1, :], sc[1:2, :], sc[2:3, :]
            m_new = jnp.maximum(gtot + m, a_max)
            decay = jnp.exp(gtot + m - m_new)
            beta = jnp.exp(a_max - m_new)
            pending.append((c0, slot, qc, qn, m))
            ct = decay * ct + beta * kv_ref[d, slot]
            n = decay * n + beta * kn
            m = m_new
        ct_ref[d], n_ref[d], m_ref[d] = ct, n, m
        yield
        for c0, slot, qc, qn, m_in in pending:
            inter = bb_ref[d, slot] + m_in
            m_loc = mb_ref[d, slot]
            m_t = jnp.maximum(inter, m_loc)
            c_intra = jnp.exp(m_loc - m_t)
            c_inter = jnp.exp(inter - m_t)
            num = c_intra * num_ref[d, slot] + c_inter * qc
            den = c_intra * den_ref[d, slot] + c_inter * qn
            h_ref[pl.ds(c0, L), :] += num / jnp.maximum(jnp.abs(den), jnp.exp(-m_t))

    ct_ref[...] = jnp.zeros_like(ct_ref)
    n_ref[...] = jnp.zeros_like(n_ref)
    m_ref[...] = jnp.full_like(m_ref, -jnp.inf)
    h_ref[...] = jnp.zeros_like(h_ref)
    lock = 4 if grp % 4 == 0 else 1

    def group(gi, carry):
        def phase_a(i, c):
            gens = []
            for u in range(lock):
                jf = gi * grp + i * lock + u
                gens += [prepare(jf, 0, i * lock + u), prepare(nc - 1 - jf, 1, i * lock + u)]
            _round_robin(gens)
            return c

        def phase_b(i, c):
            jf = [gi * grp + i * lock + u for u in range(lock)]
            _round_robin([advance(0, [(j, i * lock + u) for u, j in enumerate(jf)]),
                          advance(1, [(nc - 1 - j, i * lock + u) for u, j in enumerate(jf)])])
            return c

        lax.fori_loop(0, grp // lock, phase_a, 0)
        lax.fori_loop(0, grp // lock, phase_b, 0)
        return carry

    lax.fori_loop(0, nc // grp, group, 0)

    tile = 256
    def finish(i, carry):
        r0 = pl.multiple_of(i * tile, tile)
        hh = h_ref[pl.ds(r0, tile), :]
        mu = jnp.mean(hh, axis=-1, keepdims=True)
        hc = hh - mu
        var = jnp.mean(hc * hc, axis=-1, keepdims=True)
        hn = hc * lax.rsqrt(var + HEAD_NORM_EPS) * ng_ref[...]
        out = (hn + sk_ref[...] * xc_ref[0, pl.ds(r0, tile), :]) * _silu(za_ref[pl.ds(r0, tile), :])
        o_ref[0, pl.ds(r0, tile), :] = out
        return carry

    lax.fori_loop(0, seq // tile, finish, 0)


def _mlstm(q, k, v, xc, proj_g, gcol, norm_g, skip):
    bsz, seq, _ = q.shape
    grp = min(ML_GROUP, seq // CHUNK_A)
    hs = lambda: pl.BlockSpec((1, seq, DH_A), lambda b, h: (b, 0, h))
    return pl.pallas_call(
        _mlstm_kernel,
        grid=(bsz, H_A),
        in_specs=[
            hs(), hs(), hs(), hs(),
            pl.BlockSpec((seq, DH_A), lambda b, h: (b, G_AZ // DH_A + h)),
            pl.BlockSpec((1, seq, LANES), lambda b, h: (b, 0, 0)),
            pl.BlockSpec((1, DH_A), lambda b, h: (0, h)),
            pl.BlockSpec((1, DH_A), lambda b, h: (0, h)),
        ],
        out_specs=pl.BlockSpec((1, seq, DH_A), lambda b, h: (b, 0, h)),
        out_shape=jax.ShapeDtypeStruct((bsz, seq, D_A), F32),
        scratch_shapes=[pltpu.VMEM((seq, DH_A), F32), pltpu.VMEM((2, DH_A, DH_A), F32),
                        pltpu.VMEM((2, 1, DH_A), F32), pltpu.VMEM((2, 1, LANES), F32),
                        pltpu.VMEM((2, grp, CHUNK_A, DH_A), F32),
                        pltpu.VMEM((2, grp, CHUNK_A, LANES), F32),
                        pltpu.VMEM((2, grp, CHUNK_A, LANES), F32),
                        pltpu.VMEM((2, grp, CHUNK_A, LANES), F32),
                        pltpu.VMEM((2, grp, DH_A, DH_A), F32),
                        pltpu.VMEM((2, grp, SUBLANES, LANES), F32),
                        pltpu.VMEM((2, 3 * LANES, 2 * LANES), BF16)],
        compiler_params=_cparams(("arbitrary", "arbitrary")),
        name="mlstm",
    )(q, k, v, xc, proj_g, gcol, norm_g.reshape(1, D_A), skip.reshape(1, D_A))


FFT_N2 = 128
HY_CT = 128
HY_PASSES = 1
FFT_ROWS_UNROLL = 8
FFT_SLAB_UNROLL = 4


def _split_bf16(x):
    hi = x.astype(BF16)
    lo = (x - hi.astype(F32)).astype(BF16)
    return hi, lo


def _dot3(a_hi, a_lo, x):
    x_hi, x_lo = _split_bf16(x)
    d = lambda a, b: jnp.dot(a, b, preferred_element_type=F32)
    if HY_PASSES == 1:
        return d(a_hi, x_hi)
    return d(a_hi, x_hi) + (d(a_lo, x_hi) + d(a_hi, x_lo))


def _dft_constants(seq):
    n = 2 * seq
    n2 = FFT_N2
    n1 = n // n2
    f1 = np.arange(n1)[:, None]
    s1 = np.arange(n1)[None, :]
    th1 = 2.0 * np.pi * f1 * s1 / n1
    fwd1 = np.concatenate([np.cos(th1), -np.sin(th1)], axis=0)
    inv1 = np.concatenate([np.cos(th1), -np.sin(th1)], axis=1)[:n1 // 2] / n
    a = np.arange(n2)
    th2 = 2.0 * np.pi * a[:, None] * a[None, :] / n2
    c2, s2 = np.cos(th2), np.sin(th2)
    fwd2 = np.block([[c2, s2], [-s2, c2]])
    inv2 = np.block([[c2, -s2], [s2, c2]])
    tht = 2.0 * np.pi * a[:, None] * np.arange(n1)[None, :] / n
    tw = np.zeros((2, n2, LANES), np.float64)
    tw[0, :, :n1] = np.cos(tht)
    tw[1, :, :n1] = -np.sin(tht)

    def hl(m):
        m32 = jnp.asarray(m, F32)
        hi = m32.astype(BF16)
        lo = (m32 - hi.astype(F32)).astype(BF16)
        return jnp.stack([hi, lo])

    return dict(fwd1=hl(fwd1), inv1=hl(inv1), fwd2=hl(fwd2), inv2=hl(inv2),
                tw=jnp.asarray(tw, F32), n1=n1)


def _fft_stage_a(load_rows, k_rows, fwd1_ref, a_ref, n1):
    n2 = FFT_N2
    f_hi = fwd1_ref[0, :, :k_rows]
    f_lo = fwd1_ref[1, :, :k_rows]

    def one(s2):
        m = _dot3(f_hi, f_lo, load_rows(s2))
        yield
        a_ref[pl.ds(s2, n1, stride=2 * n2), :] = m[:n1]
        a_ref[pl.ds(n2 + s2, n1, stride=2 * n2), :] = m[n1:]

    def body(i, carry):
        _round_robin([one(i * FFT_ROWS_UNROLL + u) for u in range(FFT_ROWS_UNROLL)])
        return carry

    lax.fori_loop(0, n2 // FFT_ROWS_UNROLL, body, 0)


def _twiddle_cols(tw_ref, f1):
    lane = lax.broadcasted_iota(jnp.int32, (FFT_N2, LANES), 1)
    sel = lane == f1
    twr = jnp.sum(jnp.where(sel, tw_ref[0], 0.0), axis=-1, keepdims=True)
    twi = jnp.sum(jnp.where(sel, tw_ref[1], 0.0), axis=-1, keepdims=True)
    return twr, twi


def _hyena_filter_kernel(feat_ref, w1_ref, b1_ref, w2_ref, b2_ref, w3_ref, b3_ref, fr_ref,
                         wo_ref, dec_ref, fwd1_ref, fwd2_ref, tw_ref, o_ref, kt_ref, a_ref, hid_ref):
    n = feat_ref.shape[0]
    seq = n // 2
    n2 = FFT_N2
    n1 = n // n2
    tile = 512
    freq = fr_ref[...]

    @pl.when(pl.program_id(0) == 0)
    def _():
        def hidden(i, carry):
            r0 = pl.multiple_of(i * tile, tile)
            hid = jnp.sin(freq * (_hdot(feat_ref[pl.ds(r0, tile), :], w1_ref[...]) + b1_ref[...]))
            hid = jnp.sin(freq * (_hdot(hid, w2_ref[...]) + b2_ref[...]))
            hid_ref[pl.ds(r0, tile), :] = jnp.sin(freq * (_hdot(hid, w3_ref[...]) + b3_ref[...]))
            return carry

        lax.fori_loop(0, n // tile, hidden, 0)

    def gen(i, carry):
        r0 = pl.multiple_of(i * tile, tile)
        second = r0 >= seq
        wo = jnp.where(second, wo_ref[1, 0], wo_ref[0, 0])
        dec = jnp.where(second, dec_ref[1, 0], dec_ref[0, 0])
        t = feat_ref[pl.ds(r0, tile), 0:1]
        filt = _hdot(hid_ref[pl.ds(r0, tile), :], wo) * jnp.exp(-t * jnp.abs(dec))
        rows = r0 + lax.broadcasted_iota(jnp.int32, (tile, 1), 0)
        kt_ref[pl.ds(r0, tile), :] = jnp.where(rows == seq, 0.0, filt)
        return carry

    lax.fori_loop(0, n // tile, gen, 0)

    _fft_stage_a(lambda s2: kt_ref[pl.ds(s2, n1, stride=n2), :], n1, fwd1_ref, a_ref, n1)

    def slab(f1):
        r0 = pl.multiple_of(f1 * 2 * n2, 2 * n2)
        ar = a_ref[pl.ds(r0, n2), :]
        ai = a_ref[pl.ds(r0 + n2, n2), :]
        twr, twi = _twiddle_cols(tw_ref, f1)
        x = jnp.concatenate([ar * twr - ai * twi, ar * twi + ai * twr], axis=0)
        xf = _dot3(fwd2_ref[0], fwd2_ref[1], x)
        yield
        o_ref[0, pl.ds(r0, 2 * n2), :] = xf

    def slabs(i, carry):
        _round_robin([slab(i * FFT_SLAB_UNROLL + u) for u in range(FFT_SLAB_UNROLL)])
        return carry

    lax.fori_loop(0, n1 // FFT_SLAB_UNROLL, slabs, 0)


def _hyena_filter_spectrum(consts, feats_full, w1p, b1, w2, b2, w3, b3, freq, w_out, decay):
    n = feats_full.shape[0]
    nct = D_B // HY_CT
    full = lambda a: pl.BlockSpec(a.shape, lambda j: (0,) * a.ndim)
    wo = w_out.reshape(HY_HID, 2, nct, HY_CT).transpose(1, 2, 0, 3)
    dec = decay.reshape(2, nct, 1, HY_CT)
    small = (w1p, b1.reshape(1, -1), w2, b2.reshape(1, -1), w3, b3.reshape(1, -1), freq.reshape(1, -1))
    return pl.pallas_call(
        _hyena_filter_kernel,
        grid=(nct,),
        in_specs=[full(feats_full)] + [full(a) for a in small] + [
            pl.BlockSpec((2, 1, HY_HID, HY_CT), lambda j: (0, j, 0, 0)),
            pl.BlockSpec((2, 1, 1, HY_CT), lambda j: (0, j, 0, 0)),
            full(consts['fwd1']), full(consts['fwd2']), full(consts['tw']),
        ],
        out_specs=pl.BlockSpec((1, 2 * n, HY_CT), lambda j: (j, 0, 0)),
        out_shape=jax.ShapeDtypeStruct((nct, 2 * n, HY_CT), F32),
        scratch_shapes=[pltpu.VMEM((n, HY_CT), F32), pltpu.VMEM((2 * n, HY_CT), F32),
                        pltpu.VMEM((n, HY_HID), F32)],
        compiler_params=_cparams(("arbitrary",)),
        name="hyena_filter",
    )(feats_full, *small, wo, dec, consts['fwd1'], consts['fwd2'], consts['tw'])


def _hyena_conv_kernel(z_ref, x0_ref, zb_ref, ks_ref, bias_ref, fwd1_ref, inv1_ref, fwd2_ref,
                       inv2_ref, tw_ref, o_ref, a_ref, y_ref):
    seq = z_ref.shape[1]
    n2 = FFT_N2
    n1 = 2 * seq // n2
    nh = n1 // 2

    _fft_stage_a(lambda s2: z_ref[0, pl.ds(s2, nh, stride=n2), :], nh, fwd1_ref, a_ref, n1)

    def slab(f1):
        r0 = pl.multiple_of(f1 * 2 * n2, 2 * n2)
        ar = a_ref[pl.ds(r0, n2), :]
        ai = a_ref[pl.ds(r0 + n2, n2), :]
        twr, twi = _twiddle_cols(tw_ref, f1)
        x = jnp.concatenate([ar * twr - ai * twi, ar * twi + ai * twr], axis=0)
        xf = _dot3(fwd2_ref[0], fwd2_ref[1], x)
        yield
        xr, xi = xf[:n2], xf[n2:]
        kr = ks_ref[0, pl.ds(r0, n2), :]
        ki = ks_ref[0, pl.ds(r0 + n2, n2), :]
        y = jnp.concatenate([xr * kr - xi * ki, xr * ki + xi * kr], axis=0)
        bb = _dot3(inv2_ref[0], inv2_ref[1], y)
        yield
        br, bi = bb[:n2], bb[n2:]
        a_ref[pl.ds(r0, n2), :] = br * twr + bi * twi
        a_ref[pl.ds(r0 + n2, n2), :] = bi * twr - br * twi

    def slabs(i, carry):
        _round_robin([slab(i * FFT_SLAB_UNROLL + u) for u in range(FFT_SLAB_UNROLL)])
        return carry

    lax.fori_loop(0, n1 // FFT_SLAB_UNROLL, slabs, 0)

    i_hi = inv1_ref[0]
    i_lo = inv1_ref[1]

    def stage_d(s2):
        br = a_ref[pl.ds(s2, n1, stride=2 * n2), :]
        bi = a_ref[pl.ds(n2 + s2, n1, stride=2 * n2), :]
        y = _dot3(i_hi, i_lo, jnp.concatenate([br, bi], axis=0))
        yield
        y_ref[pl.ds(s2, nh, stride=n2), :] = y

    def stage_ds(i, carry):
        _round_robin([stage_d(i * FFT_ROWS_UNROLL + u) for u in range(FFT_ROWS_UNROLL)])
        return carry

    lax.fori_loop(0, n2 // FFT_ROWS_UNROLL, stage_ds, 0)

    tile = 256
    def finish(i, carry):
        r0 = pl.multiple_of(i * tile, tile)
        z = z_ref[0, pl.ds(r0, tile), :]
        y = y_ref[pl.ds(r0, tile), :] + bias_ref[...] * z
        o_ref[0, pl.ds(r0, tile), :] = x0_ref[0, pl.ds(r0, tile), :] * y * _silu(zb_ref[pl.ds(r0, tile), :])
        return carry

    lax.fori_loop(0, seq // tile, finish, 0)


def _hyena_conv(consts, z, x0, proj_g, kspec, bias):
    bsz, seq, _ = z.shape
    nct = D_B // HY_CT
    full = lambda a: pl.BlockSpec(a.shape, lambda j, b: (0,) * a.ndim)
    cs = pl.BlockSpec((1, seq, HY_CT), lambda j, b: (b, 0, j))
    mats = (consts['fwd1'], consts['inv1'], consts['fwd2'], consts['inv2'], consts['tw'])
    return pl.pallas_call(
        _hyena_conv_kernel,
        grid=(nct, bsz),
        in_specs=[
            cs, cs,
            pl.BlockSpec((seq, HY_CT), lambda j, b: (b, G_BZ // HY_CT + j)),
            pl.BlockSpec((1, 4 * seq, HY_CT), lambda j, b: (j, 0, 0)),
            pl.BlockSpec((1, HY_CT), lambda j, b: (0, j)),
        ] + [full(m) for m in mats],
        out_specs=cs,
        out_shape=jax.ShapeDtypeStruct((bsz, seq, D_B), F32),
        scratch_shapes=[pltpu.VMEM((4 * seq, HY_CT), F32), pltpu.VMEM((seq, HY_CT), F32)],
        compiler_params=_cparams(("arbitrary", "arbitrary")),
        name="hyena_conv",
    )(z, x0, proj_g, kspec, bias.reshape(1, D_B), *mats)


def _rwkv_kernel(r_ref, k_ref, v_ref, ul_ref, cz_ref, w0_ref, w2_ref, a0_ref, a2_ref,
                 kks_ref, ka_ref, rk_ref, lg_ref, lb_ref, o_ref, y_ref, kb_ref, s_ref,
                 wr_ref, u_ref, arb_ref, y0_ref, bt_ref, kv_ref, egc_ref):
    seq = r_ref.shape[1]
    L = CHUNK_C
    nc = seq // L
    P2 = 2 * L
    lane = lax.broadcasted_iota(jnp.int32, (L, LANES), 1)
    head0 = lane < HEAD_C
    prow = lax.broadcasted_iota(jnp.int32, (P2, P2), 0)
    pcol = lax.broadcasted_iota(jnp.int32, (P2, P2), 1)
    bd = (prow // HEAD_C) == (pcol // HEAD_C)
    blk = jnp.where(bd, 1.0, 0.0).astype(F32)
    trow = lax.broadcasted_iota(jnp.int32, (L, P2), 0)
    scol = lax.broadcasted_iota(jnp.int32, (L, P2), 1) % L
    eye = jnp.where(trow == scol, 1.0, 0.0).astype(F32)
    crow = lax.broadcasted_iota(jnp.int32, (L, L), 0)
    ccol = lax.broadcasted_iota(jnp.int32, (L, L), 1)

    def stack(x):
        return jnp.concatenate([jnp.where(head0, x, 0.0), jnp.where(head0, 0.0, x)], axis=0)

    blk_b = blk.astype(BF16)
    grp = wr_ref.shape[1]

    def dot(a, b):
        return jnp.dot(a, b, preferred_element_type=F32)

    def each(f, *cols):
        return [f(*xs) for xs in zip(*cols)]

    def prepare(probs):
        dirs = [d for _, d, _ in probs]
        fwd = [d == 0 for d in dirs]
        strict = [(scol < trow) if f else (scol > trow) for f in fwd]
        incl = [(scol <= trow) if f else (scol >= trow) for f in fwd]
        cum = [jnp.where((ccol <= crow) if f else (ccol >= crow), 1.0, 0.0).astype(BF16) for f in fwd]
        c0 = [pl.multiple_of(j * L, L) for j, _, _ in probs]
        r = [r_ref[0, pl.ds(c, L), :] for c in c0]
        k = [k_ref[0, pl.ds(c, L), :] for c in c0]
        v = [v_ref[0, pl.ds(c, L), :] for c in c0]
        lwla = [ul_ref[0, pl.ds(c, L), :] for c in c0]
        xw = [w0_ref[d:d + 1, :] + _bdot(jnp.tanh(x[:, :LANES]), w2_ref[d]) for x, d in zip(lwla, dirs)]
        lwt = [-jnp.exp(_log_sigmoid(x) - 0.5) for x in xw]
        a = [_sigmoid(a0_ref[d:d + 1, :] + _bdot(x[:, LANES:], a2_ref[d])) for x, d in zip(lwla, dirs)]
        kkv = [x * kks_ref[...] for x in k]
        ss = [dot((x * x).astype(BF16), blk_b) for x in kkv]
        yield
        kk = each(lambda x, s: x / jnp.maximum(jnp.sqrt(s), 1e-12), kkv, ss)
        kd = each(lambda x, y: x * (1.0 + (y - 1.0) * ka_ref[...]), k, a)
        for c, x in zip(c0, kd):
            kb_ref[pl.ds(c, L), :] += x
        be = each(lambda x, y: x * y, kk, a)

        def cumsum(cm, x):
            l1 = x.astype(BF16)
            e1 = x - l1.astype(F32)
            l2 = e1.astype(BF16)
            l3 = (e1 - l2.astype(F32)).astype(BF16)
            return dot(jnp.concatenate([cm, cm, cm], axis=1), jnp.concatenate([l1, l2, l3], axis=0))

        g = each(cumsum, cum, lwt)
        yield
        eng = [jnp.exp(-x) for x in g]
        egc = [jnp.exp(x[L - 1:L, :] if f else x[0:1, :]) for x, f in zip(g, fwd)]
        at = each(lambda x, y, z: (-x * jnp.exp(y - z)).astype(BF16), kk, g, lwt)
        rt_ = each(lambda x, y: (x * jnp.exp(y)).astype(BF16), r, g)
        bh = each(lambda x, y: x * y, be, eng)
        kh = each(lambda x, y: x * y, kd, eng)
        lhs = each(lambda x, y: jnp.concatenate([x, y], axis=0), at, rt_)
        rhs = each(lambda x, y: jnp.concatenate([stack(x), stack(y)], axis=0).astype(BF16), bh, kh)
        g1 = each(lambda x, y: lax.dot_general(x, y, (((1,), (1,)), ((), ())),
                                               preferred_element_type=F32), lhs, rhs)
        yield
        aab = each(lambda m, x: jnp.where(m, x[:L, :P2], 0.0), strict, g1)
        aak_ark = each(lambda ms, mi, x: jnp.concatenate(
            [jnp.where(ms, x[:L, P2:], 0.0), jnp.where(mi, x[L:, P2:], 0.0)], axis=0).astype(BF16),
            strict, incl, g1)
        for (_, d, slot), mi, x in zip(probs, incl, g1):
            arb_ref[d, slot] = jnp.where(mi, x[L:, :P2], 0.0).astype(BF16)
        t = [eye + x for x in aab]
        p = [x.astype(BF16) for x in aab]
        vs = [stack(x).astype(BF16) for x in v]
        av = each(dot, aak_ark, vs)
        p = [dot(x, stack(x)) for x in p]
        yield
        p = [x.astype(BF16) for x in p]
        sq = 2
        while sq < L:
            if 2 * sq >= L:
                pt = each(lambda x, y: dot(x, stack(y.astype(BF16))), p, t)
                yield
                t = each(lambda y, x: y + x, t, pt)
            else:
                pt = each(lambda x, y: dot(x, jnp.concatenate([stack(x), stack(y.astype(BF16))], axis=1)),
                          p, t)
                yield
                p = [x[:, :P2].astype(BF16) for x in pt]
                t = each(lambda y, x: y + x[:, P2:], t, pt)
            sq *= 2
        wu = each(lambda x, y, z: dot(x.astype(BF16), jnp.concatenate(
            [stack(y), stack(z[:L].astype(BF16))], axis=1)), t, at, av)
        kv = each(lambda x, y, z: lax.dot_general(
            x.astype(BF16), (y * z).astype(BF16), (((0,), (0,)), ((), ())), preferred_element_type=F32),
            v, kh, egc)
        yield
        for i, (_, d, slot) in enumerate(probs):
            wr_ref[d, slot] = jnp.concatenate([wu[i][:, :P2].astype(BF16), rt_[i]], axis=0)
            u_ref[d, slot] = wu[i][:, P2:].astype(BF16)
            y0_ref[d, slot] = av[i][L:]
            bt_ref[d, slot] = (bh[i] * egc[i]).astype(BF16)
            kv_ref[d, slot] = jnp.where(bd, kv[i], 0.0).astype(BF16)
            egc_ref[d, slot] = egc[i]

    def advance(steps):
        s = [s_ref[d] for d in range(2)]
        for probs in steps:
            x = [lax.dot_general(wr_ref[d, slot], s[d].astype(BF16), (((1,), (1,)), ((), ())),
                                 preferred_element_type=F32) for _, d, slot in probs]
            yield
            sa = [(y[:L] + u_ref[d, slot].astype(F32)).astype(BF16) for (_, d, slot), y in zip(probs, x)]
            upd = [lax.dot_general(y, bt_ref[d, slot], (((0,), (0,)), ((), ())),
                                   preferred_element_type=F32) for (_, d, slot), y in zip(probs, sa)]
            z = [dot(arb_ref[d, slot], stack(y)) for (_, d, slot), y in zip(probs, sa)]
            yield
            for i, (j, d, slot) in enumerate(probs):
                s[d] = (s[d] * egc_ref[d, slot] + kv_ref[d, slot].astype(F32)
                        + jnp.where(bd, upd[i], 0.0))
                c0 = pl.multiple_of(j * L, L)
                y_ref[pl.ds(c0, L), :] += x[i][L:] + z[i] + y0_ref[d, slot]
        for d in range(2):
            s_ref[d] = s[d]

    s_ref[...] = jnp.zeros_like(s_ref)
    y_ref[...] = jnp.zeros_like(y_ref)
    kb_ref[...] = jnp.zeros_like(kb_ref)
    lock = grp // 2
    nsets = nc // lock

    def chunk_set(i):
        base = (i % 2) * lock
        return [[(i * lock + q, 0, base + q), (nc - 1 - (i * lock + q), 1, base + q)]
                for q in range(lock)]

    _round_robin([prepare(sum(chunk_set(0), []))])

    def pipelined(i, carry):
        _round_robin([prepare(sum(chunk_set(i + 1), [])), advance(chunk_set(i))])
        return carry

    lax.fori_loop(0, nsets - 1, pipelined, 0)
    _round_robin([advance(chunk_set(nsets - 1))])

    tile = 256
    inv_n = 1.0 / HEAD_C
    blk2 = jnp.concatenate([blk_b, blk_b], axis=0)

    def head_sum(x):
        hi, lo = _split_bf16(x)
        return dot(jnp.concatenate([hi, lo], axis=1), blk2)

    def finish(i, carry):
        r0 = pl.multiple_of(i * tile, tile)
        y = y_ref[pl.ds(r0, tile), :]
        mu = head_sum(y) * inv_n
        yc = y - mu
        var = head_sum(yc * yc) * inv_n
        yn = yc * lax.rsqrt(var + LNX_EPS) * lg_ref[...] + lb_ref[...]
        r = r_ref[0, pl.ds(r0, tile), :]
        v = v_ref[0, pl.ds(r0, tile), :]
        kbon = 0.5 * kb_ref[pl.ds(r0, tile), :]
        bonus = head_sum(r * kbon * rk_ref[...]) * v
        o_ref[0, pl.ds(r0, tile), :] = (yn + bonus) * _silu(cz_ref[pl.ds(r0, tile), :])
        return carry

    lax.fori_loop(0, seq // tile, finish, 0)


def _rwkv(u, ul, proj_g, w0, w2pad, a0, a2pad, kk_s, ka, rk, lnx_g, lnx_b):
    bsz, seq, _ = u.shape
    npair = D_C // LANES
    grp = min(RW_GROUP, seq // CHUNK_C)
    col = lambda off: pl.BlockSpec((1, seq, LANES), lambda b, p: (b, 0, off + p))
    vec = pl.BlockSpec((1, LANES), lambda b, p: (0, p))
    two = pl.BlockSpec((2, LANES), lambda b, p: (0, p))
    lora = pl.BlockSpec((2, LANES, LANES), lambda b, p: (0, 0, p))
    row = lambda a: a.reshape(1, D_C)
    return pl.pallas_call(
        _rwkv_kernel,
        grid=(bsz, npair),
        in_specs=[
            col(0), col(npair), col(2 * npair),
            pl.BlockSpec((1, seq, 2 * LANES), lambda b, p: (b, 0, 0)),
            pl.BlockSpec((seq, LANES), lambda b, p: (b, G_CZ // LANES + p)),
            two, lora, two, lora, vec, vec, vec, vec, vec,
        ],
        out_specs=pl.BlockSpec((1, seq, LANES), lambda b, p: (b, 0, p)),
        out_shape=jax.ShapeDtypeStruct((bsz, seq, D_C), F32),
        scratch_shapes=[pltpu.VMEM((seq, LANES), F32), pltpu.VMEM((seq, LANES), F32),
                        pltpu.VMEM((2, LANES, LANES), F32),
                        pltpu.VMEM((2, grp, 2 * CHUNK_C, LANES), BF16),
                        pltpu.VMEM((2, grp, CHUNK_C, LANES), BF16),
                        pltpu.VMEM((2, grp, CHUNK_C, LANES), BF16),
                        pltpu.VMEM((2, grp, CHUNK_C, LANES), F32),
                        pltpu.VMEM((2, grp, CHUNK_C, LANES), BF16),
                        pltpu.VMEM((2, grp, LANES, LANES), BF16),
                        pltpu.VMEM((2, grp, 1, LANES), F32)],
        compiler_params=_cparams(("arbitrary", "arbitrary")),
        name="rwkv7",
    )(u, u, u, ul, proj_g, w0, w2pad, a0, a2pad, row(kk_s), row(ka), row(rk), row(lnx_g), row(lnx_b))


def _out_kernel(ya_ref, yb_ref, yc_ref, ga_ref, gb_ref, gc_ref, x_ref, gate_ref, pg_ref,
                wa_ref, wb_ref, wc_ref, wo_ref, o_ref):
    merged = (_sigmoid(ga_ref[...]) * _bdot(ya_ref[...], wa_ref[...])
              + _sigmoid(gb_ref[...]) * _bdot(yb_ref[...], wb_ref[...])
              + _sigmoid(gc_ref[...]) * _bdot(yc_ref[...], wc_ref[...]))
    out = _bdot(merged, wo_ref[...])
    y = out * lax.rsqrt(jnp.mean(out * out, axis=-1, keepdims=True) + NORM_EPS) * pg_ref[...]
    o_ref[...] = x_ref[...] + gate_ref[0] * y


def _merge_out(ya, yb, yc, proj_g, x2d, seq, gate, post_g, wa, wb, wc, wo, tm):
    rows, d = x2d.shape
    per_b = seq // tm
    full = lambda a: pl.BlockSpec(a.shape, lambda i: (0,) * a.ndim)
    gcol = lambda off: pl.BlockSpec((tm, d), lambda i: (i, off // d))
    return pl.pallas_call(
        _out_kernel,
        grid=(rows // tm,),
        in_specs=[
            pl.BlockSpec((tm, D_A), lambda i: (i, 0)),
            pl.BlockSpec((tm, D_B), lambda i: (i, 0)),
            pl.BlockSpec((tm, D_C), lambda i: (i, 0)),
            gcol(G_GA), gcol(G_GB), gcol(G_GC),
            pl.BlockSpec((tm, d), lambda i: (i, 0)),
            pl.BlockSpec((1, 1, d), lambda i: (i // per_b, 0, 0)),
            pl.BlockSpec((1, d), lambda i: (0, 0)),
            full(wa), full(wb), full(wc), full(wo),
        ],
        out_specs=pl.BlockSpec((tm, d), lambda i: (i, 0)),
        out_shape=jax.ShapeDtypeStruct((rows, d), F32),
        compiler_params=_cparams(("arbitrary",)),
        name="merge_out",
    )(ya.reshape(rows, D_A), yb.reshape(rows, D_B), yc.reshape(rows, D_C), proj_g, proj_g, proj_g,
      x2d, gate, post_g.reshape(1, d), wa, wb, wc, wo)


def _hyena_feats(seq):
    n = 2 * seq
    t = jnp.linspace(0.0, 1.0, seq, dtype=F32)
    w = 2.0 * math.pi * jnp.arange(seq, dtype=F32) / seq
    f = jnp.linspace(1e-4, HY_BANDS - 1, HY_BANDS, dtype=F32)
    zz = w[:, None] * f[None, :]
    feats = jnp.concatenate([t[:, None], jnp.cos(zz), -jnp.sin(zz)], axis=-1)
    pos = np.concatenate([np.arange(seq), [0], np.arange(seq - 1, 0, -1)])
    full = jnp.zeros((n, LANES), F32).at[:, :feats.shape[1]].set(feats[pos])
    return full


def _tiles(seq):
    return min(seq, 1024), min(seq, 256), min(seq, 512)


def kernel(x, c, ada_w, ada_b, pre_g, post_g, w_in, ml_conv_w, ml_conv_b, ml_wq, ml_wk, ml_wv, ml_gate_w, ml_gate_b, ml_norm_g, ml_skip, hy_conv_w, hy_conv_b, hy_w1, hy_b1, hy_w2, hy_b2, hy_w3, hy_b3, hy_freq, hy_w_out, hy_decay, hy_bias, rw_mu, rw_w0, rw_w2, rw_a0, rw_a2, rw_kk, rw_ka, rw_rk, rw_lnx_g, rw_lnx_b, w_branch_a, w_branch_b, w_branch_c, w_out):
    bsz, seq, d = x.shape
    depth = ada_w.shape[0]
    tm_in, ts_prep, tm_out = _tiles(seq)
    perm_h, perm_g = _ref_column_perm()
    consts = _dft_constants(seq)
    feats_full = _hyena_feats(seq)
    mod = _modulation(c, ada_w, ada_b)
    x2d = x.reshape(bsz * seq, d)
    n_gate = 4 * H_A
    for l in range(depth):
        shift = mod[l, :, None, 0:d]
        scale = mod[l, :, None, d:2 * d]
        gate = mod[l, :, None, 2 * d:3 * d]
        w_h = w_in[l][:, perm_h].astype(BF16)
        w_g = w_in[l][:, perm_g].astype(BF16)
        proj_h = _inproj(x2d, seq, pre_g[l], shift, scale, w_h, tm_in, N_H // 3)
        proj_g = _inproj(x2d, seq, pre_g[l], shift, scale, w_g, tm_in, N_G // 4)

        gw = jnp.zeros((3 * D_A, LANES), F32).at[:, :n_gate].set(ml_gate_w[l])
        gb = jnp.zeros((1, LANES), F32).at[0, :n_gate].set(ml_gate_b[l])
        q, k, v, xc, gcol, z, x0, u, ul = _prep(
            proj_h, bsz, seq, ts_prep, ml_conv_w[l], ml_conv_b[l].reshape(1, D_A),
            ml_wq[l], ml_wk[l], ml_wv[l], gw, gb,
            hy_conv_w[l], hy_conv_b[l].reshape(1, 3 * D_B), rw_mu[l].reshape(1, -1))

        y_a = _mlstm(q, k, v, xc, proj_g, gcol, ml_norm_g[l], ml_skip[l])

        w1p = jnp.zeros((LANES, HY_HID), F32).at[:hy_w1.shape[1]].set(hy_w1[l])
        kspec = _hyena_filter_spectrum(consts, feats_full, w1p, hy_b1[l], hy_w2[l], hy_b2[l],
                                       hy_w3[l], hy_b3[l], hy_freq[l], hy_w_out[l], hy_decay[l])
        y_b = _hyena_conv(consts, z, x0, proj_g, kspec, hy_bias[l])

        w2pad = (jnp.zeros((2, LANES, D_C), F32).at[0, :LORA].set(rw_w2[l, 0])
                 .at[1, LORA:].set(rw_w2[l, 1]))
        a2pad = (jnp.zeros((2, LANES, D_C), F32).at[0, :LORA].set(rw_a2[l, 0])
                 .at[1, LORA:].set(rw_a2[l, 1]))
        y_c = _rwkv(u, ul, proj_g, rw_w0[l], w2pad, rw_a0[l], a2pad, rw_kk[l], rw_ka[l],
                    rw_rk[l].reshape(-1), rw_lnx_g[l], rw_lnx_b[l])

        x2d = _merge_out(y_a, y_b, y_c, proj_g, x2d, seq, gate, post_g[l],
                         w_branch_a[l].astype(BF16), w_branch_b[l].astype(BF16),
                         w_branch_c[l].astype(BF16), w_out[l].astype(BF16), tm_out)
    return x2d.reshape(bsz, seq, d)
```

```python
import functools
import math

import numpy as np
import jax
import jax.numpy as jnp
from jax import lax
from jax.experimental import pallas as pl
from jax.experimental.pallas import tpu as pltpu

D_MODEL = 1024
DEPTH = 4
D_A = 512
H_A = 4
DH_A = 128
CHUNK_A = 64
ML_GROUP = 16
D_B = 512
HY_BANDS = 16
HY_HID = 64
D_C = 1024
HEAD_C = 64
H_C = D_C // HEAD_C
LORA = 64
CHUNK_C = 64
RW_GROUP = 8
LNX_EPS = 64e-5
NORM_EPS = 1e-6
HEAD_NORM_EPS = 1e-5

LANES = 128
SUBLANES = 8
VMEM_LIMIT = 56 * 1024 * 1024

F32 = jnp.float32
BF16 = jnp.bfloat16
HIGHEST = lax.Precision.HIGHEST

N_H = D_A + 3 * D_B + 3 * D_C + 4 * LORA
N_G = D_A + D_B + D_C + 3 * D_MODEL
H_AX, H_BV, H_BX0, H_BX1, H_CR, H_CK, H_CV, H_LW, H_LA = (
    0, 512, 1024, 1536, 2048, 3072, 4096, 5120, 5248)
G_AZ, G_BZ, G_CZ, G_GA, G_GB, G_GC = 0, 512, 1024, 2048, 3072, 4096


def _ref_column_perm():
    r = lambda a, n: np.arange(a, a + n)
    a_x, a_z = r(0, 512), r(512, 512)
    b_v, b_x0, b_x1, b_z = r(1024, 512), r(1536, 512), r(2048, 512), r(2560, 512)
    c_r, c_k, c_v = r(3072, 1024), r(4096, 1024), r(5120, 1024)
    c_lw, c_la = r(6144, 128), r(6272, 128)
    c_z = r(6400, 1024)
    g_a, g_b, g_c = r(7424, 1024), r(8448, 1024), r(9472, 1024)
    h = np.concatenate([a_x, b_v, b_x0, b_x1, c_r, c_k, c_v, c_lw, c_la])
    g = np.concatenate([a_z, b_z, c_z, g_a, g_b, g_c])
    return h, g


def _cparams(sem):
    return pltpu.CompilerParams(dimension_semantics=sem, vmem_limit_bytes=VMEM_LIMIT)


def _silu(x):
    return x * (1.0 / (1.0 + jnp.exp(-x)))


def _sigmoid(x):
    return 1.0 / (1.0 + jnp.exp(-x))


def _bdot(a, b):
    return jnp.dot(a.astype(BF16), b.astype(BF16), preferred_element_type=F32)


def _bdot_nt(a, b):
    return lax.dot_general(a.astype(BF16), b.astype(BF16), (((1,), (1,)), ((), ())),
                           preferred_element_type=F32)


def _round_robin(gens):
    gens = list(gens)
    while gens:
        alive = []
        for g in gens:
            try:
                next(g)
                alive.append(g)
            except StopIteration:
                pass
        gens = alive


def _hdot(a, b):
    return jnp.dot(a, b, preferred_element_type=F32, precision=HIGHEST)


def _mod_kernel(c_ref, w_ref, b_ref, o_ref):
    cond = _silu(c_ref[...])
    o_ref[0] = _hdot(cond, w_ref[0]) + b_ref[0]


def _modulation(c, ada_w, ada_b):
    depth, d, n3 = ada_w.shape
    bsz = c.shape[0]
    nt = n3 // d
    return pl.pallas_call(
        _mod_kernel,
        grid=(depth, nt),
        in_specs=[
            pl.BlockSpec((bsz, d), lambda l, j: (0, 0)),
            pl.BlockSpec((1, d, d), lambda l, j: (l, 0, j)),
            pl.BlockSpec((1, 1, d), lambda l, j: (l, 0, j)),
        ],
        out_specs=pl.BlockSpec((1, bsz, d), lambda l, j: (l, 0, j)),
        out_shape=jax.ShapeDtypeStruct((depth, bsz, n3), F32),
        compiler_params=_cparams(("arbitrary", "arbitrary")),
        name="adaln_mod",
    )(c, ada_w, ada_b.reshape(depth, 1, n3))


def _inproj_kernel(x_ref, g_ref, shift_ref, scale_ref, w_ref, o_ref, h_ref):
    @pl.when(pl.program_id(1) == 0)
    def _():
        x = x_ref[...]
        y = x * lax.rsqrt(jnp.mean(x * x, axis=-1, keepdims=True) + NORM_EPS)
        h = y * g_ref[...] * (1.0 + scale_ref[0]) + shift_ref[0]
        h_ref[...] = h.astype(BF16)

    o_ref[...] = jnp.dot(h_ref[...], w_ref[...], preferred_element_type=F32)


def _inproj(x2d, seq, pre_g, shift, scale, w_bf16, tm, tn):
    rows, d = x2d.shape
    n = w_bf16.shape[1]
    per_b = seq // tm
    return pl.pallas_call(
        _inproj_kernel,
        grid=(rows // tm, n // tn),
        in_specs=[
            pl.BlockSpec((tm, d), lambda i, j: (i, 0)),
            pl.BlockSpec((1, d), lambda i, j: (0, 0)),
            pl.BlockSpec((1, 1, d), lambda i, j: (i // per_b, 0, 0)),
            pl.BlockSpec((1, 1, d), lambda i, j: (i // per_b, 0, 0)),
            pl.BlockSpec((d, tn), lambda i, j: (0, j)),
        ],
        out_specs=pl.BlockSpec((tm, tn), lambda i, j: (i, j)),
        out_shape=jax.ShapeDtypeStruct((rows, n), F32),
        scratch_shapes=[pltpu.VMEM((tm, d), BF16)],
        compiler_params=_cparams(("arbitrary", "arbitrary")),
        name="inproj",
    )(x2d, pre_g.reshape(1, d), shift, scale, w_bf16)


def _log_sigmoid(x):
    return jnp.minimum(x, 0.0) - jnp.log(1.0 + jnp.exp(-jnp.abs(x)))


def _prep_kernel(main_ref, prev_ref, next_ref,
                 mcw_ref, mcb_ref, wq_ref, wk_ref, wv_ref, gw_ref, gb_ref,
                 hcw_ref, hcb_ref, mu_ref,
                 q_ref, k_ref, v_ref, xc_ref, gcol_ref, z_ref, x0_ref, u_ref, ul_ref):
    i = pl.program_id(1)
    ts = main_ref.shape[0]
    has_prev = jnp.where(i > 0, 1.0, 0.0).astype(F32)
    has_next = jnp.where(i < pl.num_programs(1) - 1, 1.0, 0.0).astype(F32)

    def neighbours(c0, cw):
        x = main_ref[:, c0:c0 + cw]
        row = lax.broadcasted_iota(jnp.int32, (ts, cw), 0)
        p_row = prev_ref[SUBLANES - 1:SUBLANES, c0:c0 + cw] * has_prev
        n_row = next_ref[0:1, c0:c0 + cw] * has_next
        xp = jnp.where(row == 0, p_row, pltpu.roll(x, 1, 0))
        xn = jnp.where(row == ts - 1, n_row, pltpu.roll(x, ts - 1, 0))
        return xp, x, xn

    def conv(c0, cw, w_ref, b_ref, w0):
        xp, x, xn = neighbours(c0, cw)
        w = w_ref[:, w0:w0 + cw]
        return xp * w[0:1] + x * w[1:2] + xn * w[2:3] + b_ref[:, w0:w0 + cw], x

    conv_a, xa = conv(H_AX, D_A, mcw_ref, mcb_ref, 0)
    xc = _silu(conv_a)
    xc_ref[0] = xc
    gates = jnp.zeros((ts, LANES), F32) + gb_ref[...]
    for h in range(H_A):
        sl = slice(h * DH_A, (h + 1) * DH_A)
        qh = _bdot(xc[:, sl], wq_ref[h])
        kh = _bdot(xc[:, sl], wk_ref[h])
        vh = _bdot(xa[:, sl], wv_ref[h])
        gates += (_bdot(qh, gw_ref[h * DH_A:(h + 1) * DH_A])
                  + _bdot(kh, gw_ref[D_A + h * DH_A:D_A + (h + 1) * DH_A])
                  + _bdot(vh, gw_ref[2 * D_A + h * DH_A:2 * D_A + (h + 1) * DH_A]))
        q_ref[0, :, sl] = qh.astype(BF16)
        k_ref[0, :, sl] = (kh * (DH_A ** -0.5)).astype(BF16)
        v_ref[0, :, sl] = vh.astype(BF16)
    col = lax.broadcasted_iota(jnp.int32, (ts, LANES), 1)
    rmod = lax.broadcasted_iota(jnp.int32, (ts, LANES), 0) % CHUNK_A
    lf = _log_sigmoid(gates)
    cf = lf
    cb = lf
    sh = 1
    while sh < CHUNK_A:
        cf = cf + jnp.where(rmod >= sh, pltpu.roll(cf, sh, 0), 0.0)
        cb = cb + jnp.where(rmod < CHUNK_A - sh, pltpu.roll(cb, ts - sh, 0), 0.0)
        sh *= 2
    is_ff = (col >= H_A) & (col < 2 * H_A)
    is_fb = (col >= 3 * H_A) & (col < 4 * H_A)
    gcol_ref[0] = jnp.where(is_ff, cf, jnp.where(is_fb, cb, gates))

    cv, _ = conv(H_BV, D_B, hcw_ref, hcb_ref, 0)
    cx0, _ = conv(H_BX0, D_B, hcw_ref, hcb_ref, D_B)
    cx1, _ = conv(H_BX1, D_B, hcw_ref, hcb_ref, 2 * D_B)
    z_ref[0] = cv * cx1
    x0_ref[0] = cx0

    cw = 512
    for j in range(3 * D_C // cw):
        xp, x, xn = neighbours(H_CR + j * cw, cw)
        mu = mu_ref[:, j * cw:(j + 1) * cw]
        u_ref[0, :, j * cw:(j + 1) * cw] = x + mu * (0.5 * (xp + xn) - x)
    xp, x, xn = neighbours(H_LW, 4 * LORA)
    mu = mu_ref[:, 3 * D_C:3 * D_C + 4 * LORA]
    ul_ref[0] = x + mu * (0.5 * (xp + xn) - x)


def _prep(proj_h, bsz, seq, ts, ml_conv_w, ml_conv_b, wq, wk, wv, gate_w_pad, gate_b_pad,
          hy_conv_w, hy_conv_b, rw_mu):
    ns = seq // ts
    hb = ts // SUBLANES
    last_hb = bsz * seq // SUBLANES - 1
    full = lambda a: pl.BlockSpec(a.shape, lambda b, i: (0,) * a.ndim)
    seq_spec = lambda w: pl.BlockSpec((1, ts, w), lambda b, i: (b, i, 0))
    params = (ml_conv_w, ml_conv_b, wq, wk, wv, gate_w_pad, gate_b_pad, hy_conv_w, hy_conv_b, rw_mu)
    outs = [(D_A, BF16), (D_A, BF16), (D_A, BF16), (D_A, F32), (LANES, F32),
            (D_B, F32), (D_B, F32), (3 * D_C, F32), (4 * LORA, F32)]
    return pl.pallas_call(
        _prep_kernel,
        grid=(bsz, ns),
        in_specs=[
            pl.BlockSpec((ts, N_H), lambda b, i: (b * ns + i, 0)),
            pl.BlockSpec((SUBLANES, N_H), lambda b, i: (jnp.maximum((b * ns + i) * hb - 1, 0), 0)),
            pl.BlockSpec((SUBLANES, N_H), lambda b, i: (jnp.minimum((b * ns + i + 1) * hb, last_hb), 0)),
        ] + [full(p) for p in params],
        out_specs=[seq_spec(w) for w, _ in outs],
        out_shape=[jax.ShapeDtypeStruct((bsz, seq, w), dt) for w, dt in outs],
        compiler_params=_cparams(("arbitrary", "arbitrary")),
        name="prep",
    )(proj_h, proj_h, proj_h, *params)


def _mlstm_kernel(q_ref, k_ref, v_ref, xc_ref, za_ref, gcol_ref, ng_ref, sk_ref, o_ref,
                  h_ref, ct_ref, n_ref, m_ref, num_ref, den_ref, bb_ref, mb_ref, kv_ref, sc_ref,
                  sel_ref):
    head = pl.program_id(1)
    seq = q_ref.shape[1]
    L = CHUNK_A
    nc = seq // L
    row = lax.broadcasted_iota(jnp.int32, (L, L), 0)
    colm = lax.broadcasted_iota(jnp.int32, (L, L), 1)
    lane = lax.broadcasted_iota(jnp.int32, (L, LANES), 1)

    grp = num_ref.shape[1]
    srow = lax.broadcasted_iota(jnp.int32, (SUBLANES, LANES), 0)
    ones = jnp.ones((LANES, LANES), BF16)
    ones2 = jnp.ones((2 * LANES, LANES), BF16)

    def dot(a, b):
        return jnp.dot(a, b, preferred_element_type=F32)

    def split3(x):
        x1 = x.astype(BF16)
        e1 = x - x1.astype(F32)
        x2 = e1.astype(BF16)
        return x1, x2, (e1 - x2.astype(F32)).astype(BF16)

    srow_i = lax.broadcasted_iota(jnp.int32, (3 * LANES, 2 * LANES), 0) % LANES
    scol_i = lax.broadcasted_iota(jnp.int32, (3 * LANES, 2 * LANES), 1)
    for d in range(2):
        li_lane = head + 2 * H_A * d
        want = jnp.where(scol_i < LANES, li_lane + H_A, li_lane)
        sel_ref[d] = jnp.where(srow_i == want, 1.0, 0.0).astype(BF16)

    def prepare(j, d, slot):
        fwd = d == 0
        tri = (colm <= row) if fwd else (colm >= row)
        c0 = pl.multiple_of(j * L, L)
        q = q_ref[0, pl.ds(c0, L), :]
        k = k_ref[0, pl.ds(c0, L), :]
        v = v_ref[0, pl.ds(c0, L), :]
        qk = _bdot_nt(q, k)
        bl = dot(jnp.concatenate(split3(gcol_ref[0, pl.ds(c0, L), :]), axis=1), sel_ref[d])
        yield
        b, li = bl[:, :LANES], bl[:, LANES:]
        x0, x1, x2 = split3(b)
        y0, y1, y2 = split3(li - b)
        xl = jnp.where(lane == 0, x0, jnp.where(lane == 1, x1, jnp.where(
            lane == 2, x2, jnp.where(lane < 6, 1.0, 0.0).astype(BF16))))
        yl = jnp.where(lane == 3, y0, jnp.where(lane == 4, y1, jnp.where(
            lane == 5, y2, jnp.where(lane < 3, 1.0, 0.0).astype(BF16))))
        dm = _bdot_nt(xl, yl)
        gtot = b[L - 1:L, :] if fwd else b[0:1, :]
        a = gtot - b + li
        a_max = jnp.max(a, axis=0, keepdims=True)
        wk = jnp.exp(a - a_max)
        vw = (v.astype(F32) * wk).astype(BF16)
        kv = lax.dot_general(k, vw, (((0,), (0,)), ((), ())), preferred_element_type=F32)
        kn = jnp.sum(k.astype(F32) * wk, axis=0, keepdims=True)
        sc_ref[d, slot] = jnp.where(srow == 0, kn, jnp.where(srow == 1, gtot, a_max))
        bb_ref[d, slot] = b
        yield
        dm = jnp.where(tri, dm, -jnp.inf)
        m_loc = jnp.max(dm, axis=-1, keepdims=True)
        s = qk * jnp.exp(dm - m_loc)
        num = _bdot(s, v)
        den = dot(jnp.concatenate(_split_bf16(s), axis=1), ones)
        mb_ref[d, slot] = jnp.broadcast_to(m_loc, (L, LANES))
        kv_ref[d, slot] = kv
        yield
        num_ref[d, slot] = num
        den_ref[d, slot] = den

    def advance(d, todo):
        ct, n, m = ct_ref[d], n_ref[d], m_ref[d]
        pending = []
        for j, slot in todo:
            c0 = pl.multiple_of(j * L, L)
            q = q_ref[0, pl.ds(c0, L), :]
            qc = _bdot(q, ct)
            qn = dot(jnp.concatenate(_split_bf16(q.astype(F32) * n), axis=1), ones2)
            sc = sc_ref[d, slot]
            kn, gtot, a_max = sc[0:1, :], sc[1:2, :], sc[2:3, :]
            m_new = jnp.maximum(gtot + m, a_max)
            decay = jnp.exp(gtot + m - m_new)
            beta = jnp.exp(a_max - m_new)
            pending.append((c0, slot, qc, qn, m))
            ct = decay * ct + beta * kv_ref[d, slot]
            n = decay * n + beta * kn
            m = m_new
        ct_ref[d], n_ref[d], m_ref[d] = ct, n, m
        yield
        for c0, slot, qc, qn, m_in in pending:
            inter = bb_ref[d, slot] + m_in
            m_loc = mb_ref[d, slot]
            m_t = jnp.maximum(inter, m_loc)
            c_intra = jnp.exp(m_loc - m_t)
            c_inter = jnp.exp(inter - m_t)
            num = c_intra * num_ref[d, slot] + c_inter * qc
            den = c_intra * den_ref[d, slot] + c_inter * qn
            h_ref[pl.ds(c0, L), :] += num / jnp.maximum(jnp.abs(den), jnp.exp(-m_t))

    ct_ref[...] = jnp.zeros_like(ct_ref)
    n_ref[...] = jnp.zeros_like(n_ref)
    m_ref[...] = jnp.full_like(m_ref, -jnp.inf)
    h_ref[...] = jnp.zeros_like(h_ref)
    lock = 4 if grp % 4 == 0 else 1

    def group(gi, carry):
        def phase_a(i, c):
            gens = []
            for u in range(lock):
                jf = gi * grp + i * lock + u
                gens += [prepare(jf, 0, i * lock + u), prepare(nc - 1 - jf, 1, i * lock + u)]
            _round_robin(gens)
            return c

        def phase_b(i, c):
            jf = [gi * grp + i * lock + u for u in range(lock)]
            _round_robin([advance(0, [(j, i * lock + u) for u, j in enumerate(jf)]),
                          advance(1, [(nc - 1 - j, i * lock + u) for u, j in enumerate(jf)])])
            return c

        lax.fori_loop(0, grp // lock, phase_a, 0)
        lax.fori_loop(0, grp // lock, phase_b, 0)
        return carry

    lax.fori_loop(0, nc // grp, group, 0)

    tile = 256
    inv_n = 1.0 / DH_A

    def lane_sum(x):
        return dot(jnp.concatenate(_split_bf16(x), axis=1), ones2)

    def finish(r0):
        hh = h_ref[pl.ds(r0, tile), :]
        mu = lane_sum(hh) * inv_n
        yield
        hc = hh - mu
        var = lane_sum(hc * hc) * inv_n
        yield
        hn = hc * lax.rsqrt(var + HEAD_NORM_EPS) * ng_ref[...]
        out = (hn + sk_ref[...] * xc_ref[0, pl.ds(r0, tile), :]) * _silu(za_ref[pl.ds(r0, tile), :])
        o_ref[0, pl.ds(r0, tile), :] = out

    unroll = 4 if (seq // tile) % 4 == 0 else 1

    def finishes(i, carry):
        _round_robin([finish(pl.multiple_of((i * unroll + u) * tile, tile)) for u in range(unroll)])
        return carry

    lax.fori_loop(0, seq // tile // unroll, finishes, 0)


def _mlstm(q, k, v, xc, proj_g, gcol, norm_g, skip):
    bsz, seq, _ = q.shape
    grp = min(ML_GROUP, seq // CHUNK_A)
    hs = lambda: pl.BlockSpec((1, seq, DH_A), lambda b, h: (b, 0, h))
    return pl.pallas_call(
        _mlstm_kernel,
        grid=(bsz, H_A),
        in_specs=[
            hs(), hs(), hs(), hs(),
            pl.BlockSpec((seq, DH_A), lambda b, h: (b, G_AZ // DH_A + h)),
            pl.BlockSpec((1, seq, LANES), lambda b, h: (b, 0, 0)),
            pl.BlockSpec((1, DH_A), lambda b, h: (0, h)),
            pl.BlockSpec((1, DH_A), lambda b, h: (0, h)),
        ],
        out_specs=pl.BlockSpec((1, seq, DH_A), lambda b, h: (b, 0, h)),
        out_shape=jax.ShapeDtypeStruct((bsz, seq, D_A), F32),
        scratch_shapes=[pltpu.VMEM((seq, DH_A), F32), pltpu.VMEM((2, DH_A, DH_A), F32),
                        pltpu.VMEM((2, 1, DH_A), F32), pltpu.VMEM((2, 1, LANES), F32),
                        pltpu.VMEM((2, grp, CHUNK_A, DH_A), F32),
                        pltpu.VMEM((2, grp, CHUNK_A, LANES), F32),
                        pltpu.VMEM((2, grp, CHUNK_A, LANES), F32),
                        pltpu.VMEM((2, grp, CHUNK_A, LANES), F32),
                        pltpu.VMEM((2, grp, DH_A, DH_A), F32),
                        pltpu.VMEM((2, grp, SUBLANES, LANES), F32),
                        pltpu.VMEM((2, 3 * LANES, 2 * LANES), BF16)],
        compiler_params=_cparams(("arbitrary", "arbitrary")),
        name="mlstm",
    )(q, k, v, xc, proj_g, gcol, norm_g.reshape(1, D_A), skip.reshape(1, D_A))


FFT_N2 = 128
HY_CT = 128
HY_PASSES = 1
FFT_ROWS_UNROLL = 8
FFT_SLAB_UNROLL = 4


def _split_bf16(x):
    hi = x.astype(BF16)
    lo = (x - hi.astype(F32)).astype(BF16)
    return hi, lo


def _dot3(a_hi, a_lo, x):
    x_hi, x_lo = _split_bf16(x)
    d = lambda a, b: jnp.dot(a, b, preferred_element_type=F32)
    if HY_PASSES == 1:
        return d(a_hi, x_hi)
    return d(a_hi, x_hi) + (d(a_lo, x_hi) + d(a_hi, x_lo))


def _dft_constants(seq):
    n = 2 * seq
    n2 = FFT_N2
    n1 = n // n2
    f1 = np.arange(n1)[:, None]
    s1 = np.arange(n1)[None, :]
    th1 = 2.0 * np.pi * f1 * s1 / n1
    fwd1 = np.concatenate([np.cos(th1), -np.sin(th1)], axis=0)
    inv1 = np.concatenate([np.cos(th1), -np.sin(th1)], axis=1)[:n1 // 2] / n
    a = np.arange(n2)
    th2 = 2.0 * np.pi * a[:, None] * a[None, :] / n2
    c2, s2 = np.cos(th2), np.sin(th2)
    fwd2 = np.block([[c2, s2], [-s2, c2]])
    inv2 = np.block([[c2, -s2], [s2, c2]])
    tht = 2.0 * np.pi * a[:, None] * np.arange(n1)[None, :] / n
    tw = np.zeros((2, n2, LANES), np.float64)
    tw[0, :, :n1] = np.cos(tht)
    tw[1, :, :n1] = -np.sin(tht)

    def hl(m):
        m32 = jnp.asarray(m, F32)
        hi = m32.astype(BF16)
        lo = (m32 - hi.astype(F32)).astype(BF16)
        return jnp.stack([hi, lo])

    nf = n1 // 2 + 1
    nfp = -(-nf // SUBLANES) * SUBLANES
    keep = np.zeros((nfp, 1))
    keep[:nf] = 1.0
    thh = 2.0 * np.pi * np.arange(nfp)[:, None] * s1 / n1
    fwd1h = np.concatenate([np.cos(thh) * keep, -np.sin(thh) * keep], axis=0)
    wgt = 2.0 * keep
    wgt[0] = wgt[nf - 1] = 1.0
    inv1h = np.concatenate([(np.cos(thh) * wgt).T, (-np.sin(thh) * wgt).T], axis=1)[:n1 // 2] / n
    return dict(fwd1=hl(fwd1), inv1=hl(inv1), fwd2=hl(fwd2), inv2=hl(inv2),
                fwd1h=hl(fwd1h), inv1h=hl(inv1h), tw=jnp.asarray(tw, F32), n1=n1)


def _fft_stage_a(load_rows, k_rows, fwd1_ref, a_ref):
    n2 = FFT_N2
    n1 = fwd1_ref.shape[1] // 2
    f_hi = fwd1_ref[0, :, :k_rows]
    f_lo = fwd1_ref[1, :, :k_rows]

    def one(s2):
        m = _dot3(f_hi, f_lo, load_rows(s2))
        yield
        a_ref[pl.ds(s2, n1, stride=2 * n2), :] = m[:n1]
        a_ref[pl.ds(n2 + s2, n1, stride=2 * n2), :] = m[n1:]

    def body(i, carry):
        _round_robin([one(i * FFT_ROWS_UNROLL + u) for u in range(FFT_ROWS_UNROLL)])
        return carry

    lax.fori_loop(0, n2 // FFT_ROWS_UNROLL, body, 0)


def _twiddle_cols(tw_ref, f1):
    lane = lax.broadcasted_iota(jnp.int32, (FFT_N2, LANES), 1)
    sel = lane == f1
    twr = jnp.sum(jnp.where(sel, tw_ref[0], 0.0), axis=-1, keepdims=True)
    twi = jnp.sum(jnp.where(sel, tw_ref[1], 0.0), axis=-1, keepdims=True)
    return twr, twi


def _hyena_filter_kernel(feat_ref, w1_ref, b1_ref, w2_ref, b2_ref, w3_ref, b3_ref, fr_ref,
                         wo_ref, dec_ref, fwd1_ref, fwd2_ref, tw_ref, o_ref, kt_ref, a_ref, hid_ref):
    n = feat_ref.shape[0]
    seq = n // 2
    n2 = FFT_N2
    n1 = n // n2
    tile = 512
    freq = fr_ref[...]

    @pl.when(pl.program_id(0) == 0)
    def _():
        def hidden(i, carry):
            r0 = pl.multiple_of(i * tile, tile)
            hid = jnp.sin(freq * (_hdot(feat_ref[pl.ds(r0, tile), :], w1_ref[...]) + b1_ref[...]))
            hid = jnp.sin(freq * (_hdot(hid, w2_ref[...]) + b2_ref[...]))
            hid_ref[pl.ds(r0, tile), :] = jnp.sin(freq * (_hdot(hid, w3_ref[...]) + b3_ref[...]))
            return carry

        lax.fori_loop(0, n // tile, hidden, 0)

    def gen(i, carry):
        r0 = pl.multiple_of(i * tile, tile)
        second = r0 >= seq
        wo = jnp.where(second, wo_ref[1, 0], wo_ref[0, 0])
        dec = jnp.where(second, dec_ref[1, 0], dec_ref[0, 0])
        t = feat_ref[pl.ds(r0, tile), 0:1]
        filt = _hdot(hid_ref[pl.ds(r0, tile), :], wo) * jnp.exp(-t * jnp.abs(dec))
        rows = r0 + lax.broadcasted_iota(jnp.int32, (tile, 1), 0)
        kt_ref[pl.ds(r0, tile), :] = jnp.where(rows == seq, 0.0, filt)
        return carry

    lax.fori_loop(0, n // tile, gen, 0)

    _fft_stage_a(lambda s2: kt_ref[pl.ds(s2, n1, stride=n2), :], n1, fwd1_ref, a_ref)

    def slab(f1):
        r0 = pl.multiple_of(f1 * 2 * n2, 2 * n2)
        ar = a_ref[pl.ds(r0, n2), :]
        ai = a_ref[pl.ds(r0 + n2, n2), :]
        twr, twi = _twiddle_cols(tw_ref, f1)
        x = jnp.concatenate([ar * twr - ai * twi, ar * twi + ai * twr], axis=0)
        xf = _dot3(fwd2_ref[0], fwd2_ref[1], x)
        yield
        o_ref[0, pl.ds(r0, 2 * n2), :] = xf

    def slabs(i, carry):
        _round_robin([slab(i * FFT_SLAB_UNROLL + u) for u in range(FFT_SLAB_UNROLL)])
        return carry

    lax.fori_loop(0, n1 // FFT_SLAB_UNROLL, slabs, 0)


def _hyena_filter_spectrum(consts, feats_full, w1p, b1, w2, b2, w3, b3, freq, w_out, decay):
    n = feats_full.shape[0]
    nct = D_B // HY_CT
    full = lambda a: pl.BlockSpec(a.shape, lambda j: (0,) * a.ndim)
    wo = w_out.reshape(HY_HID, 2, nct, HY_CT).transpose(1, 2, 0, 3)
    dec = decay.reshape(2, nct, 1, HY_CT)
    small = (w1p, b1.reshape(1, -1), w2, b2.reshape(1, -1), w3, b3.reshape(1, -1), freq.reshape(1, -1))
    return pl.pallas_call(
        _hyena_filter_kernel,
        grid=(nct,),
        in_specs=[full(feats_full)] + [full(a) for a in small] + [
            pl.BlockSpec((2, 1, HY_HID, HY_CT), lambda j: (0, j, 0, 0)),
            pl.BlockSpec((2, 1, 1, HY_CT), lambda j: (0, j, 0, 0)),
            full(consts['fwd1']), full(consts['fwd2']), full(consts['tw']),
        ],
        out_specs=pl.BlockSpec((1, 2 * n, HY_CT), lambda j: (j, 0, 0)),
        out_shape=jax.ShapeDtypeStruct((nct, 2 * n, HY_CT), F32),
        scratch_shapes=[pltpu.VMEM((n, HY_CT), F32), pltpu.VMEM((2 * n, HY_CT), F32),
                        pltpu.VMEM((n, HY_HID), F32)],
        compiler_params=_cparams(("arbitrary",)),
        name="hyena_filter",
    )(feats_full, *small, wo, dec, consts['fwd1'], consts['fwd2'], consts['tw'])


def _hyena_conv_kernel(z_ref, x0_ref, zb_ref, ks_ref, bias_ref, fwd1_ref, inv1_ref, fwd2_ref,
                       inv2_ref, tw_ref, o_ref, a_ref, y_ref):
    seq = z_ref.shape[1]
    n2 = FFT_N2
    n1 = 2 * seq // n2
    nh = n1 // 2
    nf = nh + 1
    nfp = fwd1_ref.shape[1] // 2

    _fft_stage_a(lambda s2: z_ref[0, pl.ds(s2, nh, stride=n2), :], nh, fwd1_ref, a_ref)

    def slab(f1):
        r0 = f1 * 2 * n2 if isinstance(f1, int) else pl.multiple_of(f1 * 2 * n2, 2 * n2)
        ar = a_ref[pl.ds(r0, n2), :]
        ai = a_ref[pl.ds(r0 + n2, n2), :]
        twr, twi = _twiddle_cols(tw_ref, f1)
        x = jnp.concatenate([ar * twr - ai * twi, ar * twi + ai * twr], axis=0)
        xf = _dot3(fwd2_ref[0], fwd2_ref[1], x)
        yield
        xr, xi = xf[:n2], xf[n2:]
        kr = ks_ref[0, pl.ds(r0, n2), :]
        ki = ks_ref[0, pl.ds(r0 + n2, n2), :]
        y = jnp.concatenate([xr * kr - xi * ki, xr * ki + xi * kr], axis=0)
        bb = _dot3(inv2_ref[0], inv2_ref[1], y)
        yield
        br, bi = bb[:n2], bb[n2:]
        a_ref[pl.ds(r0, n2), :] = br * twr + bi * twi
        a_ref[pl.ds(r0 + n2, n2), :] = bi * twr - br * twi

    def slabs(i, carry):
        _round_robin([slab(i * FFT_SLAB_UNROLL + u) for u in range(FFT_SLAB_UNROLL)])
        return carry

    lax.fori_loop(0, nf // FFT_SLAB_UNROLL, slabs, 0)
    _round_robin([slab(f1) for f1 in range(nf - nf % FFT_SLAB_UNROLL, nf)])

    i_hi = inv1_ref[0]
    i_lo = inv1_ref[1]

    def stage_d(s2):
        br = a_ref[pl.ds(s2, nfp, stride=2 * n2), :]
        bi = a_ref[pl.ds(n2 + s2, nfp, stride=2 * n2), :]
        y = _dot3(i_hi, i_lo, jnp.concatenate([br, bi], axis=0))
        yield
        y_ref[pl.ds(s2, nh, stride=n2), :] = y

    def stage_ds(i, carry):
        _round_robin([stage_d(i * FFT_ROWS_UNROLL + u) for u in range(FFT_ROWS_UNROLL)])
        return carry

    lax.fori_loop(0, n2 // FFT_ROWS_UNROLL, stage_ds, 0)

    tile = 256
    def finish(i, carry):
        r0 = pl.multiple_of(i * tile, tile)
        z = z_ref[0, pl.ds(r0, tile), :]
        y = y_ref[pl.ds(r0, tile), :] + bias_ref[...] * z
        o_ref[0, pl.ds(r0, tile), :] = x0_ref[0, pl.ds(r0, tile), :] * y * _silu(zb_ref[pl.ds(r0, tile), :])
        return carry

    lax.fori_loop(0, seq // tile, finish, 0)


def _hyena_conv(consts, z, x0, proj_g, kspec, bias):
    bsz, seq, _ = z.shape
    nct = D_B // HY_CT
    full = lambda a: pl.BlockSpec(a.shape, lambda j, b: (0,) * a.ndim)
    cs = pl.BlockSpec((1, seq, HY_CT), lambda j, b: (b, 0, j))
    mats = (consts['fwd1h'], consts['inv1h'], consts['fwd2'], consts['inv2'], consts['tw'])
    return pl.pallas_call(
        _hyena_conv_kernel,
        grid=(nct, bsz),
        in_specs=[
            cs, cs,
            pl.BlockSpec((seq, HY_CT), lambda j, b: (b, G_BZ // HY_CT + j)),
            pl.BlockSpec((1, 4 * seq, HY_CT), lambda j, b: (j, 0, 0)),
            pl.BlockSpec((1, HY_CT), lambda j, b: (0, j)),
        ] + [full(m) for m in mats],
        out_specs=cs,
        out_shape=jax.ShapeDtypeStruct((bsz, seq, D_B), F32),
        scratch_shapes=[pltpu.VMEM((4 * seq, HY_CT), F32), pltpu.VMEM((seq, HY_CT), F32)],
        compiler_params=_cparams(("arbitrary", "arbitrary")),
        name="hyena_conv",
    )(z, x0, proj_g, kspec, bias.reshape(1, D_B), *mats)


def _rwkv_kernel(r_ref, k_ref, v_ref, ul_ref, cz_ref, w0_ref, w2_ref, a0_ref, a2_ref,
                 kks_ref, ka_ref, rk_ref, lg_ref, lb_ref, o_ref, y_ref, kb_ref, s_ref,
                 wr_ref, u_ref, arb_ref, y0_ref, bt_ref, kv_ref, egc_ref,
                 kk_ref, lw_ref, be_ref, kd_ref):
    seq = r_ref.shape[1]
    L = CHUNK_C
    nc = seq // L
    P2 = 2 * L
    lane = lax.broadcasted_iota(jnp.int32, (L, LANES), 1)
    head0 = lane < HEAD_C
    prow = lax.broadcasted_iota(jnp.int32, (P2, P2), 0)
    pcol = lax.broadcasted_iota(jnp.int32, (P2, P2), 1)
    bd = (prow // HEAD_C) == (pcol // HEAD_C)
    blk = jnp.where(bd, 1.0, 0.0).astype(F32)
    trow = lax.broadcasted_iota(jnp.int32, (L, P2), 0)
    scol = lax.broadcasted_iota(jnp.int32, (L, P2), 1) % L
    eye = jnp.where(trow == scol, 1.0, 0.0).astype(F32)
    crow = lax.broadcasted_iota(jnp.int32, (L, L), 0)
    ccol = lax.broadcasted_iota(jnp.int32, (L, L), 1)

    def stack(x):
        return jnp.concatenate([jnp.where(head0, x, 0.0), jnp.where(head0, 0.0, x)], axis=0)

    blk_b = blk.astype(BF16)
    grp = wr_ref.shape[1]

    def dot(a, b):
        return jnp.dot(a, b, preferred_element_type=F32)

    def each(f, *cols):
        return [f(*xs) for xs in zip(*cols)]

    def prepare(probs):
        dirs = [d for _, d, _ in probs]
        fwd = [d == 0 for d in dirs]
        strict = [(scol < trow) if f else (scol > trow) for f in fwd]
        incl = [(scol <= trow) if f else (scol >= trow) for f in fwd]
        cum = [jnp.where((ccol <= crow) if f else (ccol >= crow), 1.0, 0.0).astype(BF16) for f in fwd]
        c0 = [pl.multiple_of(j * L, L) for j, _, _ in probs]
        r = [r_ref[0, pl.ds(c, L), :] for c in c0]
        v = [v_ref[0, pl.ds(c, L), :] for c in c0]
        kk = [kk_ref[pl.ds(c, L), :] for c in c0]
        lwt = [lw_ref[d, pl.ds(c, L), :] for c, d in zip(c0, dirs)]
        be = [be_ref[d, pl.ds(c, L), :] for c, d in zip(c0, dirs)]
        kd = [kd_ref[d, pl.ds(c, L), :] for c, d in zip(c0, dirs)]

        def cumsum(cm, x):
            l1 = x.astype(BF16)
            e1 = x - l1.astype(F32)
            l2 = e1.astype(BF16)
            l3 = (e1 - l2.astype(F32)).astype(BF16)
            return dot(jnp.concatenate([cm, cm, cm], axis=1), jnp.concatenate([l1, l2, l3], axis=0))

        g = each(cumsum, cum, lwt)
        yield
        eng = [jnp.exp(-x) for x in g]
        egc = [jnp.exp(x[L - 1:L, :] if f else x[0:1, :]) for x, f in zip(g, fwd)]
        at = each(lambda x, y, z: (-x * jnp.exp(y - z)).astype(BF16), kk, g, lwt)
        rt_ = each(lambda x, y: (x * jnp.exp(y)).astype(BF16), r, g)
        bh = each(lambda x, y: x * y, be, eng)
        kh = each(lambda x, y: x * y, kd, eng)
        lhs = each(lambda x, y: jnp.concatenate([x, y], axis=0), at, rt_)
        rhs = each(lambda x, y: jnp.concatenate([stack(x), stack(y)], axis=0).astype(BF16), bh, kh)
        g1 = each(lambda x, y: lax.dot_general(x, y, (((1,), (1,)), ((), ())),
                                               preferred_element_type=F32), lhs, rhs)
        yield
        aab = each(lambda m, x: jnp.where(m, x[:L, :P2], 0.0), strict, g1)
        aak_ark = each(lambda ms, mi, x: jnp.concatenate(
            [jnp.where(ms, x[:L, P2:], 0.0), jnp.where(mi, x[L:, P2:], 0.0)], axis=0).astype(BF16),
            strict, incl, g1)
        for (_, d, slot), mi, x in zip(probs, incl, g1):
            arb_ref[d, slot] = jnp.where(mi, x[L:, :P2], 0.0).astype(BF16)
        t = [eye + x for x in aab]
        p = [x.astype(BF16) for x in aab]
        vs = [stack(x).astype(BF16) for x in v]
        av = each(dot, aak_ark, vs)
        p = [dot(x, stack(x)) for x in p]
        yield
        p = [x.astype(BF16) for x in p]
        sq = 2
        while sq < L:
            if 2 * sq >= L:
                pt = each(lambda x, y: dot(x, stack(y.astype(BF16))), p, t)
                yield
                t = each(lambda y, x: y + x, t, pt)
            else:
                pt = each(lambda x, y: dot(x, jnp.concatenate([stack(x), stack(y.astype(BF16))], axis=1)),
                          p, t)
                yield
                p = [x[:, :P2].astype(BF16) for x in pt]
                t = each(lambda y, x: y + x[:, P2:], t, pt)
            sq *= 2
        wu = each(lambda x, y, z: dot(x.astype(BF16), jnp.concatenate(
            [stack(y), stack(z[:L].astype(BF16))], axis=1)), t, at, av)
        kv = each(lambda x, y, z: lax.dot_general(
            x.astype(BF16), (y * z).astype(BF16), (((0,), (0,)), ((), ())), preferred_element_type=F32),
            v, kh, egc)
        yield
        for i, (_, d, slot) in enumerate(probs):
            wr_ref[d, slot] = jnp.concatenate([wu[i][:, :P2].astype(BF16), rt_[i]], axis=0)
            u_ref[d, slot] = wu[i][:, P2:].astype(BF16)
            y0_ref[d, slot] = av[i][L:]
            bt_ref[d, slot] = (bh[i] * egc[i]).astype(BF16)
            kv_ref[d, slot] = jnp.where(bd, kv[i], 0.0).astype(BF16)
            egc_ref[d, slot] = egc[i]

    def advance(steps):
        s = [s_ref[d] for d in range(2)]
        for probs in steps:
            x = [lax.dot_general(wr_ref[d, slot], s[d].astype(BF16), (((1,), (1,)), ((), ())),
                                 preferred_element_type=F32) for _, d, slot in probs]
            yield
            sa = [(y[:L] + u_ref[d, slot].astype(F32)).astype(BF16) for (_, d, slot), y in zip(probs, x)]
            upd = [lax.dot_general(y, bt_ref[d, slot], (((0,), (0,)), ((), ())),
                                   preferred_element_type=F32) for (_, d, slot), y in zip(probs, sa)]
            z = [dot(arb_ref[d, slot], stack(y)) for (_, d, slot), y in zip(probs, sa)]
            yield
            for i, (j, d, slot) in enumerate(probs):
                s[d] = (s[d] * egc_ref[d, slot] + kv_ref[d, slot].astype(F32)
                        + jnp.where(bd, upd[i], 0.0))
                c0 = pl.multiple_of(j * L, L)
                y_ref[pl.ds(c0, L), :] += x[i][L:] + z[i] + y0_ref[d, slot]
        for d in range(2):
            s_ref[d] = s[d]

    s_ref[...] = jnp.zeros_like(s_ref)
    y_ref[...] = jnp.zeros_like(y_ref)
    tile = 256

    def pointwise(r0):
        k = k_ref[0, pl.ds(r0, tile), :]
        lwla = ul_ref[0, pl.ds(r0, tile), :]
        th = jnp.tanh(lwla[:, :LANES]).astype(BF16)
        la = lwla[:, LANES:].astype(BF16)
        xw = [w0_ref[d:d + 1, :] + dot(th, w2_ref[d].astype(BF16)) for d in range(2)]
        aa = [a0_ref[d:d + 1, :] + dot(la, a2_ref[d].astype(BF16)) for d in range(2)]
        kkv = k * kks_ref[...]
        ss = dot((kkv * kkv).astype(BF16), blk_b)
        yield
        kk = kkv / jnp.maximum(jnp.sqrt(ss), 1e-12)
        kk_ref[pl.ds(r0, tile), :] = kk
        kb = None
        for d in range(2):
            lw_ref[d, pl.ds(r0, tile), :] = -math.exp(-0.5) * _sigmoid(xw[d])
            a = _sigmoid(aa[d])
            kd = k * (1.0 + (a - 1.0) * ka_ref[...])
            be_ref[d, pl.ds(r0, tile), :] = kk * a
            kd_ref[d, pl.ds(r0, tile), :] = kd
            kb = kd if kb is None else kb + kd
        kb_ref[pl.ds(r0, tile), :] = kb

    pw_unroll = 2 if (seq // tile) % 2 == 0 else 1

    def pointwise_loop(i, carry):
        _round_robin([pointwise(pl.multiple_of((i * pw_unroll + u) * tile, tile))
                      for u in range(pw_unroll)])
        return carry

    lax.fori_loop(0, seq // tile // pw_unroll, pointwise_loop, 0)

    lock = grp // 2
    nsets = nc // lock

    def chunk_set(i):
        base = (i % 2) * lock
        return [[(i * lock + q, 0, base + q), (nc - 1 - (i * lock + q), 1, base + q)]
                for q in range(lock)]

    _round_robin([prepare(sum(chunk_set(0), []))])

    def pipelined(i, carry):
        _round_robin([prepare(sum(chunk_set(i + 1), [])), advance(chunk_set(i))])
        return carry

    lax.fori_loop(0, nsets - 1, pipelined, 0)
    _round_robin([advance(chunk_set(nsets - 1))])

    tile = 256
    inv_n = 1.0 / HEAD_C
    blk2 = jnp.concatenate([blk_b, blk_b], axis=0)

    def head_sum(x):
        hi, lo = _split_bf16(x)
        return dot(jnp.concatenate([hi, lo], axis=1), blk2)

    def finish(r0):
        y = y_ref[pl.ds(r0, tile), :]
        mu = head_sum(y) * inv_n
        r = r_ref[0, pl.ds(r0, tile), :]
        v = v_ref[0, pl.ds(r0, tile), :]
        kbon = 0.5 * kb_ref[pl.ds(r0, tile), :]
        bonus = head_sum(r * kbon * rk_ref[...]) * v
        yield
        yc = y - mu
        var = head_sum(yc * yc) * inv_n
        yield
        yn = yc * lax.rsqrt(var + LNX_EPS) * lg_ref[...] + lb_ref[...]
        o_ref[0, pl.ds(r0, tile), :] = (yn + bonus) * _silu(cz_ref[pl.ds(r0, tile), :])

    unroll = 4 if (seq // tile) % 4 == 0 else 1

    def finishes(i, carry):
        _round_robin([finish(pl.multiple_of((i * unroll + u) * tile, tile)) for u in range(unroll)])
        return carry

    lax.fori_loop(0, seq // tile // unroll, finishes, 0)


def _rwkv(u, ul, proj_g, w0, w2pad, a0, a2pad, kk_s, ka, rk, lnx_g, lnx_b):
    bsz, seq, _ = u.shape
    npair = D_C // LANES
    grp = min(RW_GROUP, seq // CHUNK_C)
    col = lambda off: pl.BlockSpec((1, seq, LANES), lambda b, p: (b, 0, off + p))
    vec = pl.BlockSpec((1, LANES), lambda b, p: (0, p))
    two = pl.BlockSpec((2, LANES), lambda b, p: (0, p))
    lora = pl.BlockSpec((2, LANES, LANES), lambda b, p: (0, 0, p))
    row = lambda a: a.reshape(1, D_C)
    return pl.pallas_call(
        _rwkv_kernel,
        grid=(bsz, npair),
        in_specs=[
            col(0), col(npair), col(2 * npair),
            pl.BlockSpec((1, seq, 2 * LANES), lambda b, p: (b, 0, 0)),
            pl.BlockSpec((seq, LANES), lambda b, p: (b, G_CZ // LANES + p)),
            two, lora, two, lora, vec, vec, vec, vec, vec,
        ],
        out_specs=pl.BlockSpec((1, seq, LANES), lambda b, p: (b, 0, p)),
        out_shape=jax.ShapeDtypeStruct((bsz, seq, D_C), F32),
        scratch_shapes=[pltpu.VMEM((seq, LANES), F32), pltpu.VMEM((seq, LANES), F32),
                        pltpu.VMEM((2, LANES, LANES), F32),
                        pltpu.VMEM((2, grp, 2 * CHUNK_C, LANES), BF16),
                        pltpu.VMEM((2, grp, CHUNK_C, LANES), BF16),
                        pltpu.VMEM((2, grp, CHUNK_C, LANES), BF16),
                        pltpu.VMEM((2, grp, CHUNK_C, LANES), F32),
                        pltpu.VMEM((2, grp, CHUNK_C, LANES), BF16),
                        pltpu.VMEM((2, grp, LANES, LANES), BF16),
                        pltpu.VMEM((2, grp, 1, LANES), F32),
                        pltpu.VMEM((seq, LANES), F32), pltpu.VMEM((2, seq, LANES), F32),
                        pltpu.VMEM((2, seq, LANES), F32), pltpu.VMEM((2, seq, LANES), F32)],
        compiler_params=_cparams(("arbitrary", "arbitrary")),
        name="rwkv7",
    )(u, u, u, ul, proj_g, w0, w2pad, a0, a2pad, row(kk_s), row(ka), row(rk), row(lnx_g), row(lnx_b))


def _out_kernel(ya_ref, yb_ref, yc_ref, ga_ref, gb_ref, gc_ref, x_ref, gate_ref, pg_ref,
                wa_ref, wb_ref, wc_ref, wo_ref, o_ref):
    merged = (_sigmoid(ga_ref[...]) * _bdot(ya_ref[...], wa_ref[...])
              + _sigmoid(gb_ref[...]) * _bdot(yb_ref[...], wb_ref[...])
              + _sigmoid(gc_ref[...]) * _bdot(yc_ref[...], wc_ref[...]))
    out = _bdot(merged, wo_ref[...])
    y = out * lax.rsqrt(jnp.mean(out * out, axis=-1, keepdims=True) + NORM_EPS) * pg_ref[...]
    o_ref[...] = x_ref[...] + gate_ref[0] * y


def _merge_out(ya, yb, yc, proj_g, x2d, seq, gate, post_g, wa, wb, wc, wo, tm):
    rows, d = x2d.shape
    per_b = seq // tm
    full = lambda a: pl.BlockSpec(a.shape, lambda i: (0,) * a.ndim)
    gcol = lambda off: pl.BlockSpec((tm, d), lambda i: (i, off // d))
    return pl.pallas_call(
        _out_kernel,
        grid=(rows // tm,),
        in_specs=[
            pl.BlockSpec((tm, D_A), lambda i: (i, 0)),
            pl.BlockSpec((tm, D_B), lambda i: (i, 0)),
            pl.BlockSpec((tm, D_C), lambda i: (i, 0)),
            gcol(G_GA), gcol(G_GB), gcol(G_GC),
            pl.BlockSpec((tm, d), lambda i: (i, 0)),
            pl.BlockSpec((1, 1, d), lambda i: (i // per_b, 0, 0)),
            pl.BlockSpec((1, d), lambda i: (0, 0)),
            full(wa), full(wb), full(wc), full(wo),
        ],
        out_specs=pl.BlockSpec((tm, d), lambda i: (i, 0)),
        out_shape=jax.ShapeDtypeStruct((rows, d), F32),
        compiler_params=_cparams(("arbitrary",)),
        name="merge_out",
    )(ya.reshape(rows, D_A), yb.reshape(rows, D_B), yc.reshape(rows, D_C), proj_g, proj_g, proj_g,
      x2d, gate, post_g.reshape(1, d), wa, wb, wc, wo)


def _hyena_feats(seq):
    n = 2 * seq
    t = jnp.linspace(0.0, 1.0, seq, dtype=F32)
    w = 2.0 * math.pi * jnp.arange(seq, dtype=F32) / seq
    f = jnp.linspace(1e-4, HY_BANDS - 1, HY_BANDS, dtype=F32)
    zz = w[:, None] * f[None, :]
    feats = jnp.concatenate([t[:, None], jnp.cos(zz), -jnp.sin(zz)], axis=-1)
    pos = np.concatenate([np.arange(seq), [0], np.arange(seq - 1, 0, -1)])
    full = jnp.zeros((n, LANES), F32).at[:, :feats.shape[1]].set(feats[pos])
    return full


def _tiles(seq):
    return min(seq, 1024), min(seq, 256), min(seq, 512)


def kernel(x, c, ada_w, ada_b, pre_g, post_g, w_in, ml_conv_w, ml_conv_b, ml_wq, ml_wk, ml_wv, ml_gate_w, ml_gate_b, ml_norm_g, ml_skip, hy_conv_w, hy_conv_b, hy_w1, hy_b1, hy_w2, hy_b2, hy_w3, hy_b3, hy_freq, hy_w_out, hy_decay, hy_bias, rw_mu, rw_w0, rw_w2, rw_a0, rw_a2, rw_kk, rw_ka, rw_rk, rw_lnx_g, rw_lnx_b, w_branch_a, w_branch_b, w_branch_c, w_out):
    bsz, seq, d = x.shape
    depth = ada_w.shape[0]
    tm_in, ts_prep, tm_out = _tiles(seq)
    perm_h, perm_g = _ref_column_perm()
    consts = _dft_constants(seq)
    feats_full = _hyena_feats(seq)
    mod = _modulation(c, ada_w, ada_b)
    x2d = x.reshape(bsz * seq, d)
    n_gate = 4 * H_A
    for l in range(depth):
        shift = mod[l, :, None, 0:d]
        scale = mod[l, :, None, d:2 * d]
        gate = mod[l, :, None, 2 * d:3 * d]
        w_h = w_in[l][:, perm_h].astype(BF16)
        w_g = w_in[l][:, perm_g].astype(BF16)
        proj_h = _inproj(x2d, seq, pre_g[l], shift, scale, w_h, tm_in, N_H // 3)
        proj_g = _inproj(x2d, seq, pre_g[l], shift, scale, w_g, tm_in, N_G // 4)

        gw = jnp.zeros((3 * D_A, LANES), F32).at[:, :n_gate].set(ml_gate_w[l])
        gb = jnp.zeros((1, LANES), F32).at[0, :n_gate].set(ml_gate_b[l])
        q, k, v, xc, gcol, z, x0, u, ul = _prep(
            proj_h, bsz, seq, ts_prep, ml_conv_w[l], ml_conv_b[l].reshape(1, D_A),
            ml_wq[l], ml_wk[l], ml_wv[l], gw, gb,
            hy_conv_w[l], hy_conv_b[l].reshape(1, 3 * D_B), rw_mu[l].reshape(1, -1))

        y_a = _mlstm(q, k, v, xc, proj_g, gcol, ml_norm_g[l], ml_skip[l])

        w1p = jnp.zeros((LANES, HY_HID), F32).at[:hy_w1.shape[1]].set(hy_w1[l])
        kspec = _hyena_filter_spectrum(consts, feats_full, w1p, hy_b1[l], hy_w2[l], hy_b2[l],
                                       hy_w3[l], hy_b3[l], hy_freq[l], hy_w_out[l], hy_decay[l])
        y_b = _hyena_conv(consts, z, x0, proj_g, kspec, hy_bias[l])

        w2pad = (jnp.zeros((2, LANES, D_C), F32).at[0, :LORA].set(rw_w2[l, 0])
                 .at[1, LORA:].set(rw_w2[l, 1]))
        a2pad = (jnp.zeros((2, LANES, D_C), F32).at[0, :LORA].set(rw_a2[l, 0])
                 .at[1, LORA:].set(rw_a2[l, 1]))
        y_c = _rwkv(u, ul, proj_g, rw_w0[l], w2pad, rw_a0[l], a2pad, rw_kk[l], rw_ka[l],
                    rw_rk[l].reshape(-1), rw_lnx_g[l], rw_lnx_b[l])

        x2d = _merge_out(y_a, y_b, y_c, proj_g, x2d, seq, gate, post_g[l],
                         w_branch_a[l].astype(BF16), w_branch_b[l].astype(BF16),
                         w_branch_c[l].astype(BF16), w_out[l].astype(BF16), tm_out)
    return x2d.reshape(bsz, seq, d)
```

```python
import functools
import math

import numpy as np
import jax
import jax.numpy as jnp
from jax import lax
from jax.experimental import pallas as pl
from jax.experimental.pallas import tpu as pltpu

D_MODEL = 1024
DEPTH = 4
D_A = 512
H_A = 4
DH_A = 128
CHUNK_A = 64
ML_GROUP = 16
D_B = 512
HY_BANDS = 16
HY_HID = 64
D_C = 1024
HEAD_C = 64
H_C = D_C // HEAD_C
LORA = 64
CHUNK_C = 64
RW_GROUP = 8
LNX_EPS = 64e-5
NORM_EPS = 1e-6
HEAD_NORM_EPS = 1e-5

LANES = 128
SUBLANES = 8
VMEM_LIMIT = 56 * 1024 * 1024

F32 = jnp.float32
BF16 = jnp.bfloat16
HIGHEST = lax.Precision.HIGHEST

N_H = D_A + 3 * D_B + 3 * D_C + 4 * LORA
N_G = D_A + D_B + D_C + 3 * D_MODEL
H_AX, H_BV, H_BX0, H_BX1, H_CR, H_CK, H_CV, H_LW, H_LA = (
    0, 512, 1024, 1536, 2048, 3072, 4096, 5120, 5248)
G_AZ, G_BZ, G_CZ, G_GA, G_GB, G_GC = 0, 512, 1024, 2048, 3072, 4096


def _ref_column_perm():
    r = lambda a, n: np.arange(a, a + n)
    a_x, a_z = r(0, 512), r(512, 512)
    b_v, b_x0, b_x1, b_z = r(1024, 512), r(1536, 512), r(2048, 512), r(2560, 512)
    c_r, c_k, c_v = r(3072, 1024), r(4096, 1024), r(5120, 1024)
    c_lw, c_la = r(6144, 128), r(6272, 128)
    c_z = r(6400, 1024)
    g_a, g_b, g_c = r(7424, 1024), r(8448, 1024), r(9472, 1024)
    h = np.concatenate([a_x, b_v, b_x0, b_x1, c_r, c_k, c_v, c_lw, c_la])
    g = np.concatenate([a_z, b_z, c_z, g_a, g_b, g_c])
    return h, g


def _contiguous_runs(idx):
    cuts = np.flatnonzero(np.diff(idx) != 1) + 1
    return [(int(seg[0]), int(seg[-1]) + 1) for seg in np.split(idx, cuts)]


def _cparams(sem, **extra):
    return pltpu.CompilerParams(dimension_semantics=sem, vmem_limit_bytes=VMEM_LIMIT, **extra)


def _silu(x):
    return x * (1.0 / (1.0 + jnp.exp(-x)))


def _sigmoid(x):
    return 1.0 / (1.0 + jnp.exp(-x))


def _bdot(a, b):
    return jnp.dot(a.astype(BF16), b.astype(BF16), preferred_element_type=F32)


def _bdot_nt(a, b):
    return lax.dot_general(a.astype(BF16), b.astype(BF16), (((1,), (1,)), ((), ())),
                           preferred_element_type=F32)


def _round_robin(gens):
    gens = list(gens)
    while gens:
        alive = []
        for g in gens:
            try:
                next(g)
                alive.append(g)
            except StopIteration:
                pass
        gens = alive


def _hdot(a, b):
    return jnp.dot(a, b, preferred_element_type=F32, precision=HIGHEST)


def _mod_kernel(c_ref, w_ref, b_ref, o_ref):
    cond = _silu(c_ref[...])
    o_ref[0] = _hdot(cond, w_ref[0]) + b_ref[0]


def _modulation(c, ada_w, ada_b):
    depth, d, n3 = ada_w.shape
    bsz = c.shape[0]
    nt = n3 // d
    return pl.pallas_call(
        _mod_kernel,
        grid=(depth, nt),
        in_specs=[
            pl.BlockSpec((bsz, d), lambda l, j: (0, 0)),
            pl.BlockSpec((1, d, d), lambda l, j: (l, 0, j)),
            pl.BlockSpec((1, 1, d), lambda l, j: (l, 0, j)),
        ],
        out_specs=pl.BlockSpec((1, bsz, d), lambda l, j: (l, 0, j)),
        out_shape=jax.ShapeDtypeStruct((depth, bsz, n3), F32),
        compiler_params=_cparams(("arbitrary", "arbitrary")),
        name="adaln_mod",
    )(c, ada_w, ada_b.reshape(depth, 1, n3))


def _inproj_kernel(x_ref, g_ref, shift_ref, scale_ref, w_ref, o_ref, h_ref):
    @pl.when(pl.program_id(1) == 0)
    def _():
        x = x_ref[...]
        y = x * lax.rsqrt(jnp.mean(x * x, axis=-1, keepdims=True) + NORM_EPS)
        h = y * g_ref[...] * (1.0 + scale_ref[0]) + shift_ref[0]
        h_ref[...] = h.astype(BF16)

    o_ref[...] = jnp.dot(h_ref[...], w_ref[...], preferred_element_type=F32)


def _inproj(x2d, seq, pre_g, shift, scale, w_bf16, tm, tn):
    rows, d = x2d.shape
    n = w_bf16.shape[1]
    per_b = seq // tm
    return pl.pallas_call(
        _inproj_kernel,
        grid=(rows // tm, n // tn),
        in_specs=[
            pl.BlockSpec((tm, d), lambda i, j: (i, 0)),
            pl.BlockSpec((1, d), lambda i, j: (0, 0)),
            pl.BlockSpec((1, 1, d), lambda i, j: (i // per_b, 0, 0)),
            pl.BlockSpec((1, 1, d), lambda i, j: (i // per_b, 0, 0)),
            pl.BlockSpec((d, tn), lambda i, j: (0, j)),
        ],
        out_specs=pl.BlockSpec((tm, tn), lambda i, j: (i, j)),
        out_shape=jax.ShapeDtypeStruct((rows, n), F32),
        scratch_shapes=[pltpu.VMEM((tm, d), BF16)],
        compiler_params=_cparams(("arbitrary", "arbitrary")),
        name="inproj",
    )(x2d, pre_g.reshape(1, d), shift, scale, w_bf16)


def _log_sigmoid(x):
    return jnp.minimum(x, 0.0) - jnp.log(1.0 + jnp.exp(-jnp.abs(x)))


def _prep_kernel(main_ref, prev_ref, next_ref,
                 mcw_ref, mcb_ref, wq_ref, wk_ref, wv_ref, gw_ref, gb_ref,
                 hcw_ref, hcb_ref, mu_ref,
                 q_ref, k_ref, v_ref, xc_ref, gcol_ref, z_ref, x0_ref, u_ref, ul_ref):
    i = pl.program_id(1)
    ts = main_ref.shape[0]
    has_prev = jnp.where(i > 0, 1.0, 0.0).astype(F32)
    has_next = jnp.where(i < pl.num_programs(1) - 1, 1.0, 0.0).astype(F32)

    def neighbours(c0, cw):
        x = main_ref[:, c0:c0 + cw]
        row = lax.broadcasted_iota(jnp.int32, (SUBLANES, cw), 0)
        p_row = prev_ref[SUBLANES - 1:SUBLANES, c0:c0 + cw] * has_prev
        n_row = next_ref[0:1, c0:c0 + cw] * has_next
        xp = pltpu.roll(x, 1, 0)
        xn = pltpu.roll(x, ts - 1, 0)
        xp = jnp.concatenate([jnp.where(row == 0, p_row, xp[:SUBLANES]), xp[SUBLANES:]], axis=0)
        xn = jnp.concatenate([xn[:ts - SUBLANES],
                              jnp.where(row == SUBLANES - 1, n_row, xn[ts - SUBLANES:])], axis=0)
        return xp, x, xn

    def conv(c0, cw, w_ref, b_ref, w0):
        xp, x, xn = neighbours(c0, cw)
        w = w_ref[:, w0:w0 + cw]
        return xp * w[0:1] + x * w[1:2] + xn * w[2:3] + b_ref[:, w0:w0 + cw], x

    conv_a, xa = conv(H_AX, D_A, mcw_ref, mcb_ref, 0)
    xc = _silu(conv_a)
    xc_ref[0] = xc
    gates = jnp.zeros((ts, LANES), F32) + gb_ref[...]
    for h in range(H_A):
        sl = slice(h * DH_A, (h + 1) * DH_A)
        qh = _bdot(xc[:, sl], wq_ref[h])
        kh = _bdot(xc[:, sl], wk_ref[h])
        vh = _bdot(xa[:, sl], wv_ref[h])
        gates += (_bdot(qh, gw_ref[h * DH_A:(h + 1) * DH_A])
                  + _bdot(kh, gw_ref[D_A + h * DH_A:D_A + (h + 1) * DH_A])
                  + _bdot(vh, gw_ref[2 * D_A + h * DH_A:2 * D_A + (h + 1) * DH_A]))
        q_ref[0, :, sl] = qh.astype(BF16)
        k_ref[0, :, sl] = (kh * (DH_A ** -0.5)).astype(BF16)
        v_ref[0, :, sl] = vh.astype(BF16)
    col = lax.broadcasted_iota(jnp.int32, (ts, LANES), 1)
    rmod = lax.broadcasted_iota(jnp.int32, (ts, LANES), 0) % CHUNK_A
    lf = _log_sigmoid(gates)
    cf = lf
    cb = lf
    sh = 1
    while sh < CHUNK_A:
        cf = cf + jnp.where(rmod >= sh, pltpu.roll(cf, sh, 0), 0.0)
        cb = cb + jnp.where(rmod < CHUNK_A - sh, pltpu.roll(cb, ts - sh, 0), 0.0)
        sh *= 2
    is_ff = (col >= H_A) & (col < 2 * H_A)
    is_fb = (col >= 3 * H_A) & (col < 4 * H_A)
    gcol_ref[0] = jnp.where(is_ff, cf, jnp.where(is_fb, cb, gates))

    cv, _ = conv(H_BV, D_B, hcw_ref, hcb_ref, 0)
    cx0, _ = conv(H_BX0, D_B, hcw_ref, hcb_ref, D_B)
    cx1, _ = conv(H_BX1, D_B, hcw_ref, hcb_ref, 2 * D_B)
    z_ref[0] = cv * cx1
    x0_ref[0] = cx0

    cw = 512
    for j in range(3 * D_C // cw):
        xp, x, xn = neighbours(H_CR + j * cw, cw)
        mu = mu_ref[:, j * cw:(j + 1) * cw]
        u_ref[0, :, j * cw:(j + 1) * cw] = x * (1.0 - mu) + (0.5 * mu) * (xp + xn)
    xp, x, xn = neighbours(H_LW, 4 * LORA)
    mu = mu_ref[:, 3 * D_C:3 * D_C + 4 * LORA]
    ul_ref[0] = x * (1.0 - mu) + (0.5 * mu) * (xp + xn)


def _prep(proj_h, bsz, seq, ts, ml_conv_w, ml_conv_b, wq, wk, wv, gate_w_pad, gate_b_pad,
          hy_conv_w, hy_conv_b, rw_mu):
    ns = seq // ts
    hb = ts // SUBLANES
    last_hb = bsz * seq // SUBLANES - 1
    full = lambda a: pl.BlockSpec(a.shape, lambda b, i: (0,) * a.ndim)
    seq_spec = lambda w: pl.BlockSpec((1, ts, w), lambda b, i: (b, i, 0))
    params = (ml_conv_w, ml_conv_b, wq, wk, wv, gate_w_pad, gate_b_pad, hy_conv_w, hy_conv_b, rw_mu)
    outs = [(D_A, BF16), (D_A, BF16), (D_A, BF16), (D_A, F32), (LANES, F32),
            (D_B, F32), (D_B, F32), (3 * D_C, F32), (4 * LORA, F32)]
    return pl.pallas_call(
        _prep_kernel,
        grid=(bsz, ns),
        in_specs=[
            pl.BlockSpec((ts, N_H), lambda b, i: (b * ns + i, 0)),
            pl.BlockSpec((SUBLANES, N_H), lambda b, i: (jnp.maximum((b * ns + i) * hb - 1, 0), 0)),
            pl.BlockSpec((SUBLANES, N_H), lambda b, i: (jnp.minimum((b * ns + i + 1) * hb, last_hb), 0)),
        ] + [full(p) for p in params],
        out_specs=[seq_spec(w) for w, _ in outs],
        out_shape=[jax.ShapeDtypeStruct((bsz, seq, w), dt) for w, dt in outs],
        compiler_params=_cparams(("arbitrary", "arbitrary")),
        name="prep",
    )(proj_h, proj_h, proj_h, *params)


def _mlstm_kernel(q_ref, k_ref, v_ref, xc_ref, za_ref, gcol_ref, ng_ref, sk_ref, o_ref,
                  h_ref, ct_ref, n_ref, m_ref, num_ref, den_ref, bb_ref, mb_ref, kv_ref, sc_ref,
                  sel_ref):
    head = pl.program_id(1)
    seq = q_ref.shape[1]
    L = CHUNK_A
    nc = seq // L
    row = lax.broadcasted_iota(jnp.int32, (L, L), 0)
    colm = lax.broadcasted_iota(jnp.int32, (L, L), 1)
    lane = lax.broadcasted_iota(jnp.int32, (L, LANES), 1)

    grp = num_ref.shape[1]
    srow = lax.broadcasted_iota(jnp.int32, (SUBLANES, LANES), 0)
    ones = jnp.ones((LANES, LANES), BF16)
    ones2 = jnp.ones((2 * LANES, LANES), BF16)

    def dot(a, b):
        return jnp.dot(a, b, preferred_element_type=F32)

    def split3(x):
        x1 = x.astype(BF16)
        e1 = x - x1.astype(F32)
        x2 = e1.astype(BF16)
        return x1, x2, (e1 - x2.astype(F32)).astype(BF16)

    srow_i = lax.broadcasted_iota(jnp.int32, (3 * LANES, 2 * LANES), 0) % LANES
    scol_i = lax.broadcasted_iota(jnp.int32, (3 * LANES, 2 * LANES), 1)
    for d in range(2):
        li_lane = head + 2 * H_A * d
        want = jnp.where(scol_i < LANES, li_lane + H_A, li_lane)
        sel_ref[d] = jnp.where(srow_i == want, 1.0, 0.0).astype(BF16)

    def prepare(j, d, slot):
        fwd = d == 0
        tri = (colm <= row) if fwd else (colm >= row)
        c0 = pl.multiple_of(j * L, L)
        q = q_ref[0, pl.ds(c0, L), :]
        k = k_ref[0, pl.ds(c0, L), :]
        v = v_ref[0, pl.ds(c0, L), :]
        qk = _bdot_nt(q, k)
        bl = dot(jnp.concatenate(split3(gcol_ref[0, pl.ds(c0, L), :]), axis=1), sel_ref[d])
        yield
        b, li = bl[:, :LANES], bl[:, LANES:]
        x0, x1, x2 = split3(b)
        y0, y1, y2 = split3(li - b)
        xl = jnp.where(lane == 0, x0, jnp.where(lane == 1, x1, jnp.where(
            lane == 2, x2, jnp.where(lane < 6, 1.0, 0.0).astype(BF16))))
        yl = jnp.where(lane == 3, y0, jnp.where(lane == 4, y1, jnp.where(
            lane == 5, y2, jnp.where(lane < 3, 1.0, 0.0).astype(BF16))))
        dm = _bdot_nt(xl, yl)
        gtot = b[L - 1:L, :] if fwd else b[0:1, :]
        a = gtot - b + li
        a_max = jnp.max(a, axis=0, keepdims=True)
        wk = jnp.exp(a - a_max)
        vw = (v.astype(F32) * wk).astype(BF16)
        kv = lax.dot_general(k, vw, (((0,), (0,)), ((), ())), preferred_element_type=F32)
        kn = jnp.sum(k.astype(F32) * wk, axis=0, keepdims=True)
        sc_ref[d, slot] = jnp.where(srow == 0, kn, jnp.where(srow == 1, gtot, a_max))
        bb_ref[d, slot] = b
        yield
        dm = jnp.where(tri, dm, -jnp.inf)
        m_loc = jnp.max(dm, axis=-1, keepdims=True)
        s = qk * jnp.exp(dm - m_loc)
        num = _bdot(s, v)
        den = dot(jnp.concatenate(_split_bf16(s), axis=1), ones)
        mb_ref[d, slot] = jnp.broadcast_to(m_loc, (L, LANES))
        kv_ref[d, slot] = kv
        yield
        num_ref[d, slot] = num
        den_ref[d, slot] = den

    def advance(d, todo):
        ct, n, m = ct_ref[d], n_ref[d], m_ref[d]
        pending = []
        for j, slot in todo:
            c0 = pl.multiple_of(j * L, L)
            q = q_ref[0, pl.ds(c0, L), :]
            qc = _bdot(q, ct)
            qn = dot(jnp.concatenate(_split_bf16(q.astype(F32) * n), axis=1), ones2)
            sc = sc_ref[d, slot]
            kn, gtot, a_max = sc[0:1, :], sc[1:2, :], sc[2:3, :]
            m_new = jnp.maximum(gtot + m, a_max)
            decay = jnp.exp(gtot + m - m_new)
            beta = jnp.exp(a_max - m_new)
            pending.append((c0, slot, qc, qn, m))
            ct = decay * ct + beta * kv_ref[d, slot]
            n = decay * n + beta * kn
            m = m_new
        ct_ref[d], n_ref[d], m_ref[d] = ct, n, m
        yield
        for c0, slot, qc, qn, m_in in pending:
            inter = bb_ref[d, slot] + m_in
            m_loc = mb_ref[d, slot]
            m_t = jnp.maximum(inter, m_loc)
            c_intra = jnp.exp(m_loc - m_t)
            c_inter = jnp.exp(inter - m_t)
            num = c_intra * num_ref[d, slot] + c_inter * qc
            den = c_intra * den_ref[d, slot] + c_inter * qn
            h_ref[pl.ds(c0, L), :] += num / jnp.maximum(jnp.abs(den), jnp.exp(-m_t))

    ct_ref[...] = jnp.zeros_like(ct_ref)
    n_ref[...] = jnp.zeros_like(n_ref)
    m_ref[...] = jnp.full_like(m_ref, -jnp.inf)
    h_ref[...] = jnp.zeros_like(h_ref)
    lock = 8 if grp % 8 == 0 else 1

    def group(gi, carry):
        def phase_a(i, c):
            gens = []
            for u in range(lock):
                jf = gi * grp + i * lock + u
                gens += [prepare(jf, 0, i * lock + u), prepare(nc - 1 - jf, 1, i * lock + u)]
            _round_robin(gens)
            return c

        def phase_b(i, c):
            jf = [gi * grp + i * lock + u for u in range(lock)]
            _round_robin([advance(0, [(j, i * lock + u) for u, j in enumerate(jf)]),
                          advance(1, [(nc - 1 - j, i * lock + u) for u, j in enumerate(jf)])])
            return c

        lax.fori_loop(0, grp // lock, phase_a, 0)
        lax.fori_loop(0, grp // lock, phase_b, 0)
        return carry

    lax.fori_loop(0, nc // grp, group, 0)

    tile = 256
    inv_n = 1.0 / DH_A

    def lane_sum(x):
        return dot(jnp.concatenate(_split_bf16(x), axis=1), ones2)

    def finish(r0):
        hh = h_ref[pl.ds(r0, tile), :]
        mu = lane_sum(hh) * inv_n
        yield
        hc = hh - mu
        var = lane_sum(hc * hc) * inv_n
        yield
        hn = hc * lax.rsqrt(var + HEAD_NORM_EPS) * ng_ref[...]
        out = (hn + sk_ref[...] * xc_ref[0, pl.ds(r0, tile), :]) * _silu(za_ref[pl.ds(r0, tile), :])
        o_ref[0, pl.ds(r0, tile), :] = out

    unroll = 4 if (seq // tile) % 4 == 0 else 1

    def finishes(i, carry):
        _round_robin([finish(pl.multiple_of((i * unroll + u) * tile, tile)) for u in range(unroll)])
        return carry

    lax.fori_loop(0, seq // tile // unroll, finishes, 0)


def _mlstm(q, k, v, xc, proj_g, gcol, norm_g, skip):
    bsz, seq, _ = q.shape
    grp = min(ML_GROUP, seq // CHUNK_A)
    hs = lambda: pl.BlockSpec((1, seq, DH_A), lambda b, h: (b, 0, h))
    return pl.pallas_call(
        _mlstm_kernel,
        grid=(bsz, H_A),
        in_specs=[
            hs(), hs(), hs(), hs(),
            pl.BlockSpec((seq, DH_A), lambda b, h: (b, G_AZ // DH_A + h)),
            pl.BlockSpec((1, seq, LANES), lambda b, h: (b, 0, 0)),
            pl.BlockSpec((1, DH_A), lambda b, h: (0, h)),
            pl.BlockSpec((1, DH_A), lambda b, h: (0, h)),
        ],
        out_specs=pl.BlockSpec((1, seq, DH_A), lambda b, h: (b, 0, h)),
        out_shape=jax.ShapeDtypeStruct((bsz, seq, D_A), F32),
        scratch_shapes=[pltpu.VMEM((seq, DH_A), F32), pltpu.VMEM((2, DH_A, DH_A), F32),
                        pltpu.VMEM((2, 1, DH_A), F32), pltpu.VMEM((2, 1, LANES), F32),
                        pltpu.VMEM((2, grp, CHUNK_A, DH_A), F32),
                        pltpu.VMEM((2, grp, CHUNK_A, LANES), F32),
                        pltpu.VMEM((2, grp, CHUNK_A, LANES), F32),
                        pltpu.VMEM((2, grp, CHUNK_A, LANES), F32),
                        pltpu.VMEM((2, grp, DH_A, DH_A), F32),
                        pltpu.VMEM((2, grp, SUBLANES, LANES), F32),
                        pltpu.VMEM((2, 3 * LANES, 2 * LANES), BF16)],
        compiler_params=_cparams(("arbitrary", "arbitrary")),
        name="mlstm",
    )(q, k, v, xc, proj_g, gcol, norm_g.reshape(1, D_A), skip.reshape(1, D_A))


FFT_N2 = 128
HY_CT = 128
HY_PASSES = 1
FFT_ROWS_UNROLL = 8
FFT_SLAB_UNROLL = 4


def _split_bf16(x):
    hi = x.astype(BF16)
    lo = (x - hi.astype(F32)).astype(BF16)
    return hi, lo


def _dot3(a_hi, a_lo, x):
    x_hi, x_lo = _split_bf16(x)
    d = lambda a, b: jnp.dot(a, b, preferred_element_type=F32)
    if HY_PASSES == 1:
        return d(a_hi, x_hi)
    return d(a_hi, x_hi) + (d(a_lo, x_hi) + d(a_hi, x_lo))


def _dft_constants(seq):
    n = 2 * seq
    n2 = FFT_N2
    n1 = n // n2
    f1 = np.arange(n1)[:, None]
    s1 = np.arange(n1)[None, :]
    th1 = 2.0 * np.pi * f1 * s1 / n1
    fwd1 = np.concatenate([np.cos(th1), -np.sin(th1)], axis=0)
    inv1 = np.concatenate([np.cos(th1), -np.sin(th1)], axis=1)[:n1 // 2] / n
    a = np.arange(n2)
    th2 = 2.0 * np.pi * a[:, None] * a[None, :] / n2
    c2, s2 = np.cos(th2), np.sin(th2)
    fwd2 = np.block([[c2, s2], [-s2, c2]])
    inv2 = np.block([[c2, -s2], [s2, c2]])
    tht = 2.0 * np.pi * a[:, None] * np.arange(n1)[None, :] / n
    tw = np.zeros((2, n2, LANES), np.float64)
    tw[0, :, :n1] = np.cos(tht)
    tw[1, :, :n1] = -np.sin(tht)

    def hl(m):
        m32 = jnp.asarray(m, F32)
        hi = m32.astype(BF16)
        lo = (m32 - hi.astype(F32)).astype(BF16)
        return jnp.stack([hi, lo])

    nf = n1 // 2 + 1
    nfp = -(-nf // SUBLANES) * SUBLANES
    keep = np.zeros((nfp, 1))
    keep[:nf] = 1.0
    thh = 2.0 * np.pi * np.arange(nfp)[:, None] * s1 / n1
    fwd1h = np.concatenate([np.cos(thh) * keep, -np.sin(thh) * keep], axis=0)
    wgt = 2.0 * keep
    wgt[0] = wgt[nf - 1] = 1.0
    inv1h = np.concatenate([(np.cos(thh) * wgt).T, (-np.sin(thh) * wgt).T], axis=1)[:n1 // 2] / n
    return dict(fwd1=hl(fwd1), inv1=hl(inv1), fwd2=hl(fwd2), inv2=hl(inv2),
                fwd1h=hl(fwd1h), inv1h=hl(inv1h), tw=jnp.asarray(tw, F32), n1=n1)


def _fft_stage_a(load_rows, k_rows, fwd1_ref, a_ref):
    n2 = FFT_N2
    n1 = fwd1_ref.shape[1] // 2
    f_hi = fwd1_ref[0, :, :k_rows]
    f_lo = fwd1_ref[1, :, :k_rows]

    def one(s2):
        m = _dot3(f_hi, f_lo, load_rows(s2))
        yield
        a_ref[pl.ds(s2, n1, stride=2 * n2), :] = m[:n1]
        a_ref[pl.ds(n2 + s2, n1, stride=2 * n2), :] = m[n1:]

    def body(i, carry):
        _round_robin([one(i * FFT_ROWS_UNROLL + u) for u in range(FFT_ROWS_UNROLL)])
        return carry

    lax.fori_loop(0, n2 // FFT_ROWS_UNROLL, body, 0)


def _twiddle_cols(tw_ref, f1):
    lane = lax.broadcasted_iota(jnp.int32, (FFT_N2, LANES), 1)
    sel = lane == f1
    twr = jnp.sum(jnp.where(sel, tw_ref[0], 0.0), axis=-1, keepdims=True)
    twi = jnp.sum(jnp.where(sel, tw_ref[1], 0.0), axis=-1, keepdims=True)
    return twr, twi


def _hyena_filter_kernel(feat_ref, w1_ref, b1_ref, w2_ref, b2_ref, w3_ref, b3_ref, fr_ref,
                         wo_ref, dec_ref, fwd1_ref, fwd2_ref, tw_ref, o_ref, kt_ref, a_ref, hid_ref):
    n = feat_ref.shape[0]
    seq = n // 2
    n2 = FFT_N2
    n1 = n // n2
    tile = 512
    freq = fr_ref[...]

    @pl.when(pl.program_id(0) == 0)
    def _():
        def hidden(i, carry):
            r0 = pl.multiple_of(i * tile, tile)
            hid = jnp.sin(freq * (_hdot(feat_ref[pl.ds(r0, tile), :], w1_ref[...]) + b1_ref[...]))
            hid = jnp.sin(freq * (_hdot(hid, w2_ref[...]) + b2_ref[...]))
            hid_ref[pl.ds(r0, tile), :] = jnp.sin(freq * (_hdot(hid, w3_ref[...]) + b3_ref[...]))
            return carry

        lax.fori_loop(0, n // tile, hidden, 0)

    def gen(i, carry):
        r0 = pl.multiple_of(i * tile, tile)
        second = r0 >= seq
        wo = jnp.where(second, wo_ref[1, 0], wo_ref[0, 0])
        dec = jnp.where(second, dec_ref[1, 0], dec_ref[0, 0])
        t = feat_ref[pl.ds(r0, tile), 0:1]
        filt = _hdot(hid_ref[pl.ds(r0, tile), :], wo) * jnp.exp(-t * jnp.abs(dec))
        rows = r0 + lax.broadcasted_iota(jnp.int32, (tile, 1), 0)
        kt_ref[pl.ds(r0, tile), :] = jnp.where(rows == seq, 0.0, filt)
        return carry

    lax.fori_loop(0, n // tile, gen, 0)

    _fft_stage_a(lambda s2: kt_ref[pl.ds(s2, n1, stride=n2), :], n1, fwd1_ref, a_ref)

    nf = n1 // 2 + 1

    def slab(f1):
        r0 = f1 * 2 * n2 if isinstance(f1, int) else pl.multiple_of(f1 * 2 * n2, 2 * n2)
        ar = a_ref[pl.ds(r0, n2), :]
        ai = a_ref[pl.ds(r0 + n2, n2), :]
        twr, twi = _twiddle_cols(tw_ref, f1)
        x = jnp.concatenate([ar * twr - ai * twi, ar * twi + ai * twr], axis=0)
        xf = _dot3(fwd2_ref[0], fwd2_ref[1], x)
        yield
        o_ref[0, pl.ds(r0, 2 * n2), :] = xf

    def slabs(i, carry):
        _round_robin([slab(i * FFT_SLAB_UNROLL + u) for u in range(FFT_SLAB_UNROLL)])
        return carry

    lax.fori_loop(0, nf // FFT_SLAB_UNROLL, slabs, 0)
    _round_robin([slab(f1) for f1 in range(nf - nf % FFT_SLAB_UNROLL, nf)])


def _hyena_filter_spectrum(consts, feats_full, w1p, b1, w2, b2, w3, b3, freq, w_out, decay):
    n = feats_full.shape[0]
    nct = D_B // HY_CT
    spec_rows = (consts['n1'] // 2 + 1) * 2 * FFT_N2
    full = lambda a: pl.BlockSpec(a.shape, lambda j: (0,) * a.ndim)
    wo = w_out.reshape(HY_HID, 2, nct, HY_CT).transpose(1, 2, 0, 3)
    dec = decay.reshape(2, nct, 1, HY_CT)
    small = (w1p, b1.reshape(1, -1), w2, b2.reshape(1, -1), w3, b3.reshape(1, -1), freq.reshape(1, -1))
    return pl.pallas_call(
        _hyena_filter_kernel,
        grid=(nct,),
        in_specs=[full(feats_full)] + [full(a) for a in small] + [
            pl.BlockSpec((2, 1, HY_HID, HY_CT), lambda j: (0, j, 0, 0)),
            pl.BlockSpec((2, 1, 1, HY_CT), lambda j: (0, j, 0, 0)),
            full(consts['fwd1h']), full(consts['fwd2']), full(consts['tw']),
        ],
        out_specs=pl.BlockSpec((1, spec_rows, HY_CT), lambda j: (j, 0, 0)),
        out_shape=jax.ShapeDtypeStruct((nct, spec_rows, HY_CT), F32),
        scratch_shapes=[pltpu.VMEM((n, HY_CT), F32), pltpu.VMEM((2 * n, HY_CT), F32),
                        pltpu.VMEM((n, HY_HID), F32)],
        compiler_params=_cparams(("arbitrary",)),
        name="hyena_filter",
    )(feats_full, *small, wo, dec, consts['fwd1h'], consts['fwd2'], consts['tw'])


def _hyena_conv_kernel(z_ref, x0_ref, zb_ref, ks_ref, bias_ref, fwd1_ref, inv1_ref, fwd2_ref,
                       inv2_ref, tw_ref, o_ref, a_ref, y_ref):
    seq = z_ref.shape[1]
    n2 = FFT_N2
    n1 = 2 * seq // n2
    nh = n1 // 2
    nf = nh + 1
    nfp = fwd1_ref.shape[1] // 2

    _fft_stage_a(lambda s2: z_ref[0, pl.ds(s2, nh, stride=n2), :], nh, fwd1_ref, a_ref)

    def slab(f1):
        r0 = f1 * 2 * n2 if isinstance(f1, int) else pl.multiple_of(f1 * 2 * n2, 2 * n2)
        ar = a_ref[pl.ds(r0, n2), :]
        ai = a_ref[pl.ds(r0 + n2, n2), :]
        twr, twi = _twiddle_cols(tw_ref, f1)
        x = jnp.concatenate([ar * twr - ai * twi, ar * twi + ai * twr], axis=0)
        xf = _dot3(fwd2_ref[0], fwd2_ref[1], x)
        yield
        xr, xi = xf[:n2], xf[n2:]
        kr = ks_ref[0, pl.ds(r0, n2), :]
        ki = ks_ref[0, pl.ds(r0 + n2, n2), :]
        y = jnp.concatenate([xr * kr - xi * ki, xr * ki + xi * kr], axis=0)
        bb = _dot3(inv2_ref[0], inv2_ref[1], y)
        yield
        br, bi = bb[:n2], bb[n2:]
        a_ref[pl.ds(r0, n2), :] = br * twr + bi * twi
        a_ref[pl.ds(r0 + n2, n2), :] = bi * twr - br * twi

    def slabs(i, carry):
        _round_robin([slab(i * FFT_SLAB_UNROLL + u) for u in range(FFT_SLAB_UNROLL)])
        return carry

    lax.fori_loop(0, nf // FFT_SLAB_UNROLL, slabs, 0)
    _round_robin([slab(f1) for f1 in range(nf - nf % FFT_SLAB_UNROLL, nf)])

    i_hi = inv1_ref[0]
    i_lo = inv1_ref[1]

    def stage_d(s2):
        br = a_ref[pl.ds(s2, nfp, stride=2 * n2), :]
        bi = a_ref[pl.ds(n2 + s2, nfp, stride=2 * n2), :]
        y = _dot3(i_hi, i_lo, jnp.concatenate([br, bi], axis=0))
        yield
        y_ref[pl.ds(s2, nh, stride=n2), :] = y

    def stage_ds(i, carry):
        _round_robin([stage_d(i * FFT_ROWS_UNROLL + u) for u in range(FFT_ROWS_UNROLL)])
        return carry

    lax.fori_loop(0, n2 // FFT_ROWS_UNROLL, stage_ds, 0)

    tile = 256
    def finish(i, carry):
        r0 = pl.multiple_of(i * tile, tile)
        z = z_ref[0, pl.ds(r0, tile), :]
        y = y_ref[pl.ds(r0, tile), :] + bias_ref[...] * z
        o_ref[0, pl.ds(r0, tile), :] = x0_ref[0, pl.ds(r0, tile), :] * y * _silu(zb_ref[pl.ds(r0, tile), :])
        return carry

    lax.fori_loop(0, seq // tile, finish, 0)


def _hyena_conv(consts, z, x0, proj_g, kspec, bias):
    bsz, seq, _ = z.shape
    nct = D_B // HY_CT
    full = lambda a: pl.BlockSpec(a.shape, lambda j, b: (0,) * a.ndim)
    cs = pl.BlockSpec((1, seq, HY_CT), lambda j, b: (b, 0, j))
    mats = (consts['fwd1h'], consts['inv1h'], consts['fwd2'], consts['inv2'], consts['tw'])
    return pl.pallas_call(
        _hyena_conv_kernel,
        grid=(nct, bsz),
        in_specs=[
            cs, cs,
            pl.BlockSpec((seq, HY_CT), lambda j, b: (b, G_BZ // HY_CT + j)),
            pl.BlockSpec((1, kspec.shape[1], HY_CT), lambda j, b: (j, 0, 0)),
            pl.BlockSpec((1, HY_CT), lambda j, b: (0, j)),
        ] + [full(m) for m in mats],
        out_specs=cs,
        out_shape=jax.ShapeDtypeStruct((bsz, seq, D_B), F32),
        scratch_shapes=[pltpu.VMEM((4 * seq, HY_CT), F32), pltpu.VMEM((seq, HY_CT), F32)],
        compiler_params=_cparams(("arbitrary", "arbitrary")),
        name="hyena_conv",
    )(z, x0, proj_g, kspec, bias.reshape(1, D_B), *mats)


def _rwkv_kernel(r_ref, k_ref, v_ref, ul_ref, cz_ref, w0_ref, w2_ref, a0_ref, a2_ref,
                 kks_ref, ka_ref, rk_ref, lg_ref, lb_ref, o_ref, y_ref, kb_ref, s_ref,
                 wr_ref, u_ref, arb_ref, y0_ref, bt_ref, kv_ref, egc_ref,
                 kk_ref, lw_ref, be_ref, kd_ref):
    seq = r_ref.shape[1]
    L = CHUNK_C
    nc = seq // L
    P2 = 2 * L
    lane = lax.broadcasted_iota(jnp.int32, (L, LANES), 1)
    head0 = lane < HEAD_C
    prow = lax.broadcasted_iota(jnp.int32, (P2, P2), 0)
    pcol = lax.broadcasted_iota(jnp.int32, (P2, P2), 1)
    bd = (prow // HEAD_C) == (pcol // HEAD_C)
    blk = jnp.where(bd, 1.0, 0.0).astype(F32)
    trow = lax.broadcasted_iota(jnp.int32, (L, P2), 0)
    scol = lax.broadcasted_iota(jnp.int32, (L, P2), 1) % L
    eye = jnp.where(trow == scol, 1.0, 0.0).astype(F32)
    crow = lax.broadcasted_iota(jnp.int32, (L, L), 0)
    ccol = lax.broadcasted_iota(jnp.int32, (L, L), 1)

    def stack(x):
        return jnp.concatenate([jnp.where(head0, x, 0.0), jnp.where(head0, 0.0, x)], axis=0)

    blk_b = blk.astype(BF16)
    grp = wr_ref.shape[1]

    def dot(a, b):
        return jnp.dot(a, b, preferred_element_type=F32)

    def each(f, *cols):
        return [f(*xs) for xs in zip(*cols)]

    def prepare(probs):
        dirs = [d for _, d, _ in probs]
        fwd = [d == 0 for d in dirs]
        strict = [(scol < trow) if f else (scol > trow) for f in fwd]
        incl = [(scol <= trow) if f else (scol >= trow) for f in fwd]
        cum = [jnp.where((ccol <= crow) if f else (ccol >= crow), 1.0, 0.0).astype(BF16) for f in fwd]
        c0 = [pl.multiple_of(j * L, L) for j, _, _ in probs]
        r = [r_ref[0, pl.ds(c, L), :] for c in c0]
        v = [v_ref[0, pl.ds(c, L), :] for c in c0]
        kk = [kk_ref[pl.ds(c, L), :] for c in c0]
        lwt = [lw_ref[d, pl.ds(c, L), :] for c, d in zip(c0, dirs)]
        be = [be_ref[d, pl.ds(c, L), :] for c, d in zip(c0, dirs)]
        kd = [kd_ref[d, pl.ds(c, L), :] for c, d in zip(c0, dirs)]

        def cumsum(cm, x):
            l1 = x.astype(BF16)
            e1 = x - l1.astype(F32)
            l2 = e1.astype(BF16)
            l3 = (e1 - l2.astype(F32)).astype(BF16)
            return dot(jnp.concatenate([cm, cm, cm], axis=1), jnp.concatenate([l1, l2, l3], axis=0))

        g = each(cumsum, cum, lwt)
        yield
        eng = [jnp.exp(-x) for x in g]
        egc = [jnp.exp(x[L - 1:L, :] if f else x[0:1, :]) for x, f in zip(g, fwd)]
        at = each(lambda x, y, z: (-x * jnp.exp(y - z)).astype(BF16), kk, g, lwt)
        rt_ = each(lambda x, y: (x * jnp.exp(y)).astype(BF16), r, g)
        bh = each(lambda x, y: x * y, be, eng)
        kh = each(lambda x, y: x * y, kd, eng)
        lhs = each(lambda x, y: jnp.concatenate([x, y], axis=0), at, rt_)
        rhs = each(lambda x, y: jnp.concatenate([stack(x), stack(y)], axis=0).astype(BF16), bh, kh)
        g1 = each(lambda x, y: lax.dot_general(x, y, (((1,), (1,)), ((), ())),
                                               preferred_element_type=F32), lhs, rhs)
        yield
        aab = each(lambda m, x: jnp.where(m, x[:L, :P2], 0.0), strict, g1)
        aak_ark = each(lambda ms, mi, x: jnp.concatenate(
            [jnp.where(ms, x[:L, P2:], 0.0), jnp.where(mi, x[L:, P2:], 0.0)], axis=0).astype(BF16),
            strict, incl, g1)
        for (_, d, slot), mi, x in zip(probs, incl, g1):
            arb_ref[d, slot] = jnp.where(mi, x[L:, :P2], 0.0).astype(BF16)
        t = [eye + x for x in aab]
        p = [x.astype(BF16) for x in aab]
        vs = [stack(x).astype(BF16) for x in v]
        av = each(dot, aak_ark, vs)
        p = [dot(x, stack(x)) for x in p]
        yield
        p = [x.astype(BF16) for x in p]
        sq = 2
        while sq < L:
            if 2 * sq >= L:
                pt = each(lambda x, y: dot(x, stack(y.astype(BF16))), p, t)
                yield
                t = each(lambda y, x: y + x, t, pt)
            else:
                pt = each(lambda x, y: dot(x, jnp.concatenate([stack(x), stack(y.astype(BF16))], axis=1)),
                          p, t)
                yield
                p = [x[:, :P2].astype(BF16) for x in pt]
                t = each(lambda y, x: y + x[:, P2:], t, pt)
            sq *= 2
        wu = each(lambda x, y, z: dot(x.astype(BF16), jnp.concatenate(
            [stack(y), stack(z[:L].astype(BF16))], axis=1)), t, at, av)
        kv = each(lambda x, y, z: lax.dot_general(
            x.astype(BF16), (y * z).astype(BF16), (((0,), (0,)), ((), ())), preferred_element_type=F32),
            v, kh, egc)
        yield
        for i, (_, d, slot) in enumerate(probs):
            wr_ref[d, slot] = jnp.concatenate([wu[i][:, :P2].astype(BF16), rt_[i]], axis=0)
            u_ref[d, slot] = wu[i][:, P2:].astype(BF16)
            y0_ref[d, slot] = av[i][L:]
            bt_ref[d, slot] = (bh[i] * egc[i]).astype(BF16)
            kv_ref[d, slot] = jnp.where(bd, kv[i], 0.0).astype(BF16)
            egc_ref[d, slot] = egc[i]

    def advance(steps):
        s = [s_ref[d] for d in range(2)]
        for probs in steps:
            x = [lax.dot_general(wr_ref[d, slot], s[d].astype(BF16), (((1,), (1,)), ((), ())),
                                 preferred_element_type=F32) for _, d, slot in probs]
            yield
            sa = [(y[:L] + u_ref[d, slot].astype(F32)).astype(BF16) for (_, d, slot), y in zip(probs, x)]
            upd = [lax.dot_general(y, bt_ref[d, slot], (((0,), (0,)), ((), ())),
                                   preferred_element_type=F32) for (_, d, slot), y in zip(probs, sa)]
            z = [dot(arb_ref[d, slot], stack(y)) for (_, d, slot), y in zip(probs, sa)]
            yield
            for i, (j, d, slot) in enumerate(probs):
                s[d] = (s[d] * egc_ref[d, slot] + kv_ref[d, slot].astype(F32)
                        + jnp.where(bd, upd[i], 0.0))
                c0 = pl.multiple_of(j * L, L)
                y_ref[pl.ds(c0, L), :] += x[i][L:] + z[i] + y0_ref[d, slot]
        for d in range(2):
            s_ref[d] = s[d]

    s_ref[...] = jnp.zeros_like(s_ref)
    y_ref[...] = jnp.zeros_like(y_ref)
    tile = 256

    def pointwise(r0):
        k = k_ref[0, pl.ds(r0, tile), :]
        lwla = ul_ref[0, pl.ds(r0, tile), :]
        th = jnp.tanh(lwla[:, :LANES]).astype(BF16)
        la = lwla[:, LANES:].astype(BF16)
        xw = [w0_ref[d:d + 1, :] + dot(th, w2_ref[d].astype(BF16)) for d in range(2)]
        aa = [a0_ref[d:d + 1, :] + dot(la, a2_ref[d].astype(BF16)) for d in range(2)]
        kkv = k * kks_ref[...]
        ss = dot((kkv * kkv).astype(BF16), blk_b)
        yield
        kk = kkv / jnp.maximum(jnp.sqrt(ss), 1e-12)
        kk_ref[pl.ds(r0, tile), :] = kk
        kb = None
        for d in range(2):
            lw_ref[d, pl.ds(r0, tile), :] = -math.exp(-0.5) * _sigmoid(xw[d])
            a = _sigmoid(aa[d])
            kd = k * (1.0 + (a - 1.0) * ka_ref[...])
            be_ref[d, pl.ds(r0, tile), :] = kk * a
            kd_ref[d, pl.ds(r0, tile), :] = kd
            kb = kd if kb is None else kb + kd
        kb_ref[pl.ds(r0, tile), :] = kb

    pw_unroll = 2 if (seq // tile) % 2 == 0 else 1

    def pointwise_loop(i, carry):
        _round_robin([pointwise(pl.multiple_of((i * pw_unroll + u) * tile, tile))
                      for u in range(pw_unroll)])
        return carry

    lax.fori_loop(0, seq // tile // pw_unroll, pointwise_loop, 0)

    lock = grp // 2
    nsets = nc // lock

    def chunk_set(i):
        base = (i % 2) * lock
        return [[(i * lock + q, 0, base + q), (nc - 1 - (i * lock + q), 1, base + q)]
                for q in range(lock)]

    _round_robin([prepare(sum(chunk_set(0), []))])

    def pipelined(i, carry):
        _round_robin([prepare(sum(chunk_set(i + 1), [])), advance(chunk_set(i))])
        return carry

    lax.fori_loop(0, nsets - 1, pipelined, 0)
    _round_robin([advance(chunk_set(nsets - 1))])

    tile = 256
    inv_n = 1.0 / HEAD_C
    blk2 = jnp.concatenate([blk_b, blk_b], axis=0)

    def head_sum(x):
        hi, lo = _split_bf16(x)
        return dot(jnp.concatenate([hi, lo], axis=1), blk2)

    def finish(r0):
        y = y_ref[pl.ds(r0, tile), :]
        mu = head_sum(y) * inv_n
        r = r_ref[0, pl.ds(r0, tile), :]
        v = v_ref[0, pl.ds(r0, tile), :]
        kbon = 0.5 * kb_ref[pl.ds(r0, tile), :]
        bonus = head_sum(r * kbon * rk_ref[...]) * v
        yield
        yc = y - mu
        var = head_sum(yc * yc) * inv_n
        yield
        yn = yc * lax.rsqrt(var + LNX_EPS) * lg_ref[...] + lb_ref[...]
        o_ref[0, pl.ds(r0, tile), :] = (yn + bonus) * _silu(cz_ref[pl.ds(r0, tile), :])

    unroll = 4 if (seq // tile) % 4 == 0 else 1

    def finishes(i, carry):
        _round_robin([finish(pl.multiple_of((i * unroll + u) * tile, tile)) for u in range(unroll)])
        return carry

    lax.fori_loop(0, seq // tile // unroll, finishes, 0)


def _rwkv(u, ul, proj_g, w0, w2pad, a0, a2pad, kk_s, ka, rk, lnx_g, lnx_b):
    bsz, seq, _ = u.shape
    npair = D_C // LANES
    grp = min(RW_GROUP, seq // CHUNK_C)
    col = lambda off: pl.BlockSpec((1, seq, LANES), lambda b, p: (b, 0, off + p))
    vec = pl.BlockSpec((1, LANES), lambda b, p: (0, p))
    two = pl.BlockSpec((2, LANES), lambda b, p: (0, p))
    lora = pl.BlockSpec((2, LANES, LANES), lambda b, p: (0, 0, p))
    row = lambda a: a.reshape(1, D_C)
    return pl.pallas_call(
        _rwkv_kernel,
        grid=(bsz, npair),
        in_specs=[
            col(0), col(npair), col(2 * npair),
            pl.BlockSpec((1, seq, 2 * LANES), lambda b, p: (b, 0, 0)),
            pl.BlockSpec((seq, LANES), lambda b, p: (b, G_CZ // LANES + p)),
            two, lora, two, lora, vec, vec, vec, vec, vec,
        ],
        out_specs=pl.BlockSpec((1, seq, LANES), lambda b, p: (b, 0, p)),
        out_shape=jax.ShapeDtypeStruct((bsz, seq, D_C), F32),
        scratch_shapes=[pltpu.VMEM((seq, LANES), F32), pltpu.VMEM((seq, LANES), F32),
                        pltpu.VMEM((2, LANES, LANES), F32),
                        pltpu.VMEM((2, grp, 2 * CHUNK_C, LANES), BF16),
                        pltpu.VMEM((2, grp, CHUNK_C, LANES), BF16),
                        pltpu.VMEM((2, grp, CHUNK_C, LANES), BF16),
                        pltpu.VMEM((2, grp, CHUNK_C, LANES), F32),
                        pltpu.VMEM((2, grp, CHUNK_C, LANES), BF16),
                        pltpu.VMEM((2, grp, LANES, LANES), BF16),
                        pltpu.VMEM((2, grp, 1, LANES), F32),
                        pltpu.VMEM((seq, LANES), F32), pltpu.VMEM((2, seq, LANES), F32),
                        pltpu.VMEM((2, seq, LANES), F32), pltpu.VMEM((2, seq, LANES), F32)],
        compiler_params=_cparams(("arbitrary", "arbitrary")),
        name="rwkv7",
    )(u, u, u, ul, proj_g, w0, w2pad, a0, a2pad, row(kk_s), row(ka), row(rk), row(lnx_g), row(lnx_b))


def _out_kernel(ya_ref, yb_ref, yc_ref, ga_ref, gb_ref, gc_ref, x_ref, gate_ref, pg_ref,
                wa_ref, wb_ref, wc_ref, wo_ref, o_ref):
    merged = (_sigmoid(ga_ref[...]) * _bdot(ya_ref[...], wa_ref[...])
              + _sigmoid(gb_ref[...]) * _bdot(yb_ref[...], wb_ref[...])
              + _sigmoid(gc_ref[...]) * _bdot(yc_ref[...], wc_ref[...]))
    out = _bdot(merged, wo_ref[...])
    y = out * lax.rsqrt(jnp.mean(out * out, axis=-1, keepdims=True) + NORM_EPS) * pg_ref[...]
    o_ref[...] = x_ref[...] + gate_ref[0] * y


def _merge_out(ya, yb, yc, proj_g, x2d, seq, gate, post_g, wa, wb, wc, wo, tm):
    rows, d = x2d.shape
    per_b = seq // tm
    full = lambda a: pl.BlockSpec(a.shape, lambda i: (0,) * a.ndim)
    gcol = lambda off: pl.BlockSpec((tm, d), lambda i: (i, off // d))
    return pl.pallas_call(
        _out_kernel,
        grid=(rows // tm,),
        in_specs=[
            pl.BlockSpec((tm, D_A), lambda i: (i, 0)),
            pl.BlockSpec((tm, D_B), lambda i: (i, 0)),
            pl.BlockSpec((tm, D_C), lambda i: (i, 0)),
            gcol(G_GA), gcol(G_GB), gcol(G_GC),
            pl.BlockSpec((tm, d), lambda i: (i, 0)),
            pl.BlockSpec((1, 1, d), lambda i: (i // per_b, 0, 0)),
            pl.BlockSpec((1, d), lambda i: (0, 0)),
            full(wa), full(wb), full(wc), full(wo),
        ],
        out_specs=pl.BlockSpec((tm, d), lambda i: (i, 0)),
        out_shape=jax.ShapeDtypeStruct((rows, d), F32),
        compiler_params=_cparams(("arbitrary",)),
        name="merge_out",
    )(ya.reshape(rows, D_A), yb.reshape(rows, D_B), yc.reshape(rows, D_C), proj_g, proj_g, proj_g,
      x2d, gate, post_g.reshape(1, d), wa, wb, wc, wo)


def _hyena_feats(seq):
    n = 2 * seq
    t = jnp.linspace(0.0, 1.0, seq, dtype=F32)
    w = 2.0 * math.pi * jnp.arange(seq, dtype=F32) / seq
    f = jnp.linspace(1e-4, HY_BANDS - 1, HY_BANDS, dtype=F32)
    zz = w[:, None] * f[None, :]
    feats = jnp.concatenate([t[:, None], jnp.cos(zz), -jnp.sin(zz)], axis=-1)
    pos = np.concatenate([np.arange(seq), [0], np.arange(seq - 1, 0, -1)])
    full = jnp.zeros((n, LANES), F32).at[:, :feats.shape[1]].set(feats[pos])
    return full


def _tiles(seq):
    return min(seq, 1024), min(seq, 256), min(seq, 512)


def kernel(x, c, ada_w, ada_b, pre_g, post_g, w_in, ml_conv_w, ml_conv_b, ml_wq, ml_wk, ml_wv, ml_gate_w, ml_gate_b, ml_norm_g, ml_skip, hy_conv_w, hy_conv_b, hy_w1, hy_b1, hy_w2, hy_b2, hy_w3, hy_b3, hy_freq, hy_w_out, hy_decay, hy_bias, rw_mu, rw_w0, rw_w2, rw_a0, rw_a2, rw_kk, rw_ka, rw_rk, rw_lnx_g, rw_lnx_b, w_branch_a, w_branch_b, w_branch_c, w_out):
    bsz, seq, d = x.shape
    depth = ada_w.shape[0]
    tm_in, ts_prep, tm_out = _tiles(seq)
    runs_h, runs_g = (_contiguous_runs(p) for p in _ref_column_perm())
    consts = _dft_constants(seq)
    feats_full = _hyena_feats(seq)
    mod = _modulation(c, ada_w, ada_b)
    x2d = x.reshape(bsz * seq, d)
    n_gate = 4 * H_A
    for l in range(depth):
        shift = mod[l, :, None, 0:d]
        scale = mod[l, :, None, d:2 * d]
        gate = mod[l, :, None, 2 * d:3 * d]
        w_h = jnp.concatenate([w_in[l][:, a:b] for a, b in runs_h], axis=1).astype(BF16)
        w_g = jnp.concatenate([w_in[l][:, a:b] for a, b in runs_g], axis=1).astype(BF16)
        proj_h = _inproj(x2d, seq, pre_g[l], shift, scale, w_h, tm_in, N_H // 3)
        proj_g = _inproj(x2d, seq, pre_g[l], shift, scale, w_g, tm_in, N_G // 4)

        gw = jnp.zeros((3 * D_A, LANES), F32).at[:, :n_gate].set(ml_gate_w[l])
        gb = jnp.zeros((1, LANES), F32).at[0, :n_gate].set(ml_gate_b[l])
        q, k, v, xc, gcol, z, x0, u, ul = _prep(
            proj_h, bsz, seq, ts_prep, ml_conv_w[l], ml_conv_b[l].reshape(1, D_A),
            ml_wq[l], ml_wk[l], ml_wv[l], gw, gb,
            hy_conv_w[l], hy_conv_b[l].reshape(1, 3 * D_B), rw_mu[l].reshape(1, -1))

        y_a = _mlstm(q, k, v, xc, proj_g, gcol, ml_norm_g[l], ml_skip[l])

        w1p = jnp.zeros((LANES, HY_HID), F32).at[:hy_w1.shape[1]].set(hy_w1[l])
        kspec = _hyena_filter_spectrum(consts, feats_full, w1p, hy_b1[l], hy_w2[l], hy_b2[l],
                                       hy_w3[l], hy_b3[l], hy_freq[l], hy_w_out[l], hy_decay[l])
        y_b = _hyena_conv(consts, z, x0, proj_g, kspec, hy_bias[l])

        w2pad = (jnp.zeros((2, LANES, D_C), F32).at[0, :LORA].set(rw_w2[l, 0])
                 .at[1, LORA:].set(rw_w2[l, 1]))
        a2pad = (jnp.zeros((2, LANES, D_C), F32).at[0, :LORA].set(rw_a2[l, 0])
                 .at[1, LORA:].set(rw_a2[l, 1]))
        y_c = _rwkv(u, ul, proj_g, rw_w0[l], w2pad, rw_a0[l], a2pad, rw_kk[l], rw_ka[l],
                    rw_rk[l].reshape(-1), rw_lnx_g[l], rw_lnx_b[l])

        x2d = _merge_out(y_a, y_b, y_c, proj_g, x2d, seq, gate, post_g[l],
                         w_branch_a[l].astype(BF16), w_branch_b[l].astype(BF16),
                         w_branch_c[l].astype(BF16), w_out[l].astype(BF16), tm_out)
    return x2d.reshape(bsz, seq, d)
```

```python
import functools
import math

import numpy as np
import jax
import jax.numpy as jnp
from jax import lax
from jax.experimental import pallas as pl
from jax.experimental.pallas import tpu as pltpu

D_MODEL = 1024
DEPTH = 4
D_A = 512
H_A = 4
DH_A = 128
CHUNK_A = 64
ML_GROUP = 16
D_B = 512
HY_BANDS = 16
HY_HID = 64
D_C = 1024
HEAD_C = 64
H_C = D_C // HEAD_C
LORA = 64
CHUNK_C = 64
RW_GROUP = 8
LNX_EPS = 64e-5
NORM_EPS = 1e-6
HEAD_NORM_EPS = 1e-5

LANES = 128
SUBLANES = 8
VMEM_LIMIT = 56 * 1024 * 1024

F32 = jnp.float32
BF16 = jnp.bfloat16
PROJ_DTYPE = BF16
PROJ_ROWS = 16
HIGHEST = lax.Precision.HIGHEST

N_H = D_A + 3 * D_B + 3 * D_C + 4 * LORA
N_G = D_A + D_B + D_C + 3 * D_MODEL
H_AX, H_BV, H_BX0, H_BX1, H_CR, H_CK, H_CV, H_LW, H_LA = (
    0, 512, 1024, 1536, 2048, 3072, 4096, 5120, 5248)
G_AZ, G_BZ, G_CZ, G_GA, G_GB, G_GC = 0, 512, 1024, 2048, 3072, 4096


def _ref_column_perm():
    r = lambda a, n: np.arange(a, a + n)
    a_x, a_z = r(0, 512), r(512, 512)
    b_v, b_x0, b_x1, b_z = r(1024, 512), r(1536, 512), r(2048, 512), r(2560, 512)
    c_r, c_k, c_v = r(3072, 1024), r(4096, 1024), r(5120, 1024)
    c_lw, c_la = r(6144, 128), r(6272, 128)
    c_z = r(6400, 1024)
    g_a, g_b, g_c = r(7424, 1024), r(8448, 1024), r(9472, 1024)
    h = np.concatenate([a_x, b_v, b_x0, b_x1, c_r, c_k, c_v, c_lw, c_la])
    g = np.concatenate([a_z, b_z, c_z, g_a, g_b, g_c])
    return h, g


def _contiguous_runs(idx):
    cuts = np.flatnonzero(np.diff(idx) != 1) + 1
    return [(int(seg[0]), int(seg[-1]) + 1) for seg in np.split(idx, cuts)]


def _cparams(sem, **extra):
    return pltpu.CompilerParams(dimension_semantics=sem, vmem_limit_bytes=VMEM_LIMIT, **extra)


def _silu(x):
    return x * (1.0 / (1.0 + jnp.exp(-x)))


def _sigmoid(x):
    return 1.0 / (1.0 + jnp.exp(-x))


def _bdot(a, b):
    return jnp.dot(a.astype(BF16), b.astype(BF16), preferred_element_type=F32)


def _bdot_nt(a, b):
    return lax.dot_general(a.astype(BF16), b.astype(BF16), (((1,), (1,)), ((), ())),
                           preferred_element_type=F32)


def _round_robin(gens):
    gens = list(gens)
    while gens:
        alive = []
        for g in gens:
            try:
                next(g)
                alive.append(g)
            except StopIteration:
                pass
        gens = alive


def _hdot(a, b):
    return jnp.dot(a, b, preferred_element_type=F32, precision=HIGHEST)


def _mod_kernel(c_ref, w_ref, b_ref, o_ref):
    cond = _silu(c_ref[...])
    o_ref[0] = _hdot(cond, w_ref[0]) + b_ref[0]


def _modulation(c, ada_w, ada_b):
    depth, d, n3 = ada_w.shape
    bsz = c.shape[0]
    nt = n3 // d
    return pl.pallas_call(
        _mod_kernel,
        grid=(depth, nt),
        in_specs=[
            pl.BlockSpec((bsz, d), lambda l, j: (0, 0)),
            pl.BlockSpec((1, d, d), lambda l, j: (l, 0, j)),
            pl.BlockSpec((1, 1, d), lambda l, j: (l, 0, j)),
        ],
        out_specs=pl.BlockSpec((1, bsz, d), lambda l, j: (l, 0, j)),
        out_shape=jax.ShapeDtypeStruct((depth, bsz, n3), F32),
        compiler_params=_cparams(("arbitrary", "arbitrary")),
        name="adaln_mod",
    )(c, ada_w, ada_b.reshape(depth, 1, n3))


def _inproj_kernel(x_ref, g_ref, shift_ref, scale_ref, w_ref, o_ref, h_ref):
    @pl.when(pl.program_id(1) == 0)
    def _():
        x = x_ref[...]
        y = x * lax.rsqrt(jnp.mean(x * x, axis=-1, keepdims=True) + NORM_EPS)
        h = y * g_ref[...] * (1.0 + scale_ref[0]) + shift_ref[0]
        h_ref[...] = h.astype(BF16)

    o_ref[...] = jnp.dot(h_ref[...], w_ref[...], preferred_element_type=F32).astype(o_ref.dtype)


def _inproj(x2d, seq, pre_g, shift, scale, w_bf16, tm, tn):
    rows, d = x2d.shape
    n = w_bf16.shape[1]
    per_b = seq // tm
    return pl.pallas_call(
        _inproj_kernel,
        grid=(rows // tm, n // tn),
        in_specs=[
            pl.BlockSpec((tm, d), lambda i, j: (i, 0)),
            pl.BlockSpec((1, d), lambda i, j: (0, 0)),
            pl.BlockSpec((1, 1, d), lambda i, j: (i // per_b, 0, 0)),
            pl.BlockSpec((1, 1, d), lambda i, j: (i // per_b, 0, 0)),
            pl.BlockSpec((d, tn), lambda i, j: (0, j)),
        ],
        out_specs=pl.BlockSpec((tm, tn), lambda i, j: (i, j)),
        out_shape=jax.ShapeDtypeStruct((rows, n), PROJ_DTYPE),
        scratch_shapes=[pltpu.VMEM((tm, d), BF16)],
        compiler_params=_cparams(("arbitrary", "arbitrary")),
        name="inproj",
    )(x2d, pre_g.reshape(1, d), shift, scale, w_bf16)


def _log_sigmoid(x):
    return jnp.minimum(x, 0.0) - jnp.log(1.0 + jnp.exp(-jnp.abs(x)))


def _prep_kernel(main_ref, prev_ref, next_ref,
                 mcw_ref, mcb_ref, wq_ref, wk_ref, wv_ref, gw_ref, gb_ref,
                 hcw_ref, hcb_ref, mu_ref,
                 q_ref, k_ref, v_ref, xc_ref, gcol_ref, z_ref, x0_ref, u_ref, ul_ref):
    i = pl.program_id(1)
    ts = main_ref.shape[0]
    has_prev = jnp.where(i > 0, 1.0, 0.0).astype(F32)
    has_next = jnp.where(i < pl.num_programs(1) - 1, 1.0, 0.0).astype(F32)

    def neighbours(c0, cw):
        x = main_ref[:, c0:c0 + cw].astype(F32)
        row = lax.broadcasted_iota(jnp.int32, (SUBLANES, cw), 0)
        p_row = prev_ref[:, c0:c0 + cw].astype(F32)[PROJ_ROWS - 1:PROJ_ROWS] * has_prev
        n_row = next_ref[:, c0:c0 + cw].astype(F32)[0:1] * has_next
        xp = pltpu.roll(x, 1, 0)
        xn = pltpu.roll(x, ts - 1, 0)
        xp = jnp.concatenate([jnp.where(row == 0, p_row, xp[:SUBLANES]), xp[SUBLANES:]], axis=0)
        xn = jnp.concatenate([xn[:ts - SUBLANES],
                              jnp.where(row == SUBLANES - 1, n_row, xn[ts - SUBLANES:])], axis=0)
        return xp, x, xn

    def conv(c0, cw, w_ref, b_ref, w0):
        xp, x, xn = neighbours(c0, cw)
        w = w_ref[:, w0:w0 + cw]
        return xp * w[0:1] + x * w[1:2] + xn * w[2:3] + b_ref[:, w0:w0 + cw], x

    gates = jnp.zeros((ts, LANES), F32) + gb_ref[...]
    for h in range(H_A):
        sl = slice(h * DH_A, (h + 1) * DH_A)
        conv_a, xa = conv(H_AX + h * DH_A, DH_A, mcw_ref, mcb_ref, h * DH_A)
        xc = _silu(conv_a)
        xc_ref[0, :, sl] = xc
        qh = _bdot(xc, wq_ref[h])
        kh = _bdot(xc, wk_ref[h])
        vh = _bdot(xa, wv_ref[h])
        gates += (_bdot(qh, gw_ref[h * DH_A:(h + 1) * DH_A])
                  + _bdot(kh, gw_ref[D_A + h * DH_A:D_A + (h + 1) * DH_A])
                  + _bdot(vh, gw_ref[2 * D_A + h * DH_A:2 * D_A + (h + 1) * DH_A]))
        q_ref[0, :, sl] = qh.astype(BF16)
        k_ref[0, :, sl] = (kh * (DH_A ** -0.5)).astype(BF16)
        v_ref[0, :, sl] = vh.astype(BF16)
    col = lax.broadcasted_iota(jnp.int32, (ts, LANES), 1)
    rmod = lax.broadcasted_iota(jnp.int32, (ts, LANES), 0) % CHUNK_A
    lf = _log_sigmoid(gates)
    cf = lf
    cb = lf
    sh = 1
    while sh < CHUNK_A:
        cf = cf + jnp.where(rmod >= sh, pltpu.roll(cf, sh, 0), 0.0)
        cb = cb + jnp.where(rmod < CHUNK_A - sh, pltpu.roll(cb, ts - sh, 0), 0.0)
        sh *= 2
    is_ff = (col >= H_A) & (col < 2 * H_A)
    is_fb = (col >= 3 * H_A) & (col < 4 * H_A)
    gcol_ref[0] = jnp.where(is_ff, cf, jnp.where(is_fb, cb, gates))

    cw = LANES
    for j in range(D_B // cw):
        c = j * cw
        cv, _ = conv(H_BV + c, cw, hcw_ref, hcb_ref, c)
        cx1, _ = conv(H_BX1 + c, cw, hcw_ref, hcb_ref, 2 * D_B + c)
        z_ref[0, :, c:c + cw] = cv * cx1
        cx0, _ = conv(H_BX0 + c, cw, hcw_ref, hcb_ref, D_B + c)
        x0_ref[0, :, c:c + cw] = cx0

    def shifted(c):
        xp, x, xn = neighbours(H_CR + c, cw)
        mu = mu_ref[:, c:c + cw]
        return x * (1.0 - mu) + (0.5 * mu) * (xp + xn)

    for j in range(3 * D_C // cw):
        u_ref[0, :, j * cw:(j + 1) * cw] = shifted(j * cw)
    for j in range(4 * LORA // cw):
        ul_ref[0, :, j * cw:(j + 1) * cw] = shifted(3 * D_C + j * cw)


def _prep(proj_h, bsz, seq, ts, ml_conv_w, ml_conv_b, wq, wk, wv, gate_w_pad, gate_b_pad,
          hy_conv_w, hy_conv_b, rw_mu):
    ns = seq // ts
    hb = ts // PROJ_ROWS
    last_hb = bsz * seq // PROJ_ROWS - 1
    full = lambda a: pl.BlockSpec(a.shape, lambda b, i: (0,) * a.ndim)
    seq_spec = lambda w: pl.BlockSpec((1, ts, w), lambda b, i: (b, i, 0))
    params = (ml_conv_w, ml_conv_b, wq, wk, wv, gate_w_pad, gate_b_pad, hy_conv_w, hy_conv_b, rw_mu)
    outs = [(D_A, BF16), (D_A, BF16), (D_A, BF16), (D_A, F32), (LANES, F32),
            (D_B, F32), (D_B, F32), (3 * D_C, F32), (4 * LORA, F32)]
    return pl.pallas_call(
        _prep_kernel,
        grid=(bsz, ns),
        in_specs=[
            pl.BlockSpec((ts, N_H), lambda b, i: (b * ns + i, 0)),
            pl.BlockSpec((PROJ_ROWS, N_H), lambda b, i: (jnp.maximum((b * ns + i) * hb - 1, 0), 0)),
            pl.BlockSpec((PROJ_ROWS, N_H), lambda b, i: (jnp.minimum((b * ns + i + 1) * hb, last_hb), 0)),
        ] + [full(p) for p in params],
        out_specs=[seq_spec(w) for w, _ in outs],
        out_shape=[jax.ShapeDtypeStruct((bsz, seq, w), dt) for w, dt in outs],
        compiler_params=_cparams(("arbitrary", "arbitrary")),
        name="prep",
    )(proj_h, proj_h, proj_h, *params)


def _mlstm_kernel(q_ref, k_ref, v_ref, xc_ref, za_ref, gcol_ref, ng_ref, sk_ref, o_ref,
                  h_ref, ct_ref, n_ref, m_ref, num_ref, den_ref, bb_ref, mb_ref, kv_ref, sc_ref,
                  sel_ref):
    head = pl.program_id(1)
    seq = q_ref.shape[1]
    L = CHUNK_A
    nc = seq // L
    row = lax.broadcasted_iota(jnp.int32, (L, L), 0)
    colm = lax.broadcasted_iota(jnp.int32, (L, L), 1)
    lane = lax.broadcasted_iota(jnp.int32, (L, LANES), 1)

    grp = num_ref.shape[1]
    srow = lax.broadcasted_iota(jnp.int32, (SUBLANES, LANES), 0)
    ones = jnp.ones((LANES, LANES), BF16)
    ones2 = jnp.ones((2 * LANES, LANES), BF16)

    def dot(a, b):
        return jnp.dot(a, b, preferred_element_type=F32)

    def split3(x):
        x1 = x.astype(BF16)
        e1 = x - x1.astype(F32)
        x2 = e1.astype(BF16)
        return x1, x2, (e1 - x2.astype(F32)).astype(BF16)

    srow_i = lax.broadcasted_iota(jnp.int32, (3 * LANES, 2 * LANES), 0) % LANES
    scol_i = lax.broadcasted_iota(jnp.int32, (3 * LANES, 2 * LANES), 1)
    for d in range(2):
        li_lane = head + 2 * H_A * d
        want = jnp.where(scol_i < LANES, li_lane + H_A, li_lane)
        sel_ref[d] = jnp.where(srow_i == want, 1.0, 0.0).astype(BF16)

    def prepare(j, d, slot):
        fwd = d == 0
        tri = (colm <= row) if fwd else (colm >= row)
        c0 = pl.multiple_of(j * L, L)
        q = q_ref[0, pl.ds(c0, L), :]
        k = k_ref[0, pl.ds(c0, L), :]
        v = v_ref[0, pl.ds(c0, L), :]
        qk = _bdot_nt(q, k)
        bl = dot(jnp.concatenate(split3(gcol_ref[0, pl.ds(c0, L), :]), axis=1), sel_ref[d])
        yield
        b, li = bl[:, :LANES], bl[:, LANES:]
        x0, x1, x2 = split3(b)
        y0, y1, y2 = split3(li - b)
        xl = jnp.where(lane == 0, x0, jnp.where(lane == 1, x1, jnp.where(
            lane == 2, x2, jnp.where(lane < 6, 1.0, 0.0).astype(BF16))))
        yl = jnp.where(lane == 3, y0, jnp.where(lane == 4, y1, jnp.where(
            lane == 5, y2, jnp.where(lane < 3, 1.0, 0.0).astype(BF16))))
        dm = _bdot_nt(xl, yl)
        gtot = b[L - 1:L, :] if fwd else b[0:1, :]
        a = gtot - b + li
        a_max = jnp.max(a, axis=0, keepdims=True)
        wk = jnp.exp(a - a_max)
        vw = (v.astype(F32) * wk).astype(BF16)
        kv = lax.dot_general(k, vw, (((0,), (0,)), ((), ())), preferred_element_type=F32)
        kn = jnp.sum(k.astype(F32) * wk, axis=0, keepdims=True)
        sc_ref[d, slot] = jnp.where(srow == 0, kn, jnp.where(srow == 1, gtot, a_max))
        bb_ref[d, slot] = b
        yield
        dm = jnp.where(tri, dm, -jnp.inf)
        m_loc = jnp.max(dm, axis=-1, keepdims=True)
        s = qk * jnp.exp(dm - m_loc)
        num = _bdot(s, v)
        den = dot(jnp.concatenate(_split_bf16(s), axis=1), ones)
        mb_ref[d, slot] = jnp.broadcast_to(m_loc, (L, LANES))
        kv_ref[d, slot] = kv
        yield
        num_ref[d, slot] = num
        den_ref[d, slot] = den

    def advance(d, todo):
        ct, n, m = ct_ref[d], n_ref[d], m_ref[d]
        pending = []
        for j, slot in todo:
            c0 = pl.multiple_of(j * L, L)
            q = q_ref[0, pl.ds(c0, L), :]
            qc = _bdot(q, ct)
            qn = dot(jnp.concatenate(_split_bf16(q.astype(F32) * n), axis=1), ones2)
            sc = sc_ref[d, slot]
            kn, gtot, a_max = sc[0:1, :], sc[1:2, :], sc[2:3, :]
            m_new = jnp.maximum(gtot + m, a_max)
            decay = jnp.exp(gtot + m - m_new)
            beta = jnp.exp(a_max - m_new)
            pending.append((c0, slot, qc, qn, m))
            ct = decay * ct + beta * kv_ref[d, slot]
            n = decay * n + beta * kn
            m = m_new
        ct_ref[d], n_ref[d], m_ref[d] = ct, n, m
        yield
        for c0, slot, qc, qn, m_in in pending:
            inter = bb_ref[d, slot] + m_in
            m_loc = mb_ref[d, slot]
            m_t = jnp.maximum(inter, m_loc)
            c_intra = jnp.exp(m_loc - m_t)
            c_inter = jnp.exp(inter - m_t)
            num = c_intra * num_ref[d, slot] + c_inter * qc
            den = c_intra * den_ref[d, slot] + c_inter * qn
            h_ref[pl.ds(c0, L), :] += num / jnp.maximum(jnp.abs(den), jnp.exp(-m_t))

    ct_ref[...] = jnp.zeros_like(ct_ref)
    n_ref[...] = jnp.zeros_like(n_ref)
    m_ref[...] = jnp.full_like(m_ref, -jnp.inf)
    h_ref[...] = jnp.zeros_like(h_ref)
    lock = 8 if grp % 8 == 0 else 1

    def group(gi, carry):
        def phase_a(i, c):
            gens = []
            for u in range(lock):
                jf = gi * grp + i * lock + u
                gens += [prepare(jf, 0, i * lock + u), prepare(nc - 1 - jf, 1, i * lock + u)]
            _round_robin(gens)
            return c

        def phase_b(i, c):
            jf = [gi * grp + i * lock + u for u in range(lock)]
            _round_robin([advance(0, [(j, i * lock + u) for u, j in enumerate(jf)]),
                          advance(1, [(nc - 1 - j, i * lock + u) for u, j in enumerate(jf)])])
            return c

        lax.fori_loop(0, grp // lock, phase_a, 0)
        lax.fori_loop(0, grp // lock, phase_b, 0)
        return carry

    lax.fori_loop(0, nc // grp, group, 0)

    tile = 256
    inv_n = 1.0 / DH_A

    def lane_sum(x):
        return dot(jnp.concatenate(_split_bf16(x), axis=1), ones2)

    def finish(r0):
        hh = h_ref[pl.ds(r0, tile), :]
        mu = lane_sum(hh) * inv_n
        yield
        hc = hh - mu
        var = lane_sum(hc * hc) * inv_n
        yield
        hn = hc * lax.rsqrt(var + HEAD_NORM_EPS) * ng_ref[...]
        out = (hn + sk_ref[...] * xc_ref[0, pl.ds(r0, tile), :]) * _silu(za_ref[pl.ds(r0, tile), :].astype(F32))
        o_ref[0, pl.ds(r0, tile), :] = out

    unroll = 4 if (seq // tile) % 4 == 0 else 1

    def finishes(i, carry):
        _round_robin([finish(pl.multiple_of((i * unroll + u) * tile, tile)) for u in range(unroll)])
        return carry

    lax.fori_loop(0, seq // tile // unroll, finishes, 0)


def _mlstm(q, k, v, xc, proj_g, gcol, norm_g, skip):
    bsz, seq, _ = q.shape
    grp = min(ML_GROUP, seq // CHUNK_A)
    hs = lambda: pl.BlockSpec((1, seq, DH_A), lambda b, h: (b, 0, h))
    return pl.pallas_call(
        _mlstm_kernel,
        grid=(bsz, H_A),
        in_specs=[
            hs(), hs(), hs(), hs(),
            pl.BlockSpec((seq, DH_A), lambda b, h: (b, G_AZ // DH_A + h)),
            pl.BlockSpec((1, seq, LANES), lambda b, h: (b, 0, 0)),
            pl.BlockSpec((1, DH_A), lambda b, h: (0, h)),
            pl.BlockSpec((1, DH_A), lambda b, h: (0, h)),
        ],
        out_specs=pl.BlockSpec((1, seq, DH_A), lambda b, h: (b, 0, h)),
        out_shape=jax.ShapeDtypeStruct((bsz, seq, D_A), F32),
        scratch_shapes=[pltpu.VMEM((seq, DH_A), F32), pltpu.VMEM((2, DH_A, DH_A), F32),
                        pltpu.VMEM((2, 1, DH_A), F32), pltpu.VMEM((2, 1, LANES), F32),
                        pltpu.VMEM((2, grp, CHUNK_A, DH_A), F32),
                        pltpu.VMEM((2, grp, CHUNK_A, LANES), F32),
                        pltpu.VMEM((2, grp, CHUNK_A, LANES), F32),
                        pltpu.VMEM((2, grp, CHUNK_A, LANES), F32),
                        pltpu.VMEM((2, grp, DH_A, DH_A), F32),
                        pltpu.VMEM((2, grp, SUBLANES, LANES), F32),
                        pltpu.VMEM((2, 3 * LANES, 2 * LANES), BF16)],
        compiler_params=_cparams(("arbitrary", "arbitrary")),
        name="mlstm",
    )(q, k, v, xc, proj_g, gcol, norm_g.reshape(1, D_A), skip.reshape(1, D_A))


FFT_N2 = 128
HY_CT = 128
HY_PASSES = 1
FFT_ROWS_UNROLL = 8
FFT_SLAB_UNROLL = 4


def _split_bf16(x):
    hi = x.astype(BF16)
    lo = (x - hi.astype(F32)).astype(BF16)
    return hi, lo


def _dot3(a_hi, a_lo, x):
    x_hi, x_lo = _split_bf16(x)
    d = lambda a, b: jnp.dot(a, b, preferred_element_type=F32)
    if HY_PASSES == 1:
        return d(a_hi, x_hi)
    return d(a_hi, x_hi) + (d(a_lo, x_hi) + d(a_hi, x_lo))


def _dft_constants(seq):
    n = 2 * seq
    n2 = FFT_N2
    n1 = n // n2
    f1 = np.arange(n1)[:, None]
    s1 = np.arange(n1)[None, :]
    th1 = 2.0 * np.pi * f1 * s1 / n1
    fwd1 = np.concatenate([np.cos(th1), -np.sin(th1)], axis=0)
    inv1 = np.concatenate([np.cos(th1), -np.sin(th1)], axis=1)[:n1 // 2] / n
    a = np.arange(n2)
    th2 = 2.0 * np.pi * a[:, None] * a[None, :] / n2
    c2, s2 = np.cos(th2), np.sin(th2)
    fwd2 = np.block([[c2, s2], [-s2, c2]])
    inv2 = np.block([[c2, -s2], [s2, c2]])
    tht = 2.0 * np.pi * a[:, None] * np.arange(n1)[None, :] / n
    tw = np.zeros((2, n2, LANES), np.float64)
    tw[0, :, :n1] = np.cos(tht)
    tw[1, :, :n1] = -np.sin(tht)

    def hl(m):
        m32 = jnp.asarray(m, F32)
        hi = m32.astype(BF16)
        lo = (m32 - hi.astype(F32)).astype(BF16)
        return jnp.stack([hi, lo])

    nf = n1 // 2 + 1
    nfp = -(-nf // SUBLANES) * SUBLANES
    keep = np.zeros((nfp, 1))
    keep[:nf] = 1.0
    thh = 2.0 * np.pi * np.arange(nfp)[:, None] * s1 / n1
    fwd1h = np.concatenate([np.cos(thh) * keep, -np.sin(thh) * keep], axis=0)
    wgt = 2.0 * keep
    wgt[0] = wgt[nf - 1] = 1.0
    inv1h = np.concatenate([(np.cos(thh) * wgt).T, (-np.sin(thh) * wgt).T], axis=1)[:n1 // 2] / n
    return dict(fwd1=hl(fwd1), inv1=hl(inv1), fwd2=hl(fwd2), inv2=hl(inv2),
                fwd1h=hl(fwd1h), inv1h=hl(inv1h), tw=jnp.asarray(tw, F32), n1=n1)


def _fft_stage_a(load_rows, k_rows, fwd1_ref, a_ref):
    n2 = FFT_N2
    n1 = fwd1_ref.shape[1] // 2
    f_hi = fwd1_ref[0, :, :k_rows]
    f_lo = fwd1_ref[1, :, :k_rows]

    def one(s2):
        m = _dot3(f_hi, f_lo, load_rows(s2))
        yield
        a_ref[pl.ds(s2, n1, stride=2 * n2), :] = m[:n1]
        a_ref[pl.ds(n2 + s2, n1, stride=2 * n2), :] = m[n1:]

    def body(i, carry):
        _round_robin([one(i * FFT_ROWS_UNROLL + u) for u in range(FFT_ROWS_UNROLL)])
        return carry

    lax.fori_loop(0, n2 // FFT_ROWS_UNROLL, body, 0)


def _twiddle_cols(tw_ref, f1):
    lane = lax.broadcasted_iota(jnp.int32, (FFT_N2, LANES), 1)
    sel = lane == f1
    twr = jnp.sum(jnp.where(sel, tw_ref[0], 0.0), axis=-1, keepdims=True)
    twi = jnp.sum(jnp.where(sel, tw_ref[1], 0.0), axis=-1, keepdims=True)
    return twr, twi


def _hyena_filter_kernel(feat_ref, w1_ref, b1_ref, w2_ref, b2_ref, w3_ref, b3_ref, fr_ref,
                         wo_ref, dec_ref, fwd1_ref, fwd2_ref, tw_ref, o_ref, kt_ref, a_ref, hid_ref):
    n = feat_ref.shape[0]
    seq = n // 2
    n2 = FFT_N2
    n1 = n // n2
    tile = 512
    freq = fr_ref[...]

    @pl.when(pl.program_id(0) == 0)
    def _():
        def hidden(i, carry):
            r0 = pl.multiple_of(i * tile, tile)
            hid = jnp.sin(freq * (_hdot(feat_ref[pl.ds(r0, tile), :], w1_ref[...]) + b1_ref[...]))
            hid = jnp.sin(freq * (_hdot(hid, w2_ref[...]) + b2_ref[...]))
            hid_ref[pl.ds(r0, tile), :] = jnp.sin(freq * (_hdot(hid, w3_ref[...]) + b3_ref[...]))
            return carry

        lax.fori_loop(0, n // tile, hidden, 0)

    def gen(i, carry):
        r0 = pl.multiple_of(i * tile, tile)
        second = r0 >= seq
        wo = jnp.where(second, wo_ref[1, 0], wo_ref[0, 0])
        dec = jnp.where(second, dec_ref[1, 0], dec_ref[0, 0])
        t = feat_ref[pl.ds(r0, tile), 0:1]
        filt = _hdot(hid_ref[pl.ds(r0, tile), :], wo) * jnp.exp(-t * jnp.abs(dec))
        rows = r0 + lax.broadcasted_iota(jnp.int32, (tile, 1), 0)
        kt_ref[pl.ds(r0, tile), :] = jnp.where(rows == seq, 0.0, filt)
        return carry

    lax.fori_loop(0, n // tile, gen, 0)

    _fft_stage_a(lambda s2: kt_ref[pl.ds(s2, n1, stride=n2), :], n1, fwd1_ref, a_ref)

    nf = n1 // 2 + 1

    def slab(f1):
        r0 = f1 * 2 * n2 if isinstance(f1, int) else pl.multiple_of(f1 * 2 * n2, 2 * n2)
        ar = a_ref[pl.ds(r0, n2), :]
        ai = a_ref[pl.ds(r0 + n2, n2), :]
        twr, twi = _twiddle_cols(tw_ref, f1)
        x = jnp.concatenate([ar * twr - ai * twi, ar * twi + ai * twr], axis=0)
        xf = _dot3(fwd2_ref[0], fwd2_ref[1], x)
        yield
        o_ref[0, pl.ds(r0, 2 * n2), :] = xf

    def slabs(i, carry):
        _round_robin([slab(i * FFT_SLAB_UNROLL + u) for u in range(FFT_SLAB_UNROLL)])
        return carry

    lax.fori_loop(0, nf // FFT_SLAB_UNROLL, slabs, 0)
    _round_robin([slab(f1) for f1 in range(nf - nf % FFT_SLAB_UNROLL, nf)])


def _hyena_filter_spectrum(consts, feats_full, w1p, b1, w2, b2, w3, b3, freq, w_out, decay):
    n = feats_full.shape[0]
    nct = D_B // HY_CT
    spec_rows = (consts['n1'] // 2 + 1) * 2 * FFT_N2
    full = lambda a: pl.BlockSpec(a.shape, lambda j: (0,) * a.ndim)
    wo = w_out.reshape(HY_HID, 2, nct, HY_CT).transpose(1, 2, 0, 3)
    dec = decay.reshape(2, nct, 1, HY_CT)
    small = (w1p, b1.reshape(1, -1), w2, b2.reshape(1, -1), w3, b3.reshape(1, -1), freq.reshape(1, -1))
    return pl.pallas_call(
        _hyena_filter_kernel,
        grid=(nct,),
        in_specs=[full(feats_full)] + [full(a) for a in small] + [
            pl.BlockSpec((2, 1, HY_HID, HY_CT), lambda j: (0, j, 0, 0)),
            pl.BlockSpec((2, 1, 1, HY_CT), lambda j: (0, j, 0, 0)),
            full(consts['fwd1h']), full(consts['fwd2']), full(consts['tw']),
        ],
        out_specs=pl.BlockSpec((1, spec_rows, HY_CT), lambda j: (j, 0, 0)),
        out_shape=jax.ShapeDtypeStruct((nct, spec_rows, HY_CT), F32),
        scratch_shapes=[pltpu.VMEM((n, HY_CT), F32), pltpu.VMEM((2 * n, HY_CT), F32),
                        pltpu.VMEM((n, HY_HID), F32)],
        compiler_params=_cparams(("arbitrary",)),
        name="hyena_filter",
    )(feats_full, *small, wo, dec, consts['fwd1h'], consts['fwd2'], consts['tw'])


def _hyena_conv_kernel(z_ref, x0_ref, zb_ref, ks_ref, bias_ref, fwd1_ref, inv1_ref, fwd2_ref,
                       inv2_ref, tw_ref, o_ref, a_ref, y_ref):
    seq = z_ref.shape[1]
    n2 = FFT_N2
    n1 = 2 * seq // n2
    nh = n1 // 2
    nf = nh + 1
    nfp = fwd1_ref.shape[1] // 2

    _fft_stage_a(lambda s2: z_ref[0, pl.ds(s2, nh, stride=n2), :], nh, fwd1_ref, a_ref)

    def slab(f1):
        r0 = f1 * 2 * n2 if isinstance(f1, int) else pl.multiple_of(f1 * 2 * n2, 2 * n2)
        ar = a_ref[pl.ds(r0, n2), :]
        ai = a_ref[pl.ds(r0 + n2, n2), :]
        twr, twi = _twiddle_cols(tw_ref, f1)
        x = jnp.concatenate([ar * twr - ai * twi, ar * twi + ai * twr], axis=0)
        xf = _dot3(fwd2_ref[0], fwd2_ref[1], x)
        yield
        xr, xi = xf[:n2], xf[n2:]
        kr = ks_ref[0, pl.ds(r0, n2), :]
        ki = ks_ref[0, pl.ds(r0 + n2, n2), :]
        y = jnp.concatenate([xr * kr - xi * ki, xr * ki + xi * kr], axis=0)
        bb = _dot3(inv2_ref[0], inv2_ref[1], y)
        yield
        br, bi = bb[:n2], bb[n2:]
        a_ref[pl.ds(r0, n2), :] = br * twr + bi * twi
        a_ref[pl.ds(r0 + n2, n2), :] = bi * twr - br * twi

    def slabs(i, carry):
        _round_robin([slab(i * FFT_SLAB_UNROLL + u) for u in range(FFT_SLAB_UNROLL)])
        return carry

    lax.fori_loop(0, nf // FFT_SLAB_UNROLL, slabs, 0)
    _round_robin([slab(f1) for f1 in range(nf - nf % FFT_SLAB_UNROLL, nf)])

    i_hi = inv1_ref[0]
    i_lo = inv1_ref[1]

    def stage_d(s2):
        br = a_ref[pl.ds(s2, nfp, stride=2 * n2), :]
        bi = a_ref[pl.ds(n2 + s2, nfp, stride=2 * n2), :]
        y = _dot3(i_hi, i_lo, jnp.concatenate([br, bi], axis=0))
        yield
        y_ref[pl.ds(s2, nh, stride=n2), :] = y

    def stage_ds(i, carry):
        _round_robin([stage_d(i * FFT_ROWS_UNROLL + u) for u in range(FFT_ROWS_UNROLL)])
        return carry

    lax.fori_loop(0, n2 // FFT_ROWS_UNROLL, stage_ds, 0)

    tile = 256
    def finish(i, carry):
        r0 = pl.multiple_of(i * tile, tile)
        z = z_ref[0, pl.ds(r0, tile), :]
        y = y_ref[pl.ds(r0, tile), :] + bias_ref[...] * z
        o_ref[0, pl.ds(r0, tile), :] = x0_ref[0, pl.ds(r0, tile), :] * y * _silu(zb_ref[pl.ds(r0, tile), :].astype(F32))
        return carry

    lax.fori_loop(0, seq // tile, finish, 0)


def _hyena_conv(consts, z, x0, proj_g, kspec, bias):
    bsz, seq, _ = z.shape
    nct = D_B // HY_CT
    full = lambda a: pl.BlockSpec(a.shape, lambda j, b: (0,) * a.ndim)
    cs = pl.BlockSpec((1, seq, HY_CT), lambda j, b: (b, 0, j))
    mats = (consts['fwd1h'], consts['inv1h'], consts['fwd2'], consts['inv2'], consts['tw'])
    return pl.pallas_call(
        _hyena_conv_kernel,
        grid=(nct, bsz),
        in_specs=[
            cs, cs,
            pl.BlockSpec((seq, HY_CT), lambda j, b: (b, G_BZ // HY_CT + j)),
            pl.BlockSpec((1, kspec.shape[1], HY_CT), lambda j, b: (j, 0, 0)),
            pl.BlockSpec((1, HY_CT), lambda j, b: (0, j)),
        ] + [full(m) for m in mats],
        out_specs=cs,
        out_shape=jax.ShapeDtypeStruct((bsz, seq, D_B), F32),
        scratch_shapes=[pltpu.VMEM((4 * seq, HY_CT), F32), pltpu.VMEM((seq, HY_CT), F32)],
        compiler_params=_cparams(("arbitrary", "arbitrary")),
        name="hyena_conv",
    )(z, x0, proj_g, kspec, bias.reshape(1, D_B), *mats)


def _rwkv_kernel(r_ref, k_ref, v_ref, ul_ref, cz_ref, w0_ref, w2_ref, a0_ref, a2_ref,
                 kks_ref, ka_ref, rk_ref, lg_ref, lb_ref, o_ref, y_ref, kb_ref, s_ref,
                 wr_ref, u_ref, arb_ref, y0_ref, bt_ref, kv_ref, egc_ref,
                 kk_ref, lw_ref, be_ref, kd_ref):
    seq = r_ref.shape[1]
    L = CHUNK_C
    nc = seq // L
    P2 = 2 * L
    lane = lax.broadcasted_iota(jnp.int32, (L, LANES), 1)
    head0 = lane < HEAD_C
    prow = lax.broadcasted_iota(jnp.int32, (P2, P2), 0)
    pcol = lax.broadcasted_iota(jnp.int32, (P2, P2), 1)
    bd = (prow // HEAD_C) == (pcol // HEAD_C)
    blk = jnp.where(bd, 1.0, 0.0).astype(F32)
    trow = lax.broadcasted_iota(jnp.int32, (L, P2), 0)
    scol = lax.broadcasted_iota(jnp.int32, (L, P2), 1) % L
    eye = jnp.where(trow == scol, 1.0, 0.0).astype(F32)
    crow = lax.broadcasted_iota(jnp.int32, (L, L), 0)
    ccol = lax.broadcasted_iota(jnp.int32, (L, L), 1)

    def stack(x):
        return jnp.concatenate([jnp.where(head0, x, 0.0), jnp.where(head0, 0.0, x)], axis=0)

    blk_b = blk.astype(BF16)
    grp = wr_ref.shape[1]

    def dot(a, b):
        return jnp.dot(a, b, preferred_element_type=F32)

    def each(f, *cols):
        return [f(*xs) for xs in zip(*cols)]

    def prepare(probs):
        dirs = [d for _, d, _ in probs]
        fwd = [d == 0 for d in dirs]
        strict = [(scol < trow) if f else (scol > trow) for f in fwd]
        incl = [(scol <= trow) if f else (scol >= trow) for f in fwd]
        cum = [jnp.where((ccol <= crow) if f else (ccol >= crow), 1.0, 0.0).astype(BF16) for f in fwd]
        c0 = [pl.multiple_of(j * L, L) for j, _, _ in probs]
        r = [r_ref[0, pl.ds(c, L), :] for c in c0]
        v = [v_ref[0, pl.ds(c, L), :] for c in c0]
        kk = [kk_ref[pl.ds(c, L), :] for c in c0]
        lwt = [lw_ref[d, pl.ds(c, L), :] for c, d in zip(c0, dirs)]
        be = [be_ref[d, pl.ds(c, L), :] for c, d in zip(c0, dirs)]
        kd = [kd_ref[d, pl.ds(c, L), :] for c, d in zip(c0, dirs)]

        def cumsum(cm, x):
            l1 = x.astype(BF16)
            e1 = x - l1.astype(F32)
            l2 = e1.astype(BF16)
            l3 = (e1 - l2.astype(F32)).astype(BF16)
            return dot(jnp.concatenate([cm, cm, cm], axis=1), jnp.concatenate([l1, l2, l3], axis=0))

        g = each(cumsum, cum, lwt)
        yield
        eng = [jnp.exp(-x) for x in g]
        egc = [jnp.exp(x[L - 1:L, :] if f else x[0:1, :]) for x, f in zip(g, fwd)]
        at = each(lambda x, y, z: (-x * jnp.exp(y - z)).astype(BF16), kk, g, lwt)
        rt_ = each(lambda x, y: (x * jnp.exp(y)).astype(BF16), r, g)
        bh = each(lambda x, y: x * y, be, eng)
        kh = each(lambda x, y: x * y, kd, eng)
        lhs = each(lambda x, y: jnp.concatenate([x, y], axis=0), at, rt_)
        rhs = each(lambda x, y: jnp.concatenate([stack(x), stack(y)], axis=0).astype(BF16), bh, kh)
        g1 = each(lambda x, y: lax.dot_general(x, y, (((1,), (1,)), ((), ())),
                                               preferred_element_type=F32), lhs, rhs)
        yield
        aab = each(lambda m, x: jnp.where(m, x[:L, :P2], 0.0), strict, g1)
        aak_ark = each(lambda ms, mi, x: jnp.concatenate(
            [jnp.where(ms, x[:L, P2:], 0.0), jnp.where(mi, x[L:, P2:], 0.0)], axis=0).astype(BF16),
            strict, incl, g1)
        for (_, d, slot), mi, x in zip(probs, incl, g1):
            arb_ref[d, slot] = jnp.where(mi, x[L:, :P2], 0.0).astype(BF16)
        t = [eye + x for x in aab]
        p = [x.astype(BF16) for x in aab]
        vs = [stack(x).astype(BF16) for x in v]
        av = each(dot, aak_ark, vs)
        p = [dot(x, stack(x)) for x in p]
        yield
        p = [x.astype(BF16) for x in p]
        sq = 2
        while sq < L:
            if 2 * sq >= L:
                pt = each(lambda x, y: dot(x, stack(y.astype(BF16))), p, t)
                yield
                t = each(lambda y, x: y + x, t, pt)
            else:
                pt = each(lambda x, y: dot(x, jnp.concatenate([stack(x), stack(y.astype(BF16))], axis=1)),
                          p, t)
                yield
                p = [x[:, :P2].astype(BF16) for x in pt]
                t = each(lambda y, x: y + x[:, P2:], t, pt)
            sq *= 2
        wu = each(lambda x, y, z: dot(x.astype(BF16), jnp.concatenate(
            [stack(y), stack(z[:L].astype(BF16))], axis=1)), t, at, av)
        kv = each(lambda x, y, z: lax.dot_general(
            x.astype(BF16), (y * z).astype(BF16), (((0,), (0,)), ((), ())), preferred_element_type=F32),
            v, kh, egc)
        yield
        for i, (_, d, slot) in enumerate(probs):
            wr_ref[d, slot] = jnp.concatenate([wu[i][:, :P2].astype(BF16), rt_[i]], axis=0)
            u_ref[d, slot] = wu[i][:, P2:].astype(BF16)
            y0_ref[d, slot] = av[i][L:]
            bt_ref[d, slot] = (bh[i] * egc[i]).astype(BF16)
            kv_ref[d, slot] = jnp.where(bd, kv[i], 0.0).astype(BF16)
            egc_ref[d, slot] = egc[i]

    def advance(steps):
        s = [s_ref[d] for d in range(2)]
        for probs in steps:
            x = [lax.dot_general(wr_ref[d, slot], s[d].astype(BF16), (((1,), (1,)), ((), ())),
                                 preferred_element_type=F32) for _, d, slot in probs]
            yield
            sa = [(y[:L] + u_ref[d, slot].astype(F32)).astype(BF16) for (_, d, slot), y in zip(probs, x)]
            upd = [lax.dot_general(y, bt_ref[d, slot], (((0,), (0,)), ((), ())),
                                   preferred_element_type=F32) for (_, d, slot), y in zip(probs, sa)]
            z = [dot(arb_ref[d, slot], stack(y)) for (_, d, slot), y in zip(probs, sa)]
            yield
            for i, (j, d, slot) in enumerate(probs):
                s[d] = (s[d] * egc_ref[d, slot] + kv_ref[d, slot].astype(F32)
                        + jnp.where(bd, upd[i], 0.0))
                c0 = pl.multiple_of(j * L, L)
                y_ref[pl.ds(c0, L), :] += x[i][L:] + z[i] + y0_ref[d, slot]
        for d in range(2):
            s_ref[d] = s[d]

    s_ref[...] = jnp.zeros_like(s_ref)
    y_ref[...] = jnp.zeros_like(y_ref)
    tile = 256

    def pointwise(r0):
        k = k_ref[0, pl.ds(r0, tile), :]
        lwla = ul_ref[0, pl.ds(r0, tile), :]
        th = jnp.tanh(lwla[:, :LANES]).astype(BF16)
        la = lwla[:, LANES:].astype(BF16)
        xw = [w0_ref[d:d + 1, :] + dot(th, w2_ref[d].astype(BF16)) for d in range(2)]
        aa = [a0_ref[d:d + 1, :] + dot(la, a2_ref[d].astype(BF16)) for d in range(2)]
        kkv = k * kks_ref[...]
        ss = dot((kkv * kkv).astype(BF16), blk_b)
        yield
        kk = kkv * lax.rsqrt(jnp.maximum(ss, 1e-24))
        kk_ref[pl.ds(r0, tile), :] = kk
        kb = None
        for d in range(2):
            lw_ref[d, pl.ds(r0, tile), :] = -math.exp(-0.5) * _sigmoid(xw[d])
            a = _sigmoid(aa[d])
            kd = k * (1.0 + (a - 1.0) * ka_ref[...])
            be_ref[d, pl.ds(r0, tile), :] = kk * a
            kd_ref[d, pl.ds(r0, tile), :] = kd
            kb = kd if kb is None else kb + kd
        kb_ref[pl.ds(r0, tile), :] = kb

    pw_unroll = 2 if (seq // tile) % 2 == 0 else 1

    def pointwise_loop(i, carry):
        _round_robin([pointwise(pl.multiple_of((i * pw_unroll + u) * tile, tile))
                      for u in range(pw_unroll)])
        return carry

    lax.fori_loop(0, seq // tile // pw_unroll, pointwise_loop, 0)

    lock = grp // 2
    nsets = nc // lock

    def chunk_set(i):
        base = (i % 2) * lock
        return [[(i * lock + q, 0, base + q), (nc - 1 - (i * lock + q), 1, base + q)]
                for q in range(lock)]

    _round_robin([prepare(sum(chunk_set(0), []))])

    def pipelined(i, carry):
        _round_robin([prepare(sum(chunk_set(i + 1), [])), advance(chunk_set(i))])
        return carry

    lax.fori_loop(0, nsets - 1, pipelined, 0)
    _round_robin([advance(chunk_set(nsets - 1))])

    tile = 256
    inv_n = 1.0 / HEAD_C
    blk2 = jnp.concatenate([blk_b, blk_b], axis=0)

    def head_sum(x):
        hi, lo = _split_bf16(x)
        return dot(jnp.concatenate([hi, lo], axis=1), blk2)

    def finish(r0):
        y = y_ref[pl.ds(r0, tile), :]
        mu = head_sum(y) * inv_n
        r = r_ref[0, pl.ds(r0, tile), :]
        v = v_ref[0, pl.ds(r0, tile), :]
        kbon = 0.5 * kb_ref[pl.ds(r0, tile), :]
        bonus = head_sum(r * kbon * rk_ref[...]) * v
        yield
        yc = y - mu
        var = head_sum(yc * yc) * inv_n
        yield
        yn = yc * lax.rsqrt(var + LNX_EPS) * lg_ref[...] + lb_ref[...]
        o_ref[0, pl.ds(r0, tile), :] = (yn + bonus) * _silu(cz_ref[pl.ds(r0, tile), :].astype(F32))

    unroll = 4 if (seq // tile) % 4 == 0 else 1

    def finishes(i, carry):
        _round_robin([finish(pl.multiple_of((i * unroll + u) * tile, tile)) for u in range(unroll)])
        return carry

    lax.fori_loop(0, seq // tile // unroll, finishes, 0)


def _rwkv(u, ul, proj_g, w0, w2pad, a0, a2pad, kk_s, ka, rk, lnx_g, lnx_b):
    bsz, seq, _ = u.shape
    npair = D_C // LANES
    grp = min(RW_GROUP, seq // CHUNK_C)
    col = lambda off: pl.BlockSpec((1, seq, LANES), lambda b, p: (b, 0, off + p))
    vec = pl.BlockSpec((1, LANES), lambda b, p: (0, p))
    two = pl.BlockSpec((2, LANES), lambda b, p: (0, p))
    lora = pl.BlockSpec((2, LANES, LANES), lambda b, p: (0, 0, p))
    row = lambda a: a.reshape(1, D_C)
    return pl.pallas_call(
        _rwkv_kernel,
        grid=(bsz, npair),
        in_specs=[
            col(0), col(npair), col(2 * npair),
            pl.BlockSpec((1, seq, 2 * LANES), lambda b, p: (b, 0, 0)),
            pl.BlockSpec((seq, LANES), lambda b, p: (b, G_CZ // LANES + p)),
            two, lora, two, lora, vec, vec, vec, vec, vec,
        ],
        out_specs=pl.BlockSpec((1, seq, LANES), lambda b, p: (b, 0, p)),
        out_shape=jax.ShapeDtypeStruct((bsz, seq, D_C), F32),
        scratch_shapes=[pltpu.VMEM((seq, LANES), F32), pltpu.VMEM((seq, LANES), F32),
                        pltpu.VMEM((2, LANES, LANES), F32),
                        pltpu.VMEM((2, grp, 2 * CHUNK_C, LANES), BF16),
                        pltpu.VMEM((2, grp, CHUNK_C, LANES), BF16),
                        pltpu.VMEM((2, grp, CHUNK_C, LANES), BF16),
                        pltpu.VMEM((2, grp, CHUNK_C, LANES), F32),
                        pltpu.VMEM((2, grp, CHUNK_C, LANES), BF16),
                        pltpu.VMEM((2, grp, LANES, LANES), BF16),
                        pltpu.VMEM((2, grp, 1, LANES), F32),
                        pltpu.VMEM((seq, LANES), F32), pltpu.VMEM((2, seq, LANES), F32),
                        pltpu.VMEM((2, seq, LANES), F32), pltpu.VMEM((2, seq, LANES), F32)],
        compiler_params=_cparams(("arbitrary", "arbitrary")),
        name="rwkv7",
    )(u, u, u, ul, proj_g, w0, w2pad, a0, a2pad, row(kk_s), row(ka), row(rk), row(lnx_g), row(lnx_b))


def _out_kernel(ya_ref, yb_ref, yc_ref, ga_ref, gb_ref, gc_ref, x_ref, gate_ref, pg_ref,
                wa_ref, wb_ref, wc_ref, wo_ref, o_ref):
    merged = (_sigmoid(ga_ref[...].astype(F32)) * _bdot(ya_ref[...], wa_ref[...])
              + _sigmoid(gb_ref[...].astype(F32)) * _bdot(yb_ref[...], wb_ref[...])
              + _sigmoid(gc_ref[...].astype(F32)) * _bdot(yc_ref[...], wc_ref[...]))
    out = _bdot(merged, wo_ref[...])
    y = out * lax.rsqrt(jnp.mean(out * out, axis=-1, keepdims=True) + NORM_EPS) * pg_ref[...]
    o_ref[...] = x_ref[...] + gate_ref[0] * y


def _merge_out(ya, yb, yc, proj_g, x2d, seq, gate, post_g, wa, wb, wc, wo, tm):
    rows, d = x2d.shape
    per_b = seq // tm
    full = lambda a: pl.BlockSpec(a.shape, lambda i: (0,) * a.ndim)
    gcol = lambda off: pl.BlockSpec((tm, d), lambda i: (i, off // d))
    return pl.pallas_call(
        _out_kernel,
        grid=(rows // tm,),
        in_specs=[
            pl.BlockSpec((tm, D_A), lambda i: (i, 0)),
            pl.BlockSpec((tm, D_B), lambda i: (i, 0)),
            pl.BlockSpec((tm, D_C), lambda i: (i, 0)),
            gcol(G_GA), gcol(G_GB), gcol(G_GC),
            pl.BlockSpec((tm, d), lambda i: (i, 0)),
            pl.BlockSpec((1, 1, d), lambda i: (i // per_b, 0, 0)),
            pl.BlockSpec((1, d), lambda i: (0, 0)),
            full(wa), full(wb), full(wc), full(wo),
        ],
        out_specs=pl.BlockSpec((tm, d), lambda i: (i, 0)),
        out_shape=jax.ShapeDtypeStruct((rows, d), F32),
        compiler_params=_cparams(("arbitrary",)),
        name="merge_out",
    )(ya.reshape(rows, D_A), yb.reshape(rows, D_B), yc.reshape(rows, D_C), proj_g, proj_g, proj_g,
      x2d, gate, post_g.reshape(1, d), wa, wb, wc, wo)


def _hyena_feats(seq):
    n = 2 * seq
    t = jnp.linspace(0.0, 1.0, seq, dtype=F32)
    w = 2.0 * math.pi * jnp.arange(seq, dtype=F32) / seq
    f = jnp.linspace(1e-4, HY_BANDS - 1, HY_BANDS, dtype=F32)
    zz = w[:, None] * f[None, :]
    feats = jnp.concatenate([t[:, None], jnp.cos(zz), -jnp.sin(zz)], axis=-1)
    pos = np.concatenate([np.arange(seq), [0], np.arange(seq - 1, 0, -1)])
    full = jnp.zeros((n, LANES), F32).at[:, :feats.shape[1]].set(feats[pos])
    return full


def _tiles(seq):
    return min(seq, 1024), min(seq, 256), min(seq, 512)


def kernel(x, c, ada_w, ada_b, pre_g, post_g, w_in, ml_conv_w, ml_conv_b, ml_wq, ml_wk, ml_wv, ml_gate_w, ml_gate_b, ml_norm_g, ml_skip, hy_conv_w, hy_conv_b, hy_w1, hy_b1, hy_w2, hy_b2, hy_w3, hy_b3, hy_freq, hy_w_out, hy_decay, hy_bias, rw_mu, rw_w0, rw_w2, rw_a0, rw_a2, rw_kk, rw_ka, rw_rk, rw_lnx_g, rw_lnx_b, w_branch_a, w_branch_b, w_branch_c, w_out):
    bsz, seq, d = x.shape
    depth = ada_w.shape[0]
    tm_in, ts_prep, tm_out = _tiles(seq)
    runs_h, runs_g = (_contiguous_runs(p) for p in _ref_column_perm())
    consts = _dft_constants(seq)
    feats_full = _hyena_feats(seq)
    mod = _modulation(c, ada_w, ada_b)
    x2d = x.reshape(bsz * seq, d)
    n_gate = 4 * H_A
    for l in range(depth):
        shift = mod[l, :, None, 0:d]
        scale = mod[l, :, None, d:2 * d]
        gate = mod[l, :, None, 2 * d:3 * d]
        w_h = jnp.concatenate([w_in[l][:, a:b] for a, b in runs_h], axis=1).astype(BF16)
        w_g = jnp.concatenate([w_in[l][:, a:b] for a, b in runs_g], axis=1).astype(BF16)
        proj_h = _inproj(x2d, seq, pre_g[l], shift, scale, w_h, tm_in, N_H // 3)
        proj_g = _inproj(x2d, seq, pre_g[l], shift, scale, w_g, tm_in, N_G // 4)

        gw = jnp.zeros((3 * D_A, LANES), F32).at[:, :n_gate].set(ml_gate_w[l])
        gb = jnp.zeros((1, LANES), F32).at[0, :n_gate].set(ml_gate_b[l])
        q, k, v, xc, gcol, z, x0, u, ul = _prep(
            proj_h, bsz, seq, ts_prep, ml_conv_w[l], ml_conv_b[l].reshape(1, D_A),
            ml_wq[l], ml_wk[l], ml_wv[l], gw, gb,
            hy_conv_w[l], hy_conv_b[l].reshape(1, 3 * D_B), rw_mu[l].reshape(1, -1))

        y_a = _mlstm(q, k, v, xc, proj_g, gcol, ml_norm_g[l], ml_skip[l])

        w1p = jnp.zeros((LANES, HY_HID), F32).at[:hy_w1.shape[1]].set(hy_w1[l])
        kspec = _hyena_filter_spectrum(consts, feats_full, w1p, hy_b1[l], hy_w2[l], hy_b2[l],
                                       hy_w3[l], hy_b3[l], hy_freq[l], hy_w_out[l], hy_decay[l])
        y_b = _hyena_conv(consts, z, x0, proj_g, kspec, hy_bias[l])

        w2pad = (jnp.zeros((2, LANES, D_C), F32).at[0, :LORA].set(rw_w2[l, 0])
                 .at[1, LORA:].set(rw_w2[l, 1]))
        a2pad = (jnp.zeros((2, LANES, D_C), F32).at[0, :LORA].set(rw_a2[l, 0])
                 .at[1, LORA:].set(rw_a2[l, 1]))
        y_c = _rwkv(u, ul, proj_g, rw_w0[l], w2pad, rw_a0[l], a2pad, rw_kk[l], rw_ka[l],
                    rw_rk[l].reshape(-1), rw_lnx_g[l], rw_lnx_b[l])

        x2d = _merge_out(y_a, y_b, y_c, proj_g, x2d, seq, gate, post_g[l],
                         w_branch_a[l].astype(BF16), w_branch_b[l].astype(BF16),
                         w_branch_c[l].astype(BF16), w_out[l].astype(BF16), tm_out)
    return x2d.reshape(bsz, seq, d)
```

```python
import functools
import math

import numpy as np
import jax
import jax.numpy as jnp
from jax import lax
from jax.experimental import pallas as pl
from jax.experimental.pallas import tpu as pltpu

D_MODEL = 1024
DEPTH = 4
D_A = 512
H_A = 4
DH_A = 128
CHUNK_A = 64
ML_GROUP = 16
D_B = 512
HY_BANDS = 16
HY_HID = 64
D_C = 1024
HEAD_C = 64
H_C = D_C // HEAD_C
LORA = 64
CHUNK_C = 64
RW_GROUP = 8
LNX_EPS = 64e-5
NORM_EPS = 1e-6
HEAD_NORM_EPS = 1e-5

LANES = 128
SUBLANES = 8
VMEM_LIMIT = 56 * 1024 * 1024

F32 = jnp.float32
BF16 = jnp.bfloat16
PROJ_DTYPE = BF16
PROJ_ROWS = 16
HIGHEST = lax.Precision.HIGHEST

N_H = D_A + 3 * D_B + 3 * D_C + 4 * LORA
N_G = D_A + D_B + D_C + 3 * D_MODEL
H_AX, H_BV, H_BX0, H_BX1, H_CR, H_CK, H_CV, H_LW, H_LA = (
    0, 512, 1024, 1536, 2048, 3072, 4096, 5120, 5248)
G_AZ, G_BZ, G_CZ, G_GA, G_GB, G_GC = 0, 512, 1024, 2048, 3072, 4096


def _ref_column_perm():
    r = lambda a, n: np.arange(a, a + n)
    a_x, a_z = r(0, 512), r(512, 512)
    b_v, b_x0, b_x1, b_z = r(1024, 512), r(1536, 512), r(2048, 512), r(2560, 512)
    c_r, c_k, c_v = r(3072, 1024), r(4096, 1024), r(5120, 1024)
    c_lw, c_la = r(6144, 128), r(6272, 128)
    c_z = r(6400, 1024)
    g_a, g_b, g_c = r(7424, 1024), r(8448, 1024), r(9472, 1024)
    h = np.concatenate([a_x, b_v, b_x0, b_x1, c_r, c_k, c_v, c_lw, c_la])
    g = np.concatenate([a_z, b_z, c_z, g_a, g_b, g_c])
    return h, g


def _contiguous_runs(idx):
    cuts = np.flatnonzero(np.diff(idx) != 1) + 1
    return [(int(seg[0]), int(seg[-1]) + 1) for seg in np.split(idx, cuts)]


def _cparams(sem, **extra):
    return pltpu.CompilerParams(dimension_semantics=sem, vmem_limit_bytes=VMEM_LIMIT, **extra)


def _silu(x):
    return x * (1.0 / (1.0 + jnp.exp(-x)))


def _sigmoid(x):
    return 1.0 / (1.0 + jnp.exp(-x))


def _bdot(a, b):
    return jnp.dot(a.astype(BF16), b.astype(BF16), preferred_element_type=F32)


def _bdot_nt(a, b):
    return lax.dot_general(a.astype(BF16), b.astype(BF16), (((1,), (1,)), ((), ())),
                           preferred_element_type=F32)


def _round_robin(gens):
    gens = list(gens)
    while gens:
        alive = []
        for g in gens:
            try:
                next(g)
                alive.append(g)
            except StopIteration:
                pass
        gens = alive


def _hdot(a, b):
    return jnp.dot(a, b, preferred_element_type=F32, precision=HIGHEST)


def _mod_kernel(c_ref, w_ref, b_ref, o_ref):
    cond = _silu(c_ref[...])
    o_ref[0] = _hdot(cond, w_ref[0]) + b_ref[0]


def _modulation(c, ada_w, ada_b):
    depth, d, n3 = ada_w.shape
    bsz = c.shape[0]
    nt = n3 // d
    return pl.pallas_call(
        _mod_kernel,
        grid=(depth, nt),
        in_specs=[
            pl.BlockSpec((bsz, d), lambda l, j: (0, 0)),
            pl.BlockSpec((1, d, d), lambda l, j: (l, 0, j)),
            pl.BlockSpec((1, 1, d), lambda l, j: (l, 0, j)),
        ],
        out_specs=pl.BlockSpec((1, bsz, d), lambda l, j: (l, 0, j)),
        out_shape=jax.ShapeDtypeStruct((depth, bsz, n3), F32),
        compiler_params=_cparams(("arbitrary", "arbitrary")),
        name="adaln_mod",
    )(c, ada_w, ada_b.reshape(depth, 1, n3))


def _inproj_kernel(x_ref, g_ref, shift_ref, scale_ref, w_ref, o_ref, h_ref):
    @pl.when(pl.program_id(1) == 0)
    def _():
        x = x_ref[...]
        y = x * lax.rsqrt(jnp.mean(x * x, axis=-1, keepdims=True) + NORM_EPS)
        h = y * g_ref[...] * (1.0 + scale_ref[0]) + shift_ref[0]
        h_ref[...] = h.astype(BF16)

    o_ref[...] = jnp.dot(h_ref[...], w_ref[...], preferred_element_type=F32).astype(o_ref.dtype)


def _inproj(x2d, seq, pre_g, shift, scale, w_bf16, tm, tn):
    rows, d = x2d.shape
    n = w_bf16.shape[1]
    per_b = seq // tm
    return pl.pallas_call(
        _inproj_kernel,
        grid=(rows // tm, n // tn),
        in_specs=[
            pl.BlockSpec((tm, d), lambda i, j: (i, 0)),
            pl.BlockSpec((1, d), lambda i, j: (0, 0)),
            pl.BlockSpec((1, 1, d), lambda i, j: (i // per_b, 0, 0)),
            pl.BlockSpec((1, 1, d), lambda i, j: (i // per_b, 0, 0)),
            pl.BlockSpec((d, tn), lambda i, j: (0, j)),
        ],
        out_specs=pl.BlockSpec((tm, tn), lambda i, j: (i, j)),
        out_shape=jax.ShapeDtypeStruct((rows, n), PROJ_DTYPE),
        scratch_shapes=[pltpu.VMEM((tm, d), BF16)],
        compiler_params=_cparams(("arbitrary", "arbitrary")),
        name="inproj",
    )(x2d, pre_g.reshape(1, d), shift, scale, w_bf16)


def _log_sigmoid(x):
    return jnp.minimum(x, 0.0) - jnp.log(1.0 + jnp.exp(-jnp.abs(x)))


def _prep_kernel(main_ref, prev_ref, next_ref,
                 mcw_ref, mcb_ref, wq_ref, wk_ref, wv_ref, gw_ref, gb_ref,
                 hcw_ref, hcb_ref, mu_ref,
                 q_ref, k_ref, v_ref, xc_ref, gcol_ref, z_ref, x0_ref, u_ref, ul_ref):
    i = pl.program_id(1)
    ts = main_ref.shape[0]
    has_prev = jnp.where(i > 0, 1.0, 0.0).astype(F32)
    has_next = jnp.where(i < pl.num_programs(1) - 1, 1.0, 0.0).astype(F32)

    def neighbours(c0, cw):
        x = main_ref[:, c0:c0 + cw].astype(F32)
        row = lax.broadcasted_iota(jnp.int32, (SUBLANES, cw), 0)
        p_row = prev_ref[:, c0:c0 + cw].astype(F32)[PROJ_ROWS - 1:PROJ_ROWS] * has_prev
        n_row = next_ref[:, c0:c0 + cw].astype(F32)[0:1] * has_next
        xp = pltpu.roll(x, 1, 0)
        xn = pltpu.roll(x, ts - 1, 0)
        xp = jnp.concatenate([jnp.where(row == 0, p_row, xp[:SUBLANES]), xp[SUBLANES:]], axis=0)
        xn = jnp.concatenate([xn[:ts - SUBLANES],
                              jnp.where(row == SUBLANES - 1, n_row, xn[ts - SUBLANES:])], axis=0)
        return xp, x, xn

    def conv(c0, cw, w_ref, b_ref, w0):
        xp, x, xn = neighbours(c0, cw)
        w = w_ref[:, w0:w0 + cw]
        return xp * w[0:1] + x * w[1:2] + xn * w[2:3] + b_ref[:, w0:w0 + cw], x

    gates = jnp.zeros((ts, LANES), F32) + gb_ref[...]
    for h in range(H_A):
        sl = slice(h * DH_A, (h + 1) * DH_A)
        conv_a, xa = conv(H_AX + h * DH_A, DH_A, mcw_ref, mcb_ref, h * DH_A)
        xc = _silu(conv_a)
        xc_ref[0, :, sl] = xc
        qh = _bdot(xc, wq_ref[h])
        kh = _bdot(xc, wk_ref[h])
        vh = _bdot(xa, wv_ref[h])
        gates += (_bdot(qh, gw_ref[h * DH_A:(h + 1) * DH_A])
                  + _bdot(kh, gw_ref[D_A + h * DH_A:D_A + (h + 1) * DH_A])
                  + _bdot(vh, gw_ref[2 * D_A + h * DH_A:2 * D_A + (h + 1) * DH_A]))
        q_ref[0, :, sl] = qh.astype(BF16)
        k_ref[0, :, sl] = (kh * (DH_A ** -0.5)).astype(BF16)
        v_ref[0, :, sl] = vh.astype(BF16)
    col = lax.broadcasted_iota(jnp.int32, (ts, LANES), 1)
    rmod = lax.broadcasted_iota(jnp.int32, (ts, LANES), 0) % CHUNK_A
    lf = _log_sigmoid(gates)
    cf = lf
    cb = lf
    sh = 1
    while sh < CHUNK_A:
        cf = cf + jnp.where(rmod >= sh, pltpu.roll(cf, sh, 0), 0.0)
        cb = cb + jnp.where(rmod < CHUNK_A - sh, pltpu.roll(cb, ts - sh, 0), 0.0)
        sh *= 2
    is_ff = (col >= H_A) & (col < 2 * H_A)
    is_fb = (col >= 3 * H_A) & (col < 4 * H_A)
    gcol_ref[0] = jnp.where(is_ff, cf, jnp.where(is_fb, cb, gates))

    cw = LANES
    for j in range(D_B // cw):
        c = j * cw
        cv, _ = conv(H_BV + c, cw, hcw_ref, hcb_ref, c)
        cx1, _ = conv(H_BX1 + c, cw, hcw_ref, hcb_ref, 2 * D_B + c)
        z_ref[0, :, c:c + cw] = cv * cx1
        cx0, _ = conv(H_BX0 + c, cw, hcw_ref, hcb_ref, D_B + c)
        x0_ref[0, :, c:c + cw] = cx0

    def shifted(c):
        xp, x, xn = neighbours(H_CR + c, cw)
        mu = mu_ref[:, c:c + cw]
        return x * (1.0 - mu) + (0.5 * mu) * (xp + xn)

    for j in range(3 * D_C // cw):
        u_ref[0, :, j * cw:(j + 1) * cw] = shifted(j * cw)
    for j in range(4 * LORA // cw):
        ul_ref[0, :, j * cw:(j + 1) * cw] = shifted(3 * D_C + j * cw)


def _prep(proj_h, bsz, seq, ts, ml_conv_w, ml_conv_b, wq, wk, wv, gate_w_pad, gate_b_pad,
          hy_conv_w, hy_conv_b, rw_mu):
    ns = seq // ts
    hb = ts // PROJ_ROWS
    last_hb = bsz * seq // PROJ_ROWS - 1
    full = lambda a: pl.BlockSpec(a.shape, lambda b, i: (0,) * a.ndim)
    seq_spec = lambda w: pl.BlockSpec((1, ts, w), lambda b, i: (b, i, 0))
    params = (ml_conv_w, ml_conv_b, wq, wk, wv, gate_w_pad, gate_b_pad, hy_conv_w, hy_conv_b, rw_mu)
    outs = [(D_A, BF16), (D_A, BF16), (D_A, BF16), (D_A, F32), (LANES, F32),
            (D_B, F32), (D_B, F32), (3 * D_C, F32), (4 * LORA, F32)]
    return pl.pallas_call(
        _prep_kernel,
        grid=(bsz, ns),
        in_specs=[
            pl.BlockSpec((ts, N_H), lambda b, i: (b * ns + i, 0)),
            pl.BlockSpec((PROJ_ROWS, N_H), lambda b, i: (jnp.maximum((b * ns + i) * hb - 1, 0), 0)),
            pl.BlockSpec((PROJ_ROWS, N_H), lambda b, i: (jnp.minimum((b * ns + i + 1) * hb, last_hb), 0)),
        ] + [full(p) for p in params],
        out_specs=[seq_spec(w) for w, _ in outs],
        out_shape=[jax.ShapeDtypeStruct((bsz, seq, w), dt) for w, dt in outs],
        compiler_params=_cparams(("arbitrary", "arbitrary")),
        name="prep",
    )(proj_h, proj_h, proj_h, *params)


def _mlstm_kernel(q_ref, k_ref, v_ref, xc_ref, za_ref, gcol_ref, ng_ref, sk_ref, o_ref,
                  h_ref, ct_ref, n_ref, m_ref, num_ref, den_ref, bb_ref, mb_ref, kv_ref, sc_ref,
                  sel_ref):
    head = pl.program_id(1)
    seq = q_ref.shape[1]
    L = CHUNK_A
    nc = seq // L
    row = lax.broadcasted_iota(jnp.int32, (L, L), 0)
    colm = lax.broadcasted_iota(jnp.int32, (L, L), 1)
    lane = lax.broadcasted_iota(jnp.int32, (L, LANES), 1)

    grp = num_ref.shape[1]
    srow = lax.broadcasted_iota(jnp.int32, (SUBLANES, LANES), 0)
    ones = jnp.ones((LANES, LANES), BF16)
    ones2 = jnp.ones((2 * LANES, LANES), BF16)

    def dot(a, b):
        return jnp.dot(a, b, preferred_element_type=F32)

    def split3(x):
        x1 = x.astype(BF16)
        e1 = x - x1.astype(F32)
        x2 = e1.astype(BF16)
        return x1, x2, (e1 - x2.astype(F32)).astype(BF16)

    srow_i = lax.broadcasted_iota(jnp.int32, (3 * LANES, 2 * LANES), 0) % LANES
    scol_i = lax.broadcasted_iota(jnp.int32, (3 * LANES, 2 * LANES), 1)
    for d in range(2):
        li_lane = head + 2 * H_A * d
        want = jnp.where(scol_i < LANES, li_lane + H_A, li_lane)
        sel_ref[d] = jnp.where(srow_i == want, 1.0, 0.0).astype(BF16)

    def prepare(j, d, slot):
        fwd = d == 0
        tri = (colm <= row) if fwd else (colm >= row)
        c0 = pl.multiple_of(j * L, L)
        q = q_ref[0, pl.ds(c0, L), :]
        k = k_ref[0, pl.ds(c0, L), :]
        v = v_ref[0, pl.ds(c0, L), :]
        qk = _bdot_nt(q, k)
        bl = dot(jnp.concatenate(split3(gcol_ref[0, pl.ds(c0, L), :]), axis=1), sel_ref[d])
        yield
        b, li = bl[:, :LANES], bl[:, LANES:]
        x0, x1, x2 = split3(b)
        y0, y1, y2 = split3(li - b)
        xl = jnp.where(lane == 0, x0, jnp.where(lane == 1, x1, jnp.where(
            lane == 2, x2, jnp.where(lane < 6, 1.0, 0.0).astype(BF16))))
        yl = jnp.where(lane == 3, y0, jnp.where(lane == 4, y1, jnp.where(
            lane == 5, y2, jnp.where(lane < 3, 1.0, 0.0).astype(BF16))))
        dm = _bdot_nt(xl, yl)
        gtot = b[L - 1:L, :] if fwd else b[0:1, :]
        a = gtot - b + li
        a_max = jnp.max(a, axis=0, keepdims=True)
        wk = jnp.exp(a - a_max)
        vw = (v.astype(F32) * wk).astype(BF16)
        kv = lax.dot_general(k, vw, (((0,), (0,)), ((), ())), preferred_element_type=F32)
        kn = jnp.sum(k.astype(F32) * wk, axis=0, keepdims=True)
        sc_ref[d, slot] = jnp.where(srow == 0, kn, jnp.where(srow == 1, gtot, a_max))
        bb_ref[d, slot] = b
        yield
        dm = jnp.where(tri, dm, -jnp.inf)
        m_loc = jnp.max(dm, axis=-1, keepdims=True)
        s = qk * jnp.exp(dm - m_loc)
        num = _bdot(s, v)
        den = dot(jnp.concatenate(_split_bf16(s), axis=1), ones)
        mb_ref[d, slot] = jnp.broadcast_to(m_loc, (L, LANES))
        kv_ref[d, slot] = kv
        yield
        num_ref[d, slot] = num
        den_ref[d, slot] = den

    def advance(d, todo):
        ct, n, m = ct_ref[d], n_ref[d], m_ref[d]
        pending = []
        for j, slot in todo:
            c0 = pl.multiple_of(j * L, L)
            q = q_ref[0, pl.ds(c0, L), :]
            qc = _bdot(q, ct)
            qn = dot(jnp.concatenate(_split_bf16(q.astype(F32) * n), axis=1), ones2)
            sc = sc_ref[d, slot]
            kn, gtot, a_max = sc[0:1, :], sc[1:2, :], sc[2:3, :]
            m_new = jnp.maximum(gtot + m, a_max)
            decay = jnp.exp(gtot + m - m_new)
            beta = jnp.exp(a_max - m_new)
            pending.append((c0, slot, qc, qn, m))
            ct = decay * ct + beta * kv_ref[d, slot]
            n = decay * n + beta * kn
            m = m_new
        ct_ref[d], n_ref[d], m_ref[d] = ct, n, m
        yield
        for c0, slot, qc, qn, m_in in pending:
            inter = bb_ref[d, slot] + m_in
            m_loc = mb_ref[d, slot]
            m_t = jnp.maximum(inter, m_loc)
            c_intra = jnp.exp(m_loc - m_t)
            c_inter = jnp.exp(inter - m_t)
            num = c_intra * num_ref[d, slot] + c_inter * qc
            den = c_intra * den_ref[d, slot] + c_inter * qn
            h_ref[pl.ds(c0, L), :] += num / jnp.maximum(jnp.abs(den), jnp.exp(-m_t))

    ct_ref[...] = jnp.zeros_like(ct_ref)
    n_ref[...] = jnp.zeros_like(n_ref)
    m_ref[...] = jnp.full_like(m_ref, -jnp.inf)
    h_ref[...] = jnp.zeros_like(h_ref)
    lock = 8 if grp % 8 == 0 else 1

    def group(gi, carry):
        def phase_a(i, c):
            gens = []
            for u in range(lock):
                jf = gi * grp + i * lock + u
                gens += [prepare(jf, 0, i * lock + u), prepare(nc - 1 - jf, 1, i * lock + u)]
            _round_robin(gens)
            return c

        def phase_b(i, c):
            jf = [gi * grp + i * lock + u for u in range(lock)]
            _round_robin([advance(0, [(j, i * lock + u) for u, j in enumerate(jf)]),
                          advance(1, [(nc - 1 - j, i * lock + u) for u, j in enumerate(jf)])])
            return c

        lax.fori_loop(0, grp // lock, phase_a, 0)
        lax.fori_loop(0, grp // lock, phase_b, 0)
        return carry

    lax.fori_loop(0, nc // grp, group, 0)

    tile = 256
    inv_n = 1.0 / DH_A

    def lane_sum(x):
        return dot(jnp.concatenate(_split_bf16(x), axis=1), ones2)

    def finish(r0):
        hh = h_ref[pl.ds(r0, tile), :]
        mu = lane_sum(hh) * inv_n
        yield
        hc = hh - mu
        var = lane_sum(hc * hc) * inv_n
        yield
        hn = hc * lax.rsqrt(var + HEAD_NORM_EPS) * ng_ref[...]
        out = (hn + sk_ref[...] * xc_ref[0, pl.ds(r0, tile), :]) * _silu(za_ref[pl.ds(r0, tile), :].astype(F32))
        o_ref[0, pl.ds(r0, tile), :] = out

    unroll = 4 if (seq // tile) % 4 == 0 else 1

    def finishes(i, carry):
        _round_robin([finish(pl.multiple_of((i * unroll + u) * tile, tile)) for u in range(unroll)])
        return carry

    lax.fori_loop(0, seq // tile // unroll, finishes, 0)


def _mlstm(q, k, v, xc, proj_g, gcol, norm_g, skip):
    bsz, seq, _ = q.shape
    grp = min(ML_GROUP, seq // CHUNK_A)
    hs = lambda: pl.BlockSpec((1, seq, DH_A), lambda b, h: (b, 0, h))
    return pl.pallas_call(
        _mlstm_kernel,
        grid=(bsz, H_A),
        in_specs=[
            hs(), hs(), hs(), hs(),
            pl.BlockSpec((seq, DH_A), lambda b, h: (b, G_AZ // DH_A + h)),
            pl.BlockSpec((1, seq, LANES), lambda b, h: (b, 0, 0)),
            pl.BlockSpec((1, DH_A), lambda b, h: (0, h)),
            pl.BlockSpec((1, DH_A), lambda b, h: (0, h)),
        ],
        out_specs=pl.BlockSpec((1, seq, DH_A), lambda b, h: (b, 0, h)),
        out_shape=jax.ShapeDtypeStruct((bsz, seq, D_A), F32),
        scratch_shapes=[pltpu.VMEM((seq, DH_A), F32), pltpu.VMEM((2, DH_A, DH_A), F32),
                        pltpu.VMEM((2, 1, DH_A), F32), pltpu.VMEM((2, 1, LANES), F32),
                        pltpu.VMEM((2, grp, CHUNK_A, DH_A), F32),
                        pltpu.VMEM((2, grp, CHUNK_A, LANES), F32),
                        pltpu.VMEM((2, grp, CHUNK_A, LANES), F32),
                        pltpu.VMEM((2, grp, CHUNK_A, LANES), F32),
                        pltpu.VMEM((2, grp, DH_A, DH_A), F32),
                        pltpu.VMEM((2, grp, SUBLANES, LANES), F32),
                        pltpu.VMEM((2, 3 * LANES, 2 * LANES), BF16)],
        compiler_params=_cparams(("arbitrary", "arbitrary")),
        name="mlstm",
    )(q, k, v, xc, proj_g, gcol, norm_g.reshape(1, D_A), skip.reshape(1, D_A))


FFT_N2 = 128
HY_CT = 128
HY_PASSES = 1
FFT_ROWS_UNROLL = 8
FFT_SLAB_UNROLL = 4


def _split_bf16(x):
    hi = x.astype(BF16)
    lo = (x - hi.astype(F32)).astype(BF16)
    return hi, lo


def _dot3(a_hi, a_lo, x):
    x_hi, x_lo = _split_bf16(x)
    d = lambda a, b: jnp.dot(a, b, preferred_element_type=F32)
    if HY_PASSES == 1:
        return d(a_hi, x_hi)
    return d(a_hi, x_hi) + (d(a_lo, x_hi) + d(a_hi, x_lo))


def _dft_constants(seq):
    n = 2 * seq
    n2 = FFT_N2
    n1 = n // n2
    f1 = np.arange(n1)[:, None]
    s1 = np.arange(n1)[None, :]
    th1 = 2.0 * np.pi * f1 * s1 / n1
    fwd1 = np.concatenate([np.cos(th1), -np.sin(th1)], axis=0)
    inv1 = np.concatenate([np.cos(th1), -np.sin(th1)], axis=1)[:n1 // 2] / n
    a = np.arange(n2)
    th2 = 2.0 * np.pi * a[:, None] * a[None, :] / n2
    c2, s2 = np.cos(th2), np.sin(th2)
    fwd2 = np.block([[c2, s2], [-s2, c2]])
    inv2 = np.block([[c2, -s2], [s2, c2]])
    tht = 2.0 * np.pi * a[:, None] * np.arange(n1)[None, :] / n
    tw = np.zeros((2, n2, LANES), np.float64)
    tw[0, :, :n1] = np.cos(tht)
    tw[1, :, :n1] = -np.sin(tht)

    def hl(m):
        m32 = jnp.asarray(m, F32)
        hi = m32.astype(BF16)
        lo = (m32 - hi.astype(F32)).astype(BF16)
        return jnp.stack([hi, lo])

    nf = n1 // 2 + 1
    nfp = -(-nf // SUBLANES) * SUBLANES
    keep = np.zeros((nfp, 1))
    keep[:nf] = 1.0
    thh = 2.0 * np.pi * np.arange(nfp)[:, None] * s1 / n1
    fwd1h = np.concatenate([np.cos(thh) * keep, -np.sin(thh) * keep], axis=0)
    wgt = 2.0 * keep
    wgt[0] = wgt[nf - 1] = 1.0
    inv1h = np.concatenate([(np.cos(thh) * wgt).T, (-np.sin(thh) * wgt).T], axis=1)[:n1 // 2] / n
    return dict(fwd1=hl(fwd1), inv1=hl(inv1), fwd2=hl(fwd2), inv2=hl(inv2),
                fwd1h=hl(fwd1h), inv1h=hl(inv1h), tw=jnp.asarray(tw, F32), n1=n1)


def _fft_stage_a(load_rows, k_rows, fwd1_ref, a_ref):
    n2 = FFT_N2
    rows = fwd1_ref.shape[1]
    f_hi = fwd1_ref[0, :, :k_rows]
    f_lo = fwd1_ref[1, :, :k_rows]

    def one(s2):
        m = _dot3(f_hi, f_lo, load_rows(s2))
        yield
        a_ref[pl.ds(pl.multiple_of(s2 * rows, SUBLANES), rows), :] = m

    def body(i, carry):
        _round_robin([one(i * FFT_ROWS_UNROLL + u) for u in range(FFT_ROWS_UNROLL)])
        return carry

    lax.fori_loop(0, n2 // FFT_ROWS_UNROLL, body, 0)


def _twiddle_cols(tw_ref, f1):
    lane = lax.broadcasted_iota(jnp.int32, (FFT_N2, LANES), 1)
    sel = lane == f1
    twr = jnp.sum(jnp.where(sel, tw_ref[0], 0.0), axis=-1, keepdims=True)
    twi = jnp.sum(jnp.where(sel, tw_ref[1], 0.0), axis=-1, keepdims=True)
    return twr, twi


def _hyena_filter_kernel(feat_ref, w1_ref, b1_ref, w2_ref, b2_ref, w3_ref, b3_ref, fr_ref,
                         wo_ref, dec_ref, fwd1_ref, fwd2_ref, tw_ref, o_ref, kt_ref, a_ref, hid_ref):
    n = feat_ref.shape[0]
    seq = n // 2
    n2 = FFT_N2
    n1 = n // n2
    tile = 512
    freq = fr_ref[...]

    @pl.when(pl.program_id(0) == 0)
    def _():
        def hidden(i, carry):
            r0 = pl.multiple_of(i * tile, tile)
            hid = jnp.sin(freq * (_hdot(feat_ref[pl.ds(r0, tile), :], w1_ref[...]) + b1_ref[...]))
            hid = jnp.sin(freq * (_hdot(hid, w2_ref[...]) + b2_ref[...]))
            hid_ref[pl.ds(r0, tile), :] = jnp.sin(freq * (_hdot(hid, w3_ref[...]) + b3_ref[...]))
            return carry

        lax.fori_loop(0, n // tile, hidden, 0)

    def gen(i, carry):
        r0 = pl.multiple_of(i * tile, tile)
        second = r0 >= seq
        wo = jnp.where(second, wo_ref[1, 0], wo_ref[0, 0])
        dec = jnp.where(second, dec_ref[1, 0], dec_ref[0, 0])
        t = feat_ref[pl.ds(r0, tile), 0:1]
        filt = _hdot(hid_ref[pl.ds(r0, tile), :], wo) * jnp.exp(-t * jnp.abs(dec))
        rows = r0 + lax.broadcasted_iota(jnp.int32, (tile, 1), 0)
        kt_ref[pl.ds(r0, tile), :] = jnp.where(rows == seq, 0.0, filt)
        return carry

    lax.fori_loop(0, n // tile, gen, 0)

    _fft_stage_a(lambda s2: kt_ref[pl.ds(s2, n1, stride=n2), :], n1, fwd1_ref, a_ref)

    nf = n1 // 2 + 1
    nfp = fwd1_ref.shape[1] // 2

    def slab(f1):
        r0 = f1 * 2 * n2 if isinstance(f1, int) else pl.multiple_of(f1 * 2 * n2, 2 * n2)
        ar = a_ref[pl.ds(f1, n2, stride=2 * nfp), :]
        ai = a_ref[pl.ds(nfp + f1, n2, stride=2 * nfp), :]
        twr, twi = _twiddle_cols(tw_ref, f1)
        x = jnp.concatenate([ar * twr - ai * twi, ar * twi + ai * twr], axis=0)
        xf = _dot3(fwd2_ref[0], fwd2_ref[1], x)
        yield
        o_ref[0, pl.ds(r0, 2 * n2), :] = xf

    def slabs(i, carry):
        _round_robin([slab(i * FFT_SLAB_UNROLL + u) for u in range(FFT_SLAB_UNROLL)])
        return carry

    lax.fori_loop(0, nf // FFT_SLAB_UNROLL, slabs, 0)
    _round_robin([slab(f1) for f1 in range(nf - nf % FFT_SLAB_UNROLL, nf)])


def _hyena_filter_spectrum(consts, feats_full, w1p, b1, w2, b2, w3, b3, freq, w_out, decay):
    n = feats_full.shape[0]
    nct = D_B // HY_CT
    spec_rows = (consts['n1'] // 2 + 1) * 2 * FFT_N2
    full = lambda a: pl.BlockSpec(a.shape, lambda j: (0,) * a.ndim)
    wo = w_out.reshape(HY_HID, 2, nct, HY_CT).transpose(1, 2, 0, 3)
    dec = decay.reshape(2, nct, 1, HY_CT)
    small = (w1p, b1.reshape(1, -1), w2, b2.reshape(1, -1), w3, b3.reshape(1, -1), freq.reshape(1, -1))
    return pl.pallas_call(
        _hyena_filter_kernel,
        grid=(nct,),
        in_specs=[full(feats_full)] + [full(a) for a in small] + [
            pl.BlockSpec((2, 1, HY_HID, HY_CT), lambda j: (0, j, 0, 0)),
            pl.BlockSpec((2, 1, 1, HY_CT), lambda j: (0, j, 0, 0)),
            full(consts['fwd1h']), full(consts['fwd2']), full(consts['tw']),
        ],
        out_specs=pl.BlockSpec((1, spec_rows, HY_CT), lambda j: (j, 0, 0)),
        out_shape=jax.ShapeDtypeStruct((nct, spec_rows, HY_CT), F32),
        scratch_shapes=[pltpu.VMEM((n, HY_CT), F32), pltpu.VMEM((2 * n, HY_CT), F32),
                        pltpu.VMEM((n, HY_HID), F32)],
        compiler_params=_cparams(("arbitrary",)),
        name="hyena_filter",
    )(feats_full, *small, wo, dec, consts['fwd1h'], consts['fwd2'], consts['tw'])


def _hyena_conv_kernel(z_ref, x0_ref, zb_ref, ks_ref, bias_ref, fwd1_ref, inv1_ref, fwd2_ref,
                       inv2_ref, tw_ref, o_ref, a_ref, b_ref, y_ref):
    seq = z_ref.shape[1]
    n2 = FFT_N2
    n1 = 2 * seq // n2
    nh = n1 // 2
    nf = nh + 1
    nfp = fwd1_ref.shape[1] // 2

    _fft_stage_a(lambda s2: z_ref[0, pl.ds(s2, nh, stride=n2), :], nh, fwd1_ref, a_ref)
    b_ref[pl.ds(nf * 2 * n2, (nfp - nf) * 2 * n2), :] = jnp.zeros(((nfp - nf) * 2 * n2, b_ref.shape[1]), F32)

    def slab(f1):
        r0 = f1 * 2 * n2 if isinstance(f1, int) else pl.multiple_of(f1 * 2 * n2, 2 * n2)
        ar = a_ref[pl.ds(f1, n2, stride=2 * nfp), :]
        ai = a_ref[pl.ds(nfp + f1, n2, stride=2 * nfp), :]
        twr, twi = _twiddle_cols(tw_ref, f1)
        x = jnp.concatenate([ar * twr - ai * twi, ar * twi + ai * twr], axis=0)
        xf = _dot3(fwd2_ref[0], fwd2_ref[1], x)
        yield
        xr, xi = xf[:n2], xf[n2:]
        kr = ks_ref[0, pl.ds(r0, n2), :]
        ki = ks_ref[0, pl.ds(r0 + n2, n2), :]
        y = jnp.concatenate([xr * kr - xi * ki, xr * ki + xi * kr], axis=0)
        bb = _dot3(inv2_ref[0], inv2_ref[1], y)
        yield
        br, bi = bb[:n2], bb[n2:]
        b_ref[pl.ds(r0, n2), :] = br * twr + bi * twi
        b_ref[pl.ds(r0 + n2, n2), :] = bi * twr - br * twi

    def slabs(i, carry):
        _round_robin([slab(i * FFT_SLAB_UNROLL + u) for u in range(FFT_SLAB_UNROLL)])
        return carry

    lax.fori_loop(0, nf // FFT_SLAB_UNROLL, slabs, 0)
    _round_robin([slab(f1) for f1 in range(nf - nf % FFT_SLAB_UNROLL, nf)])

    i_hi = inv1_ref[0]
    i_lo = inv1_ref[1]

    def stage_d(s2):
        br = b_ref[pl.ds(s2, nfp, stride=2 * n2), :]
        bi = b_ref[pl.ds(n2 + s2, nfp, stride=2 * n2), :]
        y = _dot3(i_hi, i_lo, jnp.concatenate([br, bi], axis=0))
        yield
        y_ref[pl.ds(s2, nh, stride=n2), :] = y

    def stage_ds(i, carry):
        _round_robin([stage_d(i * FFT_ROWS_UNROLL + u) for u in range(FFT_ROWS_UNROLL)])
        return carry

    lax.fori_loop(0, n2 // FFT_ROWS_UNROLL, stage_ds, 0)

    tile = 256
    def finish(i, carry):
        r0 = pl.multiple_of(i * tile, tile)
        z = z_ref[0, pl.ds(r0, tile), :]
        y = y_ref[pl.ds(r0, tile), :] + bias_ref[...] * z
        o_ref[0, pl.ds(r0, tile), :] = x0_ref[0, pl.ds(r0, tile), :] * y * _silu(zb_ref[pl.ds(r0, tile), :].astype(F32))
        return carry

    lax.fori_loop(0, seq // tile, finish, 0)


def _hyena_conv(consts, z, x0, proj_g, kspec, bias):
    bsz, seq, _ = z.shape
    nct = D_B // HY_CT
    full = lambda a: pl.BlockSpec(a.shape, lambda j, b: (0,) * a.ndim)
    cs = pl.BlockSpec((1, seq, HY_CT), lambda j, b: (b, 0, j))
    mats = (consts['fwd1h'], consts['inv1h'], consts['fwd2'], consts['inv2'], consts['tw'])
    stage_rows = consts['fwd1h'].shape[1]
    return pl.pallas_call(
        _hyena_conv_kernel,
        grid=(nct, bsz),
        in_specs=[
            cs, cs,
            pl.BlockSpec((seq, HY_CT), lambda j, b: (b, G_BZ // HY_CT + j)),
            pl.BlockSpec((1, kspec.shape[1], HY_CT), lambda j, b: (j, 0, 0)),
            pl.BlockSpec((1, HY_CT), lambda j, b: (0, j)),
        ] + [full(m) for m in mats],
        out_specs=cs,
        out_shape=jax.ShapeDtypeStruct((bsz, seq, D_B), F32),
        scratch_shapes=[pltpu.VMEM((FFT_N2 * stage_rows, HY_CT), F32),
                        pltpu.VMEM((stage_rows * FFT_N2, HY_CT), F32),
                        pltpu.VMEM((seq, HY_CT), F32)],
        compiler_params=_cparams(("arbitrary", "arbitrary")),
        name="hyena_conv",
    )(z, x0, proj_g, kspec, bias.reshape(1, D_B), *mats)


def _rwkv_kernel(r_ref, k_ref, v_ref, ul_ref, cz_ref, w0_ref, w2_ref, a0_ref, a2_ref,
                 kks_ref, ka_ref, rk_ref, lg_ref, lb_ref, o_ref, y_ref, kb_ref, s_ref,
                 wr_ref, u_ref, arb_ref, y0_ref, bt_ref, kv_ref, egc_ref,
                 kk_ref, lw_ref, be_ref, kd_ref):
    seq = r_ref.shape[1]
    L = CHUNK_C
    nc = seq // L
    P2 = 2 * L
    lane = lax.broadcasted_iota(jnp.int32, (L, LANES), 1)
    head0 = lane < HEAD_C
    prow = lax.broadcasted_iota(jnp.int32, (P2, P2), 0)
    pcol = lax.broadcasted_iota(jnp.int32, (P2, P2), 1)
    bd = (prow // HEAD_C) == (pcol // HEAD_C)
    blk = jnp.where(bd, 1.0, 0.0).astype(F32)
    trow = lax.broadcasted_iota(jnp.int32, (L, P2), 0)
    scol = lax.broadcasted_iota(jnp.int32, (L, P2), 1) % L
    eye = jnp.where(trow == scol, 1.0, 0.0).astype(F32)
    crow = lax.broadcasted_iota(jnp.int32, (L, L), 0)
    ccol = lax.broadcasted_iota(jnp.int32, (L, L), 1)

    def stack(x):
        return jnp.concatenate([jnp.where(head0, x, 0.0), jnp.where(head0, 0.0, x)], axis=0)

    blk_b = blk.astype(BF16)
    grp = wr_ref.shape[1]

    def dot(a, b):
        return jnp.dot(a, b, preferred_element_type=F32)

    def each(f, *cols):
        return [f(*xs) for xs in zip(*cols)]

    def prepare(probs):
        dirs = [d for _, d, _ in probs]
        fwd = [d == 0 for d in dirs]
        strict = [(scol < trow) if f else (scol > trow) for f in fwd]
        incl = [(scol <= trow) if f else (scol >= trow) for f in fwd]
        cum = [jnp.where((ccol <= crow) if f else (ccol >= crow), 1.0, 0.0).astype(BF16) for f in fwd]
        c0 = [pl.multiple_of(j * L, L) for j, _, _ in probs]
        r = [r_ref[0, pl.ds(c, L), :] for c in c0]
        v = [v_ref[0, pl.ds(c, L), :] for c in c0]
        kk = [kk_ref[pl.ds(c, L), :] for c in c0]
        lwt = [lw_ref[d, pl.ds(c, L), :] for c, d in zip(c0, dirs)]
        be = [be_ref[d, pl.ds(c, L), :] for c, d in zip(c0, dirs)]
        kd = [kd_ref[d, pl.ds(c, L), :] for c, d in zip(c0, dirs)]

        def cumsum(cm, x):
            l1 = x.astype(BF16)
            e1 = x - l1.astype(F32)
            l2 = e1.astype(BF16)
            l3 = (e1 - l2.astype(F32)).astype(BF16)
            return dot(jnp.concatenate([cm, cm, cm], axis=1), jnp.concatenate([l1, l2, l3], axis=0))

        g = each(cumsum, cum, lwt)
        yield
        eng = [jnp.exp(-x) for x in g]
        egc = [jnp.exp(x[L - 1:L, :] if f else x[0:1, :]) for x, f in zip(g, fwd)]
        at = each(lambda x, y, z: (-x * jnp.exp(y - z)).astype(BF16), kk, g, lwt)
        rt_ = each(lambda x, y: (x * jnp.exp(y)).astype(BF16), r, g)
        bh = each(lambda x, y: x * y, be, eng)
        kh = each(lambda x, y: x * y, kd, eng)
        lhs = each(lambda x, y: jnp.concatenate([x, y], axis=0), at, rt_)
        rhs = each(lambda x, y: jnp.concatenate([stack(x), stack(y)], axis=0).astype(BF16), bh, kh)
        g1 = each(lambda x, y: lax.dot_general(x, y, (((1,), (1,)), ((), ())),
                                               preferred_element_type=F32), lhs, rhs)
        yield
        aab = each(lambda m, x: jnp.where(m, x[:L, :P2], 0.0), strict, g1)
        aak_ark = each(lambda ms, mi, x: jnp.concatenate(
            [jnp.where(ms, x[:L, P2:], 0.0), jnp.where(mi, x[L:, P2:], 0.0)], axis=0).astype(BF16),
            strict, incl, g1)
        for (_, d, slot), mi, x in zip(probs, incl, g1):
            arb_ref[d, slot] = jnp.where(mi, x[L:, :P2], 0.0).astype(BF16)
        t = [eye + x for x in aab]
        p = [x.astype(BF16) for x in aab]
        vs = [stack(x).astype(BF16) for x in v]
        av = each(dot, aak_ark, vs)
        p = [dot(x, stack(x)) for x in p]
        yield
        p = [x.astype(BF16) for x in p]
        sq = 2
        while sq < L:
            if 2 * sq >= L:
                pt = each(lambda x, y: dot(x, stack(y.astype(BF16))), p, t)
                yield
                t = each(lambda y, x: y + x, t, pt)
            else:
                pt = each(lambda x, y: dot(x, jnp.concatenate([stack(x), stack(y.astype(BF16))], axis=1)),
                          p, t)
                yield
                p = [x[:, :P2].astype(BF16) for x in pt]
                t = each(lambda y, x: y + x[:, P2:], t, pt)
            sq *= 2
        wu = each(lambda x, y, z: dot(x.astype(BF16), jnp.concatenate(
            [stack(y), stack(z[:L].astype(BF16))], axis=1)), t, at, av)
        kv = each(lambda x, y, z: lax.dot_general(
            x.astype(BF16), (y * z).astype(BF16), (((0,), (0,)), ((), ())), preferred_element_type=F32),
            v, kh, egc)
        yield
        for i, (_, d, slot) in enumerate(probs):
            wr_ref[d, slot] = jnp.concatenate([wu[i][:, :P2].astype(BF16), rt_[i]], axis=0)
            u_ref[d, slot] = wu[i][:, P2:].astype(BF16)
            y0_ref[d, slot] = av[i][L:]
            bt_ref[d, slot] = (bh[i] * egc[i]).astype(BF16)
            kv_ref[d, slot] = jnp.where(bd, kv[i], 0.0).astype(BF16)
            egc_ref[d, slot] = egc[i]

    def advance(steps):
        s = [s_ref[d] for d in range(2)]
        for probs in steps:
            x = [lax.dot_general(wr_ref[d, slot], s[d].astype(BF16), (((1,), (1,)), ((), ())),
                                 preferred_element_type=F32) for _, d, slot in probs]
            yield
            sa = [(y[:L] + u_ref[d, slot].astype(F32)).astype(BF16) for (_, d, slot), y in zip(probs, x)]
            upd = [lax.dot_general(y, bt_ref[d, slot], (((0,), (0,)), ((), ())),
                                   preferred_element_type=F32) for (_, d, slot), y in zip(probs, sa)]
            z = [dot(arb_ref[d, slot], stack(y)) for (_, d, slot), y in zip(probs, sa)]
            yield
            for i, (j, d, slot) in enumerate(probs):
                s[d] = (s[d] * egc_ref[d, slot] + kv_ref[d, slot].astype(F32)
                        + jnp.where(bd, upd[i], 0.0))
                c0 = pl.multiple_of(j * L, L)
                y_ref[pl.ds(c0, L), :] += x[i][L:] + z[i] + y0_ref[d, slot]
        for d in range(2):
            s_ref[d] = s[d]

    s_ref[...] = jnp.zeros_like(s_ref)
    y_ref[...] = jnp.zeros_like(y_ref)
    tile = 256

    def pointwise(r0):
        k = k_ref[0, pl.ds(r0, tile), :]
        lwla = ul_ref[0, pl.ds(r0, tile), :]
        th = jnp.tanh(lwla[:, :LANES]).astype(BF16)
        la = lwla[:, LANES:].astype(BF16)
        xw = [w0_ref[d:d + 1, :] + dot(th, w2_ref[d].astype(BF16)) for d in range(2)]
        aa = [a0_ref[d:d + 1, :] + dot(la, a2_ref[d].astype(BF16)) for d in range(2)]
        kkv = k * kks_ref[...]
        ss = dot((kkv * kkv).astype(BF16), blk_b)
        yield
        kk = kkv * lax.rsqrt(jnp.maximum(ss, 1e-24))
        kk_ref[pl.ds(r0, tile), :] = kk
        kb = None
        for d in range(2):
            lw_ref[d, pl.ds(r0, tile), :] = -math.exp(-0.5) * _sigmoid(xw[d])
            a = _sigmoid(aa[d])
            kd = k * (1.0 + (a - 1.0) * ka_ref[...])
            be_ref[d, pl.ds(r0, tile), :] = kk * a
            kd_ref[d, pl.ds(r0, tile), :] = kd
            kb = kd if kb is None else kb + kd
        kb_ref[pl.ds(r0, tile), :] = kb

    pw_unroll = 2 if (seq // tile) % 2 == 0 else 1

    def pointwise_loop(i, carry):
        _round_robin([pointwise(pl.multiple_of((i * pw_unroll + u) * tile, tile))
                      for u in range(pw_unroll)])
        return carry

    lax.fori_loop(0, seq // tile // pw_unroll, pointwise_loop, 0)

    lock = grp // 2
    nsets = nc // lock

    def chunk_set(i):
        base = (i % 2) * lock
        return [[(i * lock + q, 0, base + q), (nc - 1 - (i * lock + q), 1, base + q)]
                for q in range(lock)]

    _round_robin([prepare(sum(chunk_set(0), []))])

    def pipelined(i, carry):
        _round_robin([prepare(sum(chunk_set(i + 1), [])), advance(chunk_set(i))])
        return carry

    lax.fori_loop(0, nsets - 1, pipelined, 0)
    _round_robin([advance(chunk_set(nsets - 1))])

    tile = 256
    inv_n = 1.0 / HEAD_C
    blk2 = jnp.concatenate([blk_b, blk_b], axis=0)

    def head_sum(x):
        hi, lo = _split_bf16(x)
        return dot(jnp.concatenate([hi, lo], axis=1), blk2)

    def finish(r0):
        y = y_ref[pl.ds(r0, tile), :]
        mu = head_sum(y) * inv_n
        r = r_ref[0, pl.ds(r0, tile), :]
        v = v_ref[0, pl.ds(r0, tile), :]
        kbon = 0.5 * kb_ref[pl.ds(r0, tile), :]
        bonus = head_sum(r * kbon * rk_ref[...]) * v
        yield
        yc = y - mu
        var = head_sum(yc * yc) * inv_n
        yield
        yn = yc * lax.rsqrt(var + LNX_EPS) * lg_ref[...] + lb_ref[...]
        o_ref[0, pl.ds(r0, tile), :] = (yn + bonus) * _silu(cz_ref[pl.ds(r0, tile), :].astype(F32))

    unroll = 4 if (seq // tile) % 4 == 0 else 1

    def finishes(i, carry):
        _round_robin([finish(pl.multiple_of((i * unroll + u) * tile, tile)) for u in range(unroll)])
        return carry

    lax.fori_loop(0, seq // tile // unroll, finishes, 0)


def _rwkv(u, ul, proj_g, w0, w2pad, a0, a2pad, kk_s, ka, rk, lnx_g, lnx_b):
    bsz, seq, _ = u.shape
    npair = D_C // LANES
    grp = min(RW_GROUP, seq // CHUNK_C)
    col = lambda off: pl.BlockSpec((1, seq, LANES), lambda b, p: (b, 0, off + p))
    vec = pl.BlockSpec((1, LANES), lambda b, p: (0, p))
    two = pl.BlockSpec((2, LANES), lambda b, p: (0, p))
    lora = pl.BlockSpec((2, LANES, LANES), lambda b, p: (0, 0, p))
    row = lambda a: a.reshape(1, D_C)
    return pl.pallas_call(
        _rwkv_kernel,
        grid=(bsz, npair),
        in_specs=[
            col(0), col(npair), col(2 * npair),
            pl.BlockSpec((1, seq, 2 * LANES), lambda b, p: (b, 0, 0)),
            pl.BlockSpec((seq, LANES), lambda b, p: (b, G_CZ // LANES + p)),
            two, lora, two, lora, vec, vec, vec, vec, vec,
        ],
        out_specs=pl.BlockSpec((1, seq, LANES), lambda b, p: (b, 0, p)),
        out_shape=jax.ShapeDtypeStruct((bsz, seq, D_C), F32),
        scratch_shapes=[pltpu.VMEM((seq, LANES), F32), pltpu.VMEM((seq, LANES), F32),
                        pltpu.VMEM((2, LANES, LANES), F32),
                        pltpu.VMEM((2, grp, 2 * CHUNK_C, LANES), BF16),
                        pltpu.VMEM((2, grp, CHUNK_C, LANES), BF16),
                        pltpu.VMEM((2, grp, CHUNK_C, LANES), BF16),
                        pltpu.VMEM((2, grp, CHUNK_C, LANES), F32),
                        pltpu.VMEM((2, grp, CHUNK_C, LANES), BF16),
                        pltpu.VMEM((2, grp, LANES, LANES), BF16),
                        pltpu.VMEM((2, grp, 1, LANES), F32),
                        pltpu.VMEM((seq, LANES), F32), pltpu.VMEM((2, seq, LANES), F32),
                        pltpu.VMEM((2, seq, LANES), F32), pltpu.VMEM((2, seq, LANES), F32)],
        compiler_params=_cparams(("arbitrary", "arbitrary")),
        name="rwkv7",
    )(u, u, u, ul, proj_g, w0, w2pad, a0, a2pad, row(kk_s), row(ka), row(rk), row(lnx_g), row(lnx_b))


def _out_kernel(ya_ref, yb_ref, yc_ref, ga_ref, gb_ref, gc_ref, x_ref, gate_ref, pg_ref,
                wa_ref, wb_ref, wc_ref, wo_ref, o_ref):
    merged = (_sigmoid(ga_ref[...].astype(F32)) * _bdot(ya_ref[...], wa_ref[...])
              + _sigmoid(gb_ref[...].astype(F32)) * _bdot(yb_ref[...], wb_ref[...])
              + _sigmoid(gc_ref[...].astype(F32)) * _bdot(yc_ref[...], wc_ref[...]))
    out = _bdot(merged, wo_ref[...])
    y = out * lax.rsqrt(jnp.mean(out * out, axis=-1, keepdims=True) + NORM_EPS) * pg_ref[...]
    o_ref[...] = x_ref[...] + gate_ref[0] * y


def _merge_out(ya, yb, yc, proj_g, x2d, seq, gate, post_g, wa, wb, wc, wo, tm):
    rows, d = x2d.shape
    per_b = seq // tm
    full = lambda a: pl.BlockSpec(a.shape, lambda i: (0,) * a.ndim)
    gcol = lambda off: pl.BlockSpec((tm, d), lambda i: (i, off // d))
    return pl.pallas_call(
        _out_kernel,
        grid=(rows // tm,),
        in_specs=[
            pl.BlockSpec((tm, D_A), lambda i: (i, 0)),
            pl.BlockSpec((tm, D_B), lambda i: (i, 0)),
            pl.BlockSpec((tm, D_C), lambda i: (i, 0)),
            gcol(G_GA), gcol(G_GB), gcol(G_GC),
            pl.BlockSpec((tm, d), lambda i: (i, 0)),
            pl.BlockSpec((1, 1, d), lambda i: (i // per_b, 0, 0)),
            pl.BlockSpec((1, d), lambda i: (0, 0)),
            full(wa), full(wb), full(wc), full(wo),
        ],
        out_specs=pl.BlockSpec((tm, d), lambda i: (i, 0)),
        out_shape=jax.ShapeDtypeStruct((rows, d), F32),
        compiler_params=_cparams(("arbitrary",)),
        name="merge_out",
    )(ya.reshape(rows, D_A), yb.reshape(rows, D_B), yc.reshape(rows, D_C), proj_g, proj_g, proj_g,
      x2d, gate, post_g.reshape(1, d), wa, wb, wc, wo)


def _hyena_feats(seq):
    n = 2 * seq
    t = jnp.linspace(0.0, 1.0, seq, dtype=F32)
    w = 2.0 * math.pi * jnp.arange(seq, dtype=F32) / seq
    f = jnp.linspace(1e-4, HY_BANDS - 1, HY_BANDS, dtype=F32)
    zz = w[:, None] * f[None, :]
    feats = jnp.concatenate([t[:, None], jnp.cos(zz), -jnp.sin(zz)], axis=-1)
    pos = np.concatenate([np.arange(seq), [0], np.arange(seq - 1, 0, -1)])
    full = jnp.zeros((n, LANES), F32).at[:, :feats.shape[1]].set(feats[pos])
    return full


def _tiles(seq):
    return min(seq, 1024), min(seq, 256), min(seq, 512)


def kernel(x, c, ada_w, ada_b, pre_g, post_g, w_in, ml_conv_w, ml_conv_b, ml_wq, ml_wk, ml_wv, ml_gate_w, ml_gate_b, ml_norm_g, ml_skip, hy_conv_w, hy_conv_b, hy_w1, hy_b1, hy_w2, hy_b2, hy_w3, hy_b3, hy_freq, hy_w_out, hy_decay, hy_bias, rw_mu, rw_w0, rw_w2, rw_a0, rw_a2, rw_kk, rw_ka, rw_rk, rw_lnx_g, rw_lnx_b, w_branch_a, w_branch_b, w_branch_c, w_out):
    bsz, seq, d = x.shape
    depth = ada_w.shape[0]
    tm_in, ts_prep, tm_out = _tiles(seq)
    runs_h, runs_g = (_contiguous_runs(p) for p in _ref_column_perm())
    consts = _dft_constants(seq)
    feats_full = _hyena_feats(seq)
    mod = _modulation(c, ada_w, ada_b)
    x2d = x.reshape(bsz * seq, d)
    n_gate = 4 * H_A
    for l in range(depth):
        shift = mod[l, :, None, 0:d]
        scale = mod[l, :, None, d:2 * d]
        gate = mod[l, :, None, 2 * d:3 * d]
        w_h = jnp.concatenate([w_in[l][:, a:b] for a, b in runs_h], axis=1).astype(BF16)
        w_g = jnp.concatenate([w_in[l][:, a:b] for a, b in runs_g], axis=1).astype(BF16)
        proj_h = _inproj(x2d, seq, pre_g[l], shift, scale, w_h, tm_in, N_H // 3)
        proj_g = _inproj(x2d, seq, pre_g[l], shift, scale, w_g, tm_in, N_G // 4)

        gw = jnp.zeros((3 * D_A, LANES), F32).at[:, :n_gate].set(ml_gate_w[l])
        gb = jnp.zeros((1, LANES), F32).at[0, :n_gate].set(ml_gate_b[l])
        q, k, v, xc, gcol, z, x0, u, ul = _prep(
            proj_h, bsz, seq, ts_prep, ml_conv_w[l], ml_conv_b[l].reshape(1, D_A),
            ml_wq[l], ml_wk[l], ml_wv[l], gw, gb,
            hy_conv_w[l], hy_conv_b[l].reshape(1, 3 * D_B), rw_mu[l].reshape(1, -1))

        y_a = _mlstm(q, k, v, xc, proj_g, gcol, ml_norm_g[l], ml_skip[l])

        w1p = jnp.zeros((LANES, HY_HID), F32).at[:hy_w1.shape[1]].set(hy_w1[l])
        kspec = _hyena_filter_spectrum(consts, feats_full, w1p, hy_b1[l], hy_w2[l], hy_b2[l],
                                       hy_w3[l], hy_b3[l], hy_freq[l], hy_w_out[l], hy_decay[l])
        y_b = _hyena_conv(consts, z, x0, proj_g, kspec, hy_bias[l])

        w2pad = (jnp.zeros((2, LANES, D_C), F32).at[0, :LORA].set(rw_w2[l, 0])
                 .at[1, LORA:].set(rw_w2[l, 1]))
        a2pad = (jnp.zeros((2, LANES, D_C), F32).at[0, :LORA].set(rw_a2[l, 0])
                 .at[1, LORA:].set(rw_a2[l, 1]))
        y_c = _rwkv(u, ul, proj_g, rw_w0[l], w2pad, rw_a0[l], a2pad, rw_kk[l], rw_ka[l],
                    rw_rk[l].reshape(-1), rw_lnx_g[l], rw_lnx_b[l])

        x2d = _merge_out(y_a, y_b, y_c, proj_g, x2d, seq, gate, post_g[l],
                         w_branch_a[l].astype(BF16), w_branch_b[l].astype(BF16),
                         w_branch_c[l].astype(BF16), w_out[l].astype(BF16), tm_out)
    return x2d.reshape(bsz, seq, d)
```

```python
import functools
import math

import numpy as np
import jax
import jax.numpy as jnp
from jax import lax
from jax.experimental import pallas as pl
from jax.experimental.pallas import tpu as pltpu

D_MODEL = 1024
DEPTH = 4
D_A = 512
H_A = 4
DH_A = 128
CHUNK_A = 64
ML_GROUP = 16
D_B = 512
HY_BANDS = 16
HY_HID = 64
D_C = 1024
HEAD_C = 64
H_C = D_C // HEAD_C
LORA = 64
CHUNK_C = 64
RW_GROUP = 8
LNX_EPS = 64e-5
NORM_EPS = 1e-6
HEAD_NORM_EPS = 1e-5

LANES = 128
SUBLANES = 8
VMEM_LIMIT = 56 * 1024 * 1024

F32 = jnp.float32
BF16 = jnp.bfloat16
PROJ_DTYPE = BF16
PROJ_ROWS = 16
HIGHEST = lax.Precision.HIGHEST

N_H = D_A + 3 * D_B + 3 * D_C + 4 * LORA
N_G = D_A + D_B + D_C + 3 * D_MODEL
H_AX, H_BV, H_BX0, H_BX1, H_CR, H_CK, H_CV, H_LW, H_LA = (
    0, 512, 1024, 1536, 2048, 3072, 4096, 5120, 5248)
G_AZ, G_BZ, G_CZ, G_GA, G_GB, G_GC = 0, 512, 1024, 2048, 3072, 4096


def _ref_column_perm():
    r = lambda a, n: np.arange(a, a + n)
    a_x, a_z = r(0, 512), r(512, 512)
    b_v, b_x0, b_x1, b_z = r(1024, 512), r(1536, 512), r(2048, 512), r(2560, 512)
    c_r, c_k, c_v = r(3072, 1024), r(4096, 1024), r(5120, 1024)
    c_lw, c_la = r(6144, 128), r(6272, 128)
    c_z = r(6400, 1024)
    g_a, g_b, g_c = r(7424, 1024), r(8448, 1024), r(9472, 1024)
    h = np.concatenate([a_x, b_v, b_x0, b_x1, c_r, c_k, c_v, c_lw, c_la])
    g = np.concatenate([a_z, b_z, c_z, g_a, g_b, g_c])
    return h, g


def _contiguous_runs(idx):
    cuts = np.flatnonzero(np.diff(idx) != 1) + 1
    return [(int(seg[0]), int(seg[-1]) + 1) for seg in np.split(idx, cuts)]


def _cparams(sem, **extra):
    return pltpu.CompilerParams(dimension_semantics=sem, vmem_limit_bytes=VMEM_LIMIT, **extra)


def _silu(x):
    return x * (1.0 / (1.0 + jnp.exp(-x)))


def _sigmoid(x):
    return 1.0 / (1.0 + jnp.exp(-x))


def _bdot(a, b):
    return jnp.dot(a.astype(BF16), b.astype(BF16), preferred_element_type=F32)


def _bdot_nt(a, b):
    return lax.dot_general(a.astype(BF16), b.astype(BF16), (((1,), (1,)), ((), ())),
                           preferred_element_type=F32)


def _round_robin(gens):
    gens = list(gens)
    while gens:
        alive = []
        for g in gens:
            try:
                next(g)
                alive.append(g)
            except StopIteration:
                pass
        gens = alive


def _hdot(a, b):
    return jnp.dot(a, b, preferred_element_type=F32, precision=HIGHEST)


def _mod_kernel(c_ref, w_ref, b_ref, o_ref):
    cond = _silu(c_ref[...])
    o_ref[0] = _hdot(cond, w_ref[0]) + b_ref[0]


def _modulation(c, ada_w, ada_b):
    depth, d, n3 = ada_w.shape
    bsz = c.shape[0]
    nt = n3 // d
    return pl.pallas_call(
        _mod_kernel,
        grid=(depth, nt),
        in_specs=[
            pl.BlockSpec((bsz, d), lambda l, j: (0, 0)),
            pl.BlockSpec((1, d, d), lambda l, j: (l, 0, j)),
            pl.BlockSpec((1, 1, d), lambda l, j: (l, 0, j)),
        ],
        out_specs=pl.BlockSpec((1, bsz, d), lambda l, j: (l, 0, j)),
        out_shape=jax.ShapeDtypeStruct((depth, bsz, n3), F32),
        compiler_params=_cparams(("arbitrary", "arbitrary")),
        name="adaln_mod",
    )(c, ada_w, ada_b.reshape(depth, 1, n3))


def _inproj_kernel(x_ref, g_ref, shift_ref, scale_ref, w_ref, o_ref, h_ref):
    @pl.when(pl.program_id(1) == 0)
    def _():
        x = x_ref[...]
        y = x * lax.rsqrt(jnp.mean(x * x, axis=-1, keepdims=True) + NORM_EPS)
        h = y * g_ref[...] * (1.0 + scale_ref[0]) + shift_ref[0]
        h_ref[...] = h.astype(BF16)

    o_ref[...] = jnp.dot(h_ref[...], w_ref[...], preferred_element_type=F32).astype(o_ref.dtype)


def _inproj(x2d, seq, pre_g, shift, scale, w_bf16, tm, tn):
    rows, d = x2d.shape
    n = w_bf16.shape[1]
    per_b = seq // tm
    return pl.pallas_call(
        _inproj_kernel,
        grid=(rows // tm, n // tn),
        in_specs=[
            pl.BlockSpec((tm, d), lambda i, j: (i, 0)),
            pl.BlockSpec((1, d), lambda i, j: (0, 0)),
            pl.BlockSpec((1, 1, d), lambda i, j: (i // per_b, 0, 0)),
            pl.BlockSpec((1, 1, d), lambda i, j: (i // per_b, 0, 0)),
            pl.BlockSpec((d, tn), lambda i, j: (0, j)),
        ],
        out_specs=pl.BlockSpec((tm, tn), lambda i, j: (i, j)),
        out_shape=jax.ShapeDtypeStruct((rows, n), PROJ_DTYPE),
        scratch_shapes=[pltpu.VMEM((tm, d), BF16)],
        compiler_params=_cparams(("arbitrary", "arbitrary")),
        name="inproj",
    )(x2d, pre_g.reshape(1, d), shift, scale, w_bf16)


def _log_sigmoid(x):
    return jnp.minimum(x, 0.0) - jnp.log(1.0 + jnp.exp(-jnp.abs(x)))


def _prep_kernel(main_ref, prev_ref, next_ref,
                 mcw_ref, mcb_ref, wq_ref, wk_ref, wv_ref, gw_ref, gb_ref,
                 hcw_ref, hcb_ref, mu_ref,
                 q_ref, k_ref, v_ref, xc_ref, gcol_ref, z_ref, x0_ref, u_ref, ul_ref):
    i = pl.program_id(1)
    ts = main_ref.shape[0]
    has_prev = jnp.where(i > 0, 1.0, 0.0).astype(F32)
    has_next = jnp.where(i < pl.num_programs(1) - 1, 1.0, 0.0).astype(F32)

    def neighbours(c0, cw):
        x = main_ref[:, c0:c0 + cw].astype(F32)
        row = lax.broadcasted_iota(jnp.int32, (SUBLANES, cw), 0)
        p_row = prev_ref[:, c0:c0 + cw].astype(F32)[PROJ_ROWS - 1:PROJ_ROWS] * has_prev
        n_row = next_ref[:, c0:c0 + cw].astype(F32)[0:1] * has_next
        xp = pltpu.roll(x, 1, 0)
        xn = pltpu.roll(x, ts - 1, 0)
        xp = jnp.concatenate([jnp.where(row == 0, p_row, xp[:SUBLANES]), xp[SUBLANES:]], axis=0)
        xn = jnp.concatenate([xn[:ts - SUBLANES],
                              jnp.where(row == SUBLANES - 1, n_row, xn[ts - SUBLANES:])], axis=0)
        return xp, x, xn

    def conv(c0, cw, w_ref, b_ref, w0):
        xp, x, xn = neighbours(c0, cw)
        w = w_ref[:, w0:w0 + cw]
        return xp * w[0:1] + x * w[1:2] + xn * w[2:3] + b_ref[:, w0:w0 + cw], x

    gates = jnp.zeros((ts, LANES), F32) + gb_ref[...]
    for h in range(H_A):
        sl = slice(h * DH_A, (h + 1) * DH_A)
        conv_a, xa = conv(H_AX + h * DH_A, DH_A, mcw_ref, mcb_ref, h * DH_A)
        xc = _silu(conv_a)
        xc_ref[0, :, sl] = xc
        qh = _bdot(xc, wq_ref[h])
        kh = _bdot(xc, wk_ref[h])
        vh = _bdot(xa, wv_ref[h])
        gates += (_bdot(qh, gw_ref[h * DH_A:(h + 1) * DH_A])
                  + _bdot(kh, gw_ref[D_A + h * DH_A:D_A + (h + 1) * DH_A])
                  + _bdot(vh, gw_ref[2 * D_A + h * DH_A:2 * D_A + (h + 1) * DH_A]))
        q_ref[0, :, sl] = qh.astype(BF16)
        k_ref[0, :, sl] = (kh * (DH_A ** -0.5)).astype(BF16)
        v_ref[0, :, sl] = vh.astype(BF16)
    col = lax.broadcasted_iota(jnp.int32, (ts, LANES), 1)
    rmod = lax.broadcasted_iota(jnp.int32, (ts, LANES), 0) % CHUNK_A
    lf = _log_sigmoid(gates)
    cf = lf
    cb = lf
    sh = 1
    while sh < CHUNK_A:
        cf = cf + jnp.where(rmod >= sh, pltpu.roll(cf, sh, 0), 0.0)
        cb = cb + jnp.where(rmod < CHUNK_A - sh, pltpu.roll(cb, ts - sh, 0), 0.0)
        sh *= 2
    is_ff = (col >= H_A) & (col < 2 * H_A)
    is_fb = (col >= 3 * H_A) & (col < 4 * H_A)
    gcol_ref[0] = jnp.where(is_ff, cf, jnp.where(is_fb, cb, gates))

    cw = LANES
    for j in range(D_B // cw):
        c = j * cw
        cv, _ = conv(H_BV + c, cw, hcw_ref, hcb_ref, c)
        cx1, _ = conv(H_BX1 + c, cw, hcw_ref, hcb_ref, 2 * D_B + c)
        z_ref[0, :, c:c + cw] = cv * cx1
        cx0, _ = conv(H_BX0 + c, cw, hcw_ref, hcb_ref, D_B + c)
        x0_ref[0, :, c:c + cw] = cx0

    def shifted(c):
        xp, x, xn = neighbours(H_CR + c, cw)
        mu = mu_ref[:, c:c + cw]
        return x * (1.0 - mu) + (0.5 * mu) * (xp + xn)

    for j in range(3 * D_C // cw):
        u_ref[0, :, j * cw:(j + 1) * cw] = shifted(j * cw)
    for j in range(4 * LORA // cw):
        ul_ref[0, :, j * cw:(j + 1) * cw] = shifted(3 * D_C + j * cw)


def _prep(proj_h, bsz, seq, ts, ml_conv_w, ml_conv_b, wq, wk, wv, gate_w_pad, gate_b_pad,
          hy_conv_w, hy_conv_b, rw_mu):
    ns = seq // ts
    hb = ts // PROJ_ROWS
    last_hb = bsz * seq // PROJ_ROWS - 1
    full = lambda a: pl.BlockSpec(a.shape, lambda b, i: (0,) * a.ndim)
    seq_spec = lambda w: pl.BlockSpec((1, ts, w), lambda b, i: (b, i, 0))
    params = (ml_conv_w, ml_conv_b, wq, wk, wv, gate_w_pad, gate_b_pad, hy_conv_w, hy_conv_b, rw_mu)
    outs = [(D_A, BF16), (D_A, BF16), (D_A, BF16), (D_A, F32), (LANES, F32),
            (D_B, F32), (D_B, F32), (3 * D_C, F32), (4 * LORA, F32)]
    return pl.pallas_call(
        _prep_kernel,
        grid=(bsz, ns),
        in_specs=[
            pl.BlockSpec((ts, N_H), lambda b, i: (b * ns + i, 0)),
            pl.BlockSpec((PROJ_ROWS, N_H), lambda b, i: (jnp.maximum((b * ns + i) * hb - 1, 0), 0)),
            pl.BlockSpec((PROJ_ROWS, N_H), lambda b, i: (jnp.minimum((b * ns + i + 1) * hb, last_hb), 0)),
        ] + [full(p) for p in params],
        out_specs=[seq_spec(w) for w, _ in outs],
        out_shape=[jax.ShapeDtypeStruct((bsz, seq, w), dt) for w, dt in outs],
        compiler_params=_cparams(("arbitrary", "arbitrary")),
        name="prep",
    )(proj_h, proj_h, proj_h, *params)


def _mlstm_kernel(q_ref, k_ref, v_ref, xc_ref, za_ref, gcol_ref, ng_ref, sk_ref, o_ref,
                  h_ref, ct_ref, n_ref, m_ref, num_ref, den_ref, bb_ref, mb_ref, kv_ref, sc_ref,
                  sel_ref):
    head = pl.program_id(1)
    seq = q_ref.shape[1]
    L = CHUNK_A
    nc = seq // L
    row = lax.broadcasted_iota(jnp.int32, (L, L), 0)
    colm = lax.broadcasted_iota(jnp.int32, (L, L), 1)
    lane = lax.broadcasted_iota(jnp.int32, (L, LANES), 1)

    grp = num_ref.shape[1]
    srow = lax.broadcasted_iota(jnp.int32, (SUBLANES, LANES), 0)
    ones = jnp.ones((LANES, LANES), BF16)
    ones2 = jnp.ones((2 * LANES, LANES), BF16)

    def dot(a, b):
        return jnp.dot(a, b, preferred_element_type=F32)

    def split3(x):
        x1 = x.astype(BF16)
        e1 = x - x1.astype(F32)
        x2 = e1.astype(BF16)
        return x1, x2, (e1 - x2.astype(F32)).astype(BF16)

    srow_i = lax.broadcasted_iota(jnp.int32, (3 * LANES, 2 * LANES), 0) % LANES
    scol_i = lax.broadcasted_iota(jnp.int32, (3 * LANES, 2 * LANES), 1)
    for d in range(2):
        li_lane = head + 2 * H_A * d
        want = jnp.where(scol_i < LANES, li_lane + H_A, li_lane)
        sel_ref[d] = jnp.where(srow_i == want, 1.0, 0.0).astype(BF16)

    def prepare(j, d, slot):
        fwd = d == 0
        tri = (colm <= row) if fwd else (colm >= row)
        c0 = pl.multiple_of(j * L, L)
        q = q_ref[0, pl.ds(c0, L), :]
        k = k_ref[0, pl.ds(c0, L), :]
        v = v_ref[0, pl.ds(c0, L), :]
        qk = _bdot_nt(q, k)
        bl = dot(jnp.concatenate(split3(gcol_ref[0, pl.ds(c0, L), :]), axis=1), sel_ref[d])
        yield
        b, li = bl[:, :LANES], bl[:, LANES:]
        x0, x1, x2 = split3(b)
        y0, y1, y2 = split3(li - b)
        xl = jnp.where(lane == 0, x0, jnp.where(lane == 1, x1, jnp.where(
            lane == 2, x2, jnp.where(lane < 6, 1.0, 0.0).astype(BF16))))
        yl = jnp.where(lane == 3, y0, jnp.where(lane == 4, y1, jnp.where(
            lane == 5, y2, jnp.where(lane < 3, 1.0, 0.0).astype(BF16))))
        dm = _bdot_nt(xl, yl)
        gtot = b[L - 1:L, :] if fwd else b[0:1, :]
        a = gtot - b + li
        a_max = jnp.max(a, axis=0, keepdims=True)
        wk = jnp.exp(a - a_max)
        vw = (v.astype(F32) * wk).astype(BF16)
        kv = lax.dot_general(k, vw, (((0,), (0,)), ((), ())), preferred_element_type=F32)
        kn = jnp.sum(k.astype(F32) * wk, axis=0, keepdims=True)
        sc_ref[d, slot] = jnp.where(srow == 0, kn, jnp.where(srow == 1, gtot, a_max))
        bb_ref[d, slot] = b
        yield
        dm = jnp.where(tri, dm, -jnp.inf)
        m_loc = jnp.max(dm, axis=-1, keepdims=True)
        s = qk * jnp.exp(dm - m_loc)
        num = _bdot(s, v)
        den = dot(jnp.concatenate(_split_bf16(s), axis=1), ones)
        mb_ref[d, slot] = jnp.broadcast_to(m_loc, (L, LANES))
        kv_ref[d, slot] = kv
        yield
        num_ref[d, slot] = num
        den_ref[d, slot] = den

    def advance(d, todo):
        ct, n, m = ct_ref[d], n_ref[d], m_ref[d]
        pending = []
        for j, slot in todo:
            c0 = pl.multiple_of(j * L, L)
            q = q_ref[0, pl.ds(c0, L), :]
            qc = _bdot(q, ct)
            qn = dot(jnp.concatenate(_split_bf16(q.astype(F32) * n), axis=1), ones2)
            sc = sc_ref[d, slot]
            kn, gtot, a_max = sc[0:1, :], sc[1:2, :], sc[2:3, :]
            m_new = jnp.maximum(gtot + m, a_max)
            decay = jnp.exp(gtot + m - m_new)
            beta = jnp.exp(a_max - m_new)
            pending.append((c0, slot, qc, qn, m))
            ct = decay * ct + beta * kv_ref[d, slot]
            n = decay * n + beta * kn
            m = m_new
        ct_ref[d], n_ref[d], m_ref[d] = ct, n, m
        yield
        for c0, slot, qc, qn, m_in in pending:
            inter = bb_ref[d, slot] + m_in
            m_loc = mb_ref[d, slot]
            m_t = jnp.maximum(inter, m_loc)
            c_intra = jnp.exp(m_loc - m_t)
            c_inter = jnp.exp(inter - m_t)
            num = c_intra * num_ref[d, slot] + c_inter * qc
            den = c_intra * den_ref[d, slot] + c_inter * qn
            h_ref[pl.ds(c0, L), :] += num / jnp.maximum(jnp.abs(den), jnp.exp(-m_t))

    ct_ref[...] = jnp.zeros_like(ct_ref)
    n_ref[...] = jnp.zeros_like(n_ref)
    m_ref[...] = jnp.full_like(m_ref, -jnp.inf)
    h_ref[...] = jnp.zeros_like(h_ref)
    lock = 8 if grp % 8 == 0 else 1

    def group(gi, carry):
        def phase_a(i, c):
            gens = []
            for u in range(lock):
                jf = gi * grp + i * lock + u
                gens += [prepare(jf, 0, i * lock + u), prepare(nc - 1 - jf, 1, i * lock + u)]
            _round_robin(gens)
            return c

        def phase_b(i, c):
            jf = [gi * grp + i * lock + u for u in range(lock)]
            _round_robin([advance(0, [(j, i * lock + u) for u, j in enumerate(jf)]),
                          advance(1, [(nc - 1 - j, i * lock + u) for u, j in enumerate(jf)])])
            return c

        lax.fori_loop(0, grp // lock, phase_a, 0)
        lax.fori_loop(0, grp // lock, phase_b, 0)
        return carry

    lax.fori_loop(0, nc // grp, group, 0)

    tile = 256
    inv_n = 1.0 / DH_A

    def lane_sum(x):
        return dot(jnp.concatenate(_split_bf16(x), axis=1), ones2)

    def finish(r0):
        hh = h_ref[pl.ds(r0, tile), :]
        mu = lane_sum(hh) * inv_n
        yield
        hc = hh - mu
        var = lane_sum(hc * hc) * inv_n
        yield
        hn = hc * lax.rsqrt(var + HEAD_NORM_EPS) * ng_ref[...]
        out = (hn + sk_ref[...] * xc_ref[0, pl.ds(r0, tile), :]) * _silu(za_ref[pl.ds(r0, tile), :].astype(F32))
        o_ref[0, pl.ds(r0, tile), :] = out

    unroll = 4 if (seq // tile) % 4 == 0 else 1

    def finishes(i, carry):
        _round_robin([finish(pl.multiple_of((i * unroll + u) * tile, tile)) for u in range(unroll)])
        return carry

    lax.fori_loop(0, seq // tile // unroll, finishes, 0)


def _mlstm(q, k, v, xc, proj_g, gcol, norm_g, skip):
    bsz, seq, _ = q.shape
    grp = min(ML_GROUP, seq // CHUNK_A)
    hs = lambda: pl.BlockSpec((1, seq, DH_A), lambda b, h: (b, 0, h))
    return pl.pallas_call(
        _mlstm_kernel,
        grid=(bsz, H_A),
        in_specs=[
            hs(), hs(), hs(), hs(),
            pl.BlockSpec((seq, DH_A), lambda b, h: (b, G_AZ // DH_A + h)),
            pl.BlockSpec((1, seq, LANES), lambda b, h: (b, 0, 0)),
            pl.BlockSpec((1, DH_A), lambda b, h: (0, h)),
            pl.BlockSpec((1, DH_A), lambda b, h: (0, h)),
        ],
        out_specs=pl.BlockSpec((1, seq, DH_A), lambda b, h: (b, 0, h)),
        out_shape=jax.ShapeDtypeStruct((bsz, seq, D_A), F32),
        scratch_shapes=[pltpu.VMEM((seq, DH_A), F32), pltpu.VMEM((2, DH_A, DH_A), F32),
                        pltpu.VMEM((2, 1, DH_A), F32), pltpu.VMEM((2, 1, LANES), F32),
                        pltpu.VMEM((2, grp, CHUNK_A, DH_A), F32),
                        pltpu.VMEM((2, grp, CHUNK_A, LANES), F32),
                        pltpu.VMEM((2, grp, CHUNK_A, LANES), F32),
                        pltpu.VMEM((2, grp, CHUNK_A, LANES), F32),
                        pltpu.VMEM((2, grp, DH_A, DH_A), F32),
                        pltpu.VMEM((2, grp, SUBLANES, LANES), F32),
                        pltpu.VMEM((2, 3 * LANES, 2 * LANES), BF16)],
        compiler_params=_cparams(("arbitrary", "arbitrary")),
        name="mlstm",
    )(q, k, v, xc, proj_g, gcol, norm_g.reshape(1, D_A), skip.reshape(1, D_A))


FFT_N2 = 128
HY_CT = 128
HY_PASSES = 1
FFT_ROWS_UNROLL = 32
FFT_SLAB_UNROLL = 4


def _split_bf16(x):
    hi = x.astype(BF16)
    lo = (x - hi.astype(F32)).astype(BF16)
    return hi, lo


def _dot3(a_hi, a_lo, x):
    x_hi, x_lo = _split_bf16(x)
    d = lambda a, b: jnp.dot(a, b, preferred_element_type=F32)
    if HY_PASSES == 1:
        return d(a_hi, x_hi)
    return d(a_hi, x_hi) + (d(a_lo, x_hi) + d(a_hi, x_lo))


def _dft_constants(seq):
    n = 2 * seq
    n2 = FFT_N2
    n1 = n // n2
    f1 = np.arange(n1)[:, None]
    s1 = np.arange(n1)[None, :]
    th1 = 2.0 * np.pi * f1 * s1 / n1
    fwd1 = np.concatenate([np.cos(th1), -np.sin(th1)], axis=0)
    inv1 = np.concatenate([np.cos(th1), -np.sin(th1)], axis=1)[:n1 // 2] / n
    a = np.arange(n2)
    th2 = 2.0 * np.pi * a[:, None] * a[None, :] / n2
    c2, s2 = np.cos(th2), np.sin(th2)
    fwd2 = np.block([[c2, s2], [-s2, c2]])
    inv2 = np.block([[c2, -s2], [s2, c2]])
    tht = 2.0 * np.pi * a[:, None] * np.arange(n1)[None, :] / n
    tw = np.zeros((2, n2, LANES), np.float64)
    tw[0, :, :n1] = np.cos(tht)
    tw[1, :, :n1] = -np.sin(tht)

    def hl(m):
        m32 = jnp.asarray(m, F32)
        hi = m32.astype(BF16)
        lo = (m32 - hi.astype(F32)).astype(BF16)
        return jnp.stack([hi, lo])

    nf = n1 // 2 + 1
    nfp = -(-nf // SUBLANES) * SUBLANES
    keep = np.zeros((nfp, 1))
    keep[:nf] = 1.0
    thh = 2.0 * np.pi * np.arange(nfp)[:, None] * s1 / n1
    fwd1h = np.concatenate([np.cos(thh) * keep, -np.sin(thh) * keep], axis=0)
    wgt = 2.0 * keep
    wgt[0] = wgt[nf - 1] = 1.0
    inv1h = np.concatenate([(np.cos(thh) * wgt).T, (-np.sin(thh) * wgt).T], axis=1)[:n1 // 2] / n
    return dict(fwd1=hl(fwd1), inv1=hl(inv1), fwd2=hl(fwd2), inv2=hl(inv2),
                fwd1h=hl(fwd1h), inv1h=hl(inv1h), tw=jnp.asarray(tw, F32), n1=n1)


def _fft_stage_a(load_rows, k_rows, fwd1_ref, a_ref):
    n2 = FFT_N2
    rows = fwd1_ref.shape[1]
    f_hi = fwd1_ref[0, :, :k_rows]
    f_lo = fwd1_ref[1, :, :k_rows]

    def one(s2):
        m = _dot3(f_hi, f_lo, load_rows(s2))
        yield
        a_ref[pl.ds(pl.multiple_of(s2 * rows, SUBLANES), rows), :] = m

    def body(i, carry):
        _round_robin([one(i * FFT_ROWS_UNROLL + u) for u in range(FFT_ROWS_UNROLL)])
        return carry

    lax.fori_loop(0, n2 // FFT_ROWS_UNROLL, body, 0)


def _twiddle_cols(tw_ref, f1):
    lane = lax.broadcasted_iota(jnp.int32, (FFT_N2, LANES), 1)
    sel = lane == f1
    twr = jnp.sum(jnp.where(sel, tw_ref[0], 0.0), axis=-1, keepdims=True)
    twi = jnp.sum(jnp.where(sel, tw_ref[1], 0.0), axis=-1, keepdims=True)
    return twr, twi


def _hyena_filter_kernel(feat_ref, w1_ref, b1_ref, w2_ref, b2_ref, w3_ref, b3_ref, fr_ref,
                         wo_ref, dec_ref, fwd1_ref, fwd2_ref, tw_ref, o_ref, kt_ref, a_ref, hid_ref):
    n = feat_ref.shape[0]
    seq = n // 2
    n2 = FFT_N2
    n1 = n // n2
    tile = 512
    freq = fr_ref[...]

    @pl.when(pl.program_id(0) == 0)
    def _():
        def hidden(i, carry):
            r0 = pl.multiple_of(i * tile, tile)
            hid = jnp.sin(freq * (_hdot(feat_ref[pl.ds(r0, tile), :], w1_ref[...]) + b1_ref[...]))
            hid = jnp.sin(freq * (_hdot(hid, w2_ref[...]) + b2_ref[...]))
            hid_ref[pl.ds(r0, tile), :] = jnp.sin(freq * (_hdot(hid, w3_ref[...]) + b3_ref[...]))
            return carry

        lax.fori_loop(0, n // tile, hidden, 0)

    def gen(i, carry):
        r0 = pl.multiple_of(i * tile, tile)
        second = r0 >= seq
        wo = jnp.where(second, wo_ref[1, 0], wo_ref[0, 0])
        dec = jnp.where(second, dec_ref[1, 0], dec_ref[0, 0])
        t = feat_ref[pl.ds(r0, tile), 0:1]
        filt = _hdot(hid_ref[pl.ds(r0, tile), :], wo) * jnp.exp(-t * jnp.abs(dec))
        rows = r0 + lax.broadcasted_iota(jnp.int32, (tile, 1), 0)
        kt_ref[pl.ds(r0, tile), :] = jnp.where(rows == seq, 0.0, filt)
        return carry

    lax.fori_loop(0, n // tile, gen, 0)

    _fft_stage_a(lambda s2: kt_ref[pl.ds(s2, n1, stride=n2), :], n1, fwd1_ref, a_ref)

    nf = n1 // 2 + 1
    nfp = fwd1_ref.shape[1] // 2

    def slab(f1):
        r0 = f1 * 2 * n2 if isinstance(f1, int) else pl.multiple_of(f1 * 2 * n2, 2 * n2)
        ar = a_ref[pl.ds(f1, n2, stride=2 * nfp), :]
        ai = a_ref[pl.ds(nfp + f1, n2, stride=2 * nfp), :]
        twr, twi = _twiddle_cols(tw_ref, f1)
        x = jnp.concatenate([ar * twr - ai * twi, ar * twi + ai * twr], axis=0)
        xf = _dot3(fwd2_ref[0], fwd2_ref[1], x)
        yield
        o_ref[0, pl.ds(r0, 2 * n2), :] = xf

    def slabs(i, carry):
        _round_robin([slab(i * FFT_SLAB_UNROLL + u) for u in range(FFT_SLAB_UNROLL)])
        return carry

    lax.fori_loop(0, nf // FFT_SLAB_UNROLL, slabs, 0)
    _round_robin([slab(f1) for f1 in range(nf - nf % FFT_SLAB_UNROLL, nf)])


def _hyena_filter_spectrum(consts, feats_full, w1p, b1, w2, b2, w3, b3, freq, w_out, decay):
    n = feats_full.shape[0]
    nct = D_B // HY_CT
    spec_rows = (consts['n1'] // 2 + 1) * 2 * FFT_N2
    full = lambda a: pl.BlockSpec(a.shape, lambda j: (0,) * a.ndim)
    wo = w_out.reshape(HY_HID, 2, nct, HY_CT).transpose(1, 2, 0, 3)
    dec = decay.reshape(2, nct, 1, HY_CT)
    small = (w1p, b1.reshape(1, -1), w2, b2.reshape(1, -1), w3, b3.reshape(1, -1), freq.reshape(1, -1))
    return pl.pallas_call(
        _hyena_filter_kernel,
        grid=(nct,),
        in_specs=[full(feats_full)] + [full(a) for a in small] + [
            pl.BlockSpec((2, 1, HY_HID, HY_CT), lambda j: (0, j, 0, 0)),
            pl.BlockSpec((2, 1, 1, HY_CT), lambda j: (0, j, 0, 0)),
            full(consts['fwd1h']), full(consts['fwd2']), full(consts['tw']),
        ],
        out_specs=pl.BlockSpec((1, spec_rows, HY_CT), lambda j: (j, 0, 0)),
        out_shape=jax.ShapeDtypeStruct((nct, spec_rows, HY_CT), F32),
        scratch_shapes=[pltpu.VMEM((n, HY_CT), F32), pltpu.VMEM((2 * n, HY_CT), F32),
                        pltpu.VMEM((n, HY_HID), F32)],
        compiler_params=_cparams(("arbitrary",)),
        name="hyena_filter",
    )(feats_full, *small, wo, dec, consts['fwd1h'], consts['fwd2'], consts['tw'])


def _hyena_conv_kernel(z_ref, x0_ref, zb_ref, ks_ref, bias_ref, fwd1_ref, inv1_ref, fwd2_ref,
                       inv2_ref, tw_ref, o_ref, a_ref, b_ref, y_ref):
    seq = z_ref.shape[1]
    n2 = FFT_N2
    n1 = 2 * seq // n2
    nh = n1 // 2
    nf = nh + 1
    nfp = fwd1_ref.shape[1] // 2

    _fft_stage_a(lambda s2: z_ref[0, pl.ds(s2, nh, stride=n2), :], nh, fwd1_ref, a_ref)
    b_ref[pl.ds(nf * 2 * n2, (nfp - nf) * 2 * n2), :] = jnp.zeros(((nfp - nf) * 2 * n2, b_ref.shape[1]), F32)

    def slab(f1):
        r0 = f1 * 2 * n2 if isinstance(f1, int) else pl.multiple_of(f1 * 2 * n2, 2 * n2)
        ar = a_ref[pl.ds(f1, n2, stride=2 * nfp), :]
        ai = a_ref[pl.ds(nfp + f1, n2, stride=2 * nfp), :]
        twr, twi = _twiddle_cols(tw_ref, f1)
        x = jnp.concatenate([ar * twr - ai * twi, ar * twi + ai * twr], axis=0)
        xf = _dot3(fwd2_ref[0], fwd2_ref[1], x)
        yield
        xr, xi = xf[:n2], xf[n2:]
        kr = ks_ref[0, pl.ds(r0, n2), :]
        ki = ks_ref[0, pl.ds(r0 + n2, n2), :]
        y = jnp.concatenate([xr * kr - xi * ki, xr * ki + xi * kr], axis=0)
        bb = _dot3(inv2_ref[0], inv2_ref[1], y)
        yield
        br, bi = bb[:n2], bb[n2:]
        b_ref[pl.ds(r0, n2), :] = br * twr + bi * twi
        b_ref[pl.ds(r0 + n2, n2), :] = bi * twr - br * twi

    def slabs(i, carry):
        _round_robin([slab(i * FFT_SLAB_UNROLL + u) for u in range(FFT_SLAB_UNROLL)])
        return carry

    lax.fori_loop(0, nf // FFT_SLAB_UNROLL, slabs, 0)
    _round_robin([slab(f1) for f1 in range(nf - nf % FFT_SLAB_UNROLL, nf)])

    i_hi = inv1_ref[0]
    i_lo = inv1_ref[1]

    def stage_d(s2):
        br = b_ref[pl.ds(s2, nfp, stride=2 * n2), :]
        bi = b_ref[pl.ds(n2 + s2, nfp, stride=2 * n2), :]
        y = _dot3(i_hi, i_lo, jnp.concatenate([br, bi], axis=0))
        yield
        y_ref[pl.ds(s2, nh, stride=n2), :] = y

    def stage_ds(i, carry):
        _round_robin([stage_d(i * FFT_ROWS_UNROLL + u) for u in range(FFT_ROWS_UNROLL)])
        return carry

    lax.fori_loop(0, n2 // FFT_ROWS_UNROLL, stage_ds, 0)

    tile = 256
    def finish(i, carry):
        r0 = pl.multiple_of(i * tile, tile)
        z = z_ref[0, pl.ds(r0, tile), :]
        y = y_ref[pl.ds(r0, tile), :] + bias_ref[...] * z
        o_ref[0, pl.ds(r0, tile), :] = x0_ref[0, pl.ds(r0, tile), :] * y * _silu(zb_ref[pl.ds(r0, tile), :].astype(F32))
        return carry

    lax.fori_loop(0, seq // tile, finish, 0)


def _hyena_conv(consts, z, x0, proj_g, kspec, bias):
    bsz, seq, _ = z.shape
    nct = D_B // HY_CT
    full = lambda a: pl.BlockSpec(a.shape, lambda j, b: (0,) * a.ndim)
    cs = pl.BlockSpec((1, seq, HY_CT), lambda j, b: (b, 0, j))
    mats = (consts['fwd1h'], consts['inv1h'], consts['fwd2'], consts['inv2'], consts['tw'])
    stage_rows = consts['fwd1h'].shape[1]
    return pl.pallas_call(
        _hyena_conv_kernel,
        grid=(nct, bsz),
        in_specs=[
            cs, cs,
            pl.BlockSpec((seq, HY_CT), lambda j, b: (b, G_BZ // HY_CT + j)),
            pl.BlockSpec((1, kspec.shape[1], HY_CT), lambda j, b: (j, 0, 0)),
            pl.BlockSpec((1, HY_CT), lambda j, b: (0, j)),
        ] + [full(m) for m in mats],
        out_specs=cs,
        out_shape=jax.ShapeDtypeStruct((bsz, seq, D_B), F32),
        scratch_shapes=[pltpu.VMEM((FFT_N2 * stage_rows, HY_CT), F32),
                        pltpu.VMEM((stage_rows * FFT_N2, HY_CT), F32),
                        pltpu.VMEM((seq, HY_CT), F32)],
        compiler_params=_cparams(("arbitrary", "arbitrary")),
        name="hyena_conv",
    )(z, x0, proj_g, kspec, bias.reshape(1, D_B), *mats)


def _rwkv_kernel(r_ref, k_ref, v_ref, ul_ref, cz_ref, w0_ref, w2_ref, a0_ref, a2_ref,
                 kks_ref, ka_ref, rk_ref, lg_ref, lb_ref, o_ref, y_ref, kb_ref, s_ref,
                 wr_ref, u_ref, arb_ref, y0_ref, bt_ref, kv_ref, egc_ref,
                 kk_ref, lw_ref, be_ref, kd_ref):
    seq = r_ref.shape[1]
    L = CHUNK_C
    nc = seq // L
    P2 = 2 * L
    lane = lax.broadcasted_iota(jnp.int32, (L, LANES), 1)
    head0 = lane < HEAD_C
    prow = lax.broadcasted_iota(jnp.int32, (P2, P2), 0)
    pcol = lax.broadcasted_iota(jnp.int32, (P2, P2), 1)
    bd = (prow // HEAD_C) == (pcol // HEAD_C)
    blk = jnp.where(bd, 1.0, 0.0).astype(F32)
    trow = lax.broadcasted_iota(jnp.int32, (L, P2), 0)
    scol = lax.broadcasted_iota(jnp.int32, (L, P2), 1) % L
    eye = jnp.where(trow == scol, 1.0, 0.0).astype(F32)
    crow = lax.broadcasted_iota(jnp.int32, (L, L), 0)
    ccol = lax.broadcasted_iota(jnp.int32, (L, L), 1)

    def stack(x):
        return jnp.concatenate([jnp.where(head0, x, 0.0), jnp.where(head0, 0.0, x)], axis=0)

    blk_b = blk.astype(BF16)
    grp = wr_ref.shape[1]

    def dot(a, b):
        return jnp.dot(a, b, preferred_element_type=F32)

    def each(f, *cols):
        return [f(*xs) for xs in zip(*cols)]

    def prepare(probs):
        dirs = [d for _, d, _ in probs]
        fwd = [d == 0 for d in dirs]
        strict = [(scol < trow) if f else (scol > trow) for f in fwd]
        incl = [(scol <= trow) if f else (scol >= trow) for f in fwd]
        cum = [jnp.where((ccol <= crow) if f else (ccol >= crow), 1.0, 0.0).astype(BF16) for f in fwd]
        c0 = [pl.multiple_of(j * L, L) for j, _, _ in probs]
        r = [r_ref[0, pl.ds(c, L), :] for c in c0]
        v = [v_ref[0, pl.ds(c, L), :] for c in c0]
        kk = [kk_ref[pl.ds(c, L), :] for c in c0]
        lwt = [lw_ref[d, pl.ds(c, L), :] for c, d in zip(c0, dirs)]
        be = [be_ref[d, pl.ds(c, L), :] for c, d in zip(c0, dirs)]
        kd = [kd_ref[d, pl.ds(c, L), :] for c, d in zip(c0, dirs)]

        def cumsum(cm, x):
            l1 = x.astype(BF16)
            e1 = x - l1.astype(F32)
            l2 = e1.astype(BF16)
            l3 = (e1 - l2.astype(F32)).astype(BF16)
            return dot(jnp.concatenate([cm, cm, cm], axis=1), jnp.concatenate([l1, l2, l3], axis=0))

        g = each(cumsum, cum, lwt)
        yield
        eng = [jnp.exp(-x) for x in g]
        egc = [jnp.exp(x[L - 1:L, :] if f else x[0:1, :]) for x, f in zip(g, fwd)]
        at = each(lambda x, y, z: (-x * jnp.exp(y - z)).astype(BF16), kk, g, lwt)
        rt_ = each(lambda x, y: (x * jnp.exp(y)).astype(BF16), r, g)
        bh = each(lambda x, y: x * y, be, eng)
        kh = each(lambda x, y: x * y, kd, eng)
        lhs = each(lambda x, y: jnp.concatenate([x, y], axis=0), at, rt_)
        rhs = each(lambda x, y: jnp.concatenate([stack(x), stack(y)], axis=0).astype(BF16), bh, kh)
        g1 = each(lambda x, y: lax.dot_general(x, y, (((1,), (1,)), ((), ())),
                                               preferred_element_type=F32), lhs, rhs)
        yield
        aab = each(lambda m, x: jnp.where(m, x[:L, :P2], 0.0), strict, g1)
        aak_ark = each(lambda ms, mi, x: jnp.concatenate(
            [jnp.where(ms, x[:L, P2:], 0.0), jnp.where(mi, x[L:, P2:], 0.0)], axis=0).astype(BF16),
            strict, incl, g1)
        for (_, d, slot), mi, x in zip(probs, incl, g1):
            arb_ref[d, slot] = jnp.where(mi, x[L:, :P2], 0.0).astype(BF16)
        t = [eye + x for x in aab]
        p = [x.astype(BF16) for x in aab]
        vs = [stack(x).astype(BF16) for x in v]
        av = each(dot, aak_ark, vs)
        p = [dot(x, stack(x)) for x in p]
        yield
        p = [x.astype(BF16) for x in p]
        sq = 2
        while sq < L:
            if 2 * sq >= L:
                pt = each(lambda x, y: dot(x, stack(y.astype(BF16))), p, t)
                yield
                t = each(lambda y, x: y + x, t, pt)
            else:
                pt = each(lambda x, y: dot(x, jnp.concatenate([stack(x), stack(y.astype(BF16))], axis=1)),
                          p, t)
                yield
                p = [x[:, :P2].astype(BF16) for x in pt]
                t = each(lambda y, x: y + x[:, P2:], t, pt)
            sq *= 2
        wu = each(lambda x, y, z: dot(x.astype(BF16), jnp.concatenate(
            [stack(y), stack(z[:L].astype(BF16))], axis=1)), t, at, av)
        kv = each(lambda x, y, z: lax.dot_general(
            x.astype(BF16), (y * z).astype(BF16), (((0,), (0,)), ((), ())), preferred_element_type=F32),
            v, kh, egc)
        yield
        for i, (_, d, slot) in enumerate(probs):
            wr_ref[d, slot] = jnp.concatenate([wu[i][:, :P2].astype(BF16), rt_[i]], axis=0)
            u_ref[d, slot] = wu[i][:, P2:].astype(BF16)
            y0_ref[d, slot] = av[i][L:]
            bt_ref[d, slot] = (bh[i] * egc[i]).astype(BF16)
            kv_ref[d, slot] = jnp.where(bd, kv[i], 0.0).astype(BF16)
            egc_ref[d, slot] = egc[i]

    def advance(steps):
        s = [s_ref[d] for d in range(2)]
        for probs in steps:
            x = [lax.dot_general(wr_ref[d, slot], s[d].astype(BF16), (((1,), (1,)), ((), ())),
                                 preferred_element_type=F32) for _, d, slot in probs]
            yield
            sa = [(y[:L] + u_ref[d, slot].astype(F32)).astype(BF16) for (_, d, slot), y in zip(probs, x)]
            upd = [lax.dot_general(y, bt_ref[d, slot], (((0,), (0,)), ((), ())),
                                   preferred_element_type=F32) for (_, d, slot), y in zip(probs, sa)]
            z = [dot(arb_ref[d, slot], stack(y)) for (_, d, slot), y in zip(probs, sa)]
            yield
            for i, (j, d, slot) in enumerate(probs):
                s[d] = (s[d] * egc_ref[d, slot] + kv_ref[d, slot].astype(F32)
                        + jnp.where(bd, upd[i], 0.0))
                c0 = pl.multiple_of(j * L, L)
                y_ref[pl.ds(c0, L), :] += x[i][L:] + z[i] + y0_ref[d, slot]
        for d in range(2):
            s_ref[d] = s[d]

    s_ref[...] = jnp.zeros_like(s_ref)
    y_ref[...] = jnp.zeros_like(y_ref)
    tile = 256

    def pointwise(r0):
        k = k_ref[0, pl.ds(r0, tile), :]
        lwla = ul_ref[0, pl.ds(r0, tile), :]
        th = jnp.tanh(lwla[:, :LANES]).astype(BF16)
        la = lwla[:, LANES:].astype(BF16)
        xw = [w0_ref[d:d + 1, :] + dot(th, w2_ref[d].astype(BF16)) for d in range(2)]
        aa = [a0_ref[d:d + 1, :] + dot(la, a2_ref[d].astype(BF16)) for d in range(2)]
        kkv = k * kks_ref[...]
        ss = dot((kkv * kkv).astype(BF16), blk_b)
        yield
        kk = kkv * lax.rsqrt(jnp.maximum(ss, 1e-24))
        kk_ref[pl.ds(r0, tile), :] = kk
        kb = None
        for d in range(2):
            lw_ref[d, pl.ds(r0, tile), :] = -math.exp(-0.5) * _sigmoid(xw[d])
            a = _sigmoid(aa[d])
            kd = k * (1.0 + (a - 1.0) * ka_ref[...])
            be_ref[d, pl.ds(r0, tile), :] = kk * a
            kd_ref[d, pl.ds(r0, tile), :] = kd
            kb = kd if kb is None else kb + kd
        kb_ref[pl.ds(r0, tile), :] = kb

    pw_unroll = 2 if (seq // tile) % 2 == 0 else 1

    def pointwise_loop(i, carry):
        _round_robin([pointwise(pl.multiple_of((i * pw_unroll + u) * tile, tile))
                      for u in range(pw_unroll)])
        return carry

    lax.fori_loop(0, seq // tile // pw_unroll, pointwise_loop, 0)

    lock = grp // 2
    nsets = nc // lock

    def chunk_set(i):
        base = (i % 2) * lock
        return [[(i * lock + q, 0, base + q), (nc - 1 - (i * lock + q), 1, base + q)]
                for q in range(lock)]

    _round_robin([prepare(sum(chunk_set(0), []))])

    def pipelined(i, carry):
        _round_robin([prepare(sum(chunk_set(i + 1), [])), advance(chunk_set(i))])
        return carry

    lax.fori_loop(0, nsets - 1, pipelined, 0)
    _round_robin([advance(chunk_set(nsets - 1))])

    tile = 256
    inv_n = 1.0 / HEAD_C
    blk2 = jnp.concatenate([blk_b, blk_b], axis=0)

    def head_sum(x):
        hi, lo = _split_bf16(x)
        return dot(jnp.concatenate([hi, lo], axis=1), blk2)

    def finish(r0):
        y = y_ref[pl.ds(r0, tile), :]
        mu = head_sum(y) * inv_n
        r = r_ref[0, pl.ds(r0, tile), :]
        v = v_ref[0, pl.ds(r0, tile), :]
        kbon = 0.5 * kb_ref[pl.ds(r0, tile), :]
        bonus = head_sum(r * kbon * rk_ref[...]) * v
        yield
        yc = y - mu
        var = head_sum(yc * yc) * inv_n
        yield
        yn = yc * lax.rsqrt(var + LNX_EPS) * lg_ref[...] + lb_ref[...]
        o_ref[0, pl.ds(r0, tile), :] = (yn + bonus) * _silu(cz_ref[pl.ds(r0, tile), :].astype(F32))

    unroll = 4 if (seq // tile) % 4 == 0 else 1

    def finishes(i, carry):
        _round_robin([finish(pl.multiple_of((i * unroll + u) * tile, tile)) for u in range(unroll)])
        return carry

    lax.fori_loop(0, seq // tile // unroll, finishes, 0)


def _rwkv(u, ul, proj_g, w0, w2pad, a0, a2pad, kk_s, ka, rk, lnx_g, lnx_b):
    bsz, seq, _ = u.shape
    npair = D_C // LANES
    grp = min(RW_GROUP, seq // CHUNK_C)
    col = lambda off: pl.BlockSpec((1, seq, LANES), lambda b, p: (b, 0, off + p))
    vec = pl.BlockSpec((1, LANES), lambda b, p: (0, p))
    two = pl.BlockSpec((2, LANES), lambda b, p: (0, p))
    lora = pl.BlockSpec((2, LANES, LANES), lambda b, p: (0, 0, p))
    row = lambda a: a.reshape(1, D_C)
    return pl.pallas_call(
        _rwkv_kernel,
        grid=(bsz, npair),
        in_specs=[
            col(0), col(npair), col(2 * npair),
            pl.BlockSpec((1, seq, 2 * LANES), lambda b, p: (b, 0, 0)),
            pl.BlockSpec((seq, LANES), lambda b, p: (b, G_CZ // LANES + p)),
            two, lora, two, lora, vec, vec, vec, vec, vec,
        ],
        out_specs=pl.BlockSpec((1, seq, LANES), lambda b, p: (b, 0, p)),
        out_shape=jax.ShapeDtypeStruct((bsz, seq, D_C), F32),
        scratch_shapes=[pltpu.VMEM((seq, LANES), F32), pltpu.VMEM((seq, LANES), F32),
                        pltpu.VMEM((2, LANES, LANES), F32),
                        pltpu.VMEM((2, grp, 2 * CHUNK_C, LANES), BF16),
                        pltpu.VMEM((2, grp, CHUNK_C, LANES), BF16),
                        pltpu.VMEM((2, grp, CHUNK_C, LANES), BF16),
                        pltpu.VMEM((2, grp, CHUNK_C, LANES), F32),
                        pltpu.VMEM((2, grp, CHUNK_C, LANES), BF16),
                        pltpu.VMEM((2, grp, LANES, LANES), BF16),
                        pltpu.VMEM((2, grp, 1, LANES), F32),
                        pltpu.VMEM((seq, LANES), F32), pltpu.VMEM((2, seq, LANES), F32),
                        pltpu.VMEM((2, seq, LANES), F32), pltpu.VMEM((2, seq, LANES), F32)],
        compiler_params=_cparams(("arbitrary", "arbitrary")),
        name="rwkv7",
    )(u, u, u, ul, proj_g, w0, w2pad, a0, a2pad, row(kk_s), row(ka), row(rk), row(lnx_g), row(lnx_b))


def _out_kernel(ya_ref, yb_ref, yc_ref, ga_ref, gb_ref, gc_ref, x_ref, gate_ref, pg_ref,
                wa_ref, wb_ref, wc_ref, wo_ref, o_ref):
    merged = (_sigmoid(ga_ref[...].astype(F32)) * _bdot(ya_ref[...], wa_ref[...])
              + _sigmoid(gb_ref[...].astype(F32)) * _bdot(yb_ref[...], wb_ref[...])
              + _sigmoid(gc_ref[...].astype(F32)) * _bdot(yc_ref[...], wc_ref[...]))
    out = _bdot(merged, wo_ref[...])
    y = out * lax.rsqrt(jnp.mean(out * out, axis=-1, keepdims=True) + NORM_EPS) * pg_ref[...]
    o_ref[...] = x_ref[...] + gate_ref[0] * y


def _merge_out(ya, yb, yc, proj_g, x2d, seq, gate, post_g, wa, wb, wc, wo, tm):
    rows, d = x2d.shape
    per_b = seq // tm
    full = lambda a: pl.BlockSpec(a.shape, lambda i: (0,) * a.ndim)
    gcol = lambda off: pl.BlockSpec((tm, d), lambda i: (i, off // d))
    return pl.pallas_call(
        _out_kernel,
        grid=(rows // tm,),
        in_specs=[
            pl.BlockSpec((tm, D_A), lambda i: (i, 0)),
            pl.BlockSpec((tm, D_B), lambda i: (i, 0)),
            pl.BlockSpec((tm, D_C), lambda i: (i, 0)),
            gcol(G_GA), gcol(G_GB), gcol(G_GC),
            pl.BlockSpec((tm, d), lambda i: (i, 0)),
            pl.BlockSpec((1, 1, d), lambda i: (i // per_b, 0, 0)),
            pl.BlockSpec((1, d), lambda i: (0, 0)),
            full(wa), full(wb), full(wc), full(wo),
        ],
        out_specs=pl.BlockSpec((tm, d), lambda i: (i, 0)),
        out_shape=jax.ShapeDtypeStruct((rows, d), F32),
        compiler_params=_cparams(("arbitrary",)),
        name="merge_out",
    )(ya.reshape(rows, D_A), yb.reshape(rows, D_B), yc.reshape(rows, D_C), proj_g, proj_g, proj_g,
      x2d, gate, post_g.reshape(1, d), wa, wb, wc, wo)


def _hyena_feats(seq):
    n = 2 * seq
    t = jnp.linspace(0.0, 1.0, seq, dtype=F32)
    w = 2.0 * math.pi * jnp.arange(seq, dtype=F32) / seq
    f = jnp.linspace(1e-4, HY_BANDS - 1, HY_BANDS, dtype=F32)
    zz = w[:, None] * f[None, :]
    feats = jnp.concatenate([t[:, None], jnp.cos(zz), -jnp.sin(zz)], axis=-1)
    pos = np.concatenate([np.arange(seq), [0], np.arange(seq - 1, 0, -1)])
    full = jnp.zeros((n, LANES), F32).at[:, :feats.shape[1]].set(feats[pos])
    return full


def _tiles(seq):
    return min(seq, 1024), min(seq, 256), min(seq, 512)


def kernel(x, c, ada_w, ada_b, pre_g, post_g, w_in, ml_conv_w, ml_conv_b, ml_wq, ml_wk, ml_wv, ml_gate_w, ml_gate_b, ml_norm_g, ml_skip, hy_conv_w, hy_conv_b, hy_w1, hy_b1, hy_w2, hy_b2, hy_w3, hy_b3, hy_freq, hy_w_out, hy_decay, hy_bias, rw_mu, rw_w0, rw_w2, rw_a0, rw_a2, rw_kk, rw_ka, rw_rk, rw_lnx_g, rw_lnx_b, w_branch_a, w_branch_b, w_branch_c, w_out):
    bsz, seq, d = x.shape
    depth = ada_w.shape[0]
    tm_in, ts_prep, tm_out = _tiles(seq)
    runs_h, runs_g = (_contiguous_runs(p) for p in _ref_column_perm())
    consts = _dft_constants(seq)
    feats_full = _hyena_feats(seq)
    mod = _modulation(c, ada_w, ada_b)
    x2d = x.reshape(bsz * seq, d)
    n_gate = 4 * H_A
    for l in range(depth):
        shift = mod[l, :, None, 0:d]
        scale = mod[l, :, None, d:2 * d]
        gate = mod[l, :, None, 2 * d:3 * d]
        w_h = jnp.concatenate([w_in[l][:, a:b] for a, b in runs_h], axis=1).astype(BF16)
        w_g = jnp.concatenate([w_in[l][:, a:b] for a, b in runs_g], axis=1).astype(BF16)
        proj_h = _inproj(x2d, seq, pre_g[l], shift, scale, w_h, tm_in, N_H // 3)
        proj_g = _inproj(x2d, seq, pre_g[l], shift, scale, w_g, tm_in, N_G // 4)

        gw = jnp.zeros((3 * D_A, LANES), F32).at[:, :n_gate].set(ml_gate_w[l])
        gb = jnp.zeros((1, LANES), F32).at[0, :n_gate].set(ml_gate_b[l])
        q, k, v, xc, gcol, z, x0, u, ul = _prep(
            proj_h, bsz, seq, ts_prep, ml_conv_w[l], ml_conv_b[l].reshape(1, D_A),
            ml_wq[l], ml_wk[l], ml_wv[l], gw, gb,
            hy_conv_w[l], hy_conv_b[l].reshape(1, 3 * D_B), rw_mu[l].reshape(1, -1))

        y_a = _mlstm(q, k, v, xc, proj_g, gcol, ml_norm_g[l], ml_skip[l])

        w1p = jnp.zeros((LANES, HY_HID), F32).at[:hy_w1.shape[1]].set(hy_w1[l])
        kspec = _hyena_filter_spectrum(consts, feats_full, w1p, hy_b1[l], hy_w2[l], hy_b2[l],
                                       hy_w3[l], hy_b3[l], hy_freq[l], hy_w_out[l], hy_decay[l])
        y_b = _hyena_conv(consts, z, x0, proj_g, kspec, hy_bias[l])

        w2pad = (jnp.zeros((2, LANES, D_C), F32).at[0, :LORA].set(rw_w2[l, 0])
                 .at[1, LORA:].set(rw_w2[l, 1]))
        a2pad = (jnp.zeros((2, LANES, D_C), F32).at[0, :LORA].set(rw_a2[l, 0])
                 .at[1, LORA:].set(rw_a2[l, 1]))
        y_c = _rwkv(u, ul, proj_g, rw_w0[l], w2pad, rw_a0[l], a2pad, rw_kk[l], rw_ka[l],
                    rw_rk[l].reshape(-1), rw_lnx_g[l], rw_lnx_b[l])

        x2d = _merge_out(y_a, y_b, y_c, proj_g, x2d, seq, gate, post_g[l],
                         w_branch_a[l].astype(BF16), w_branch_b[l].astype(BF16),
                         w_branch_c[l].astype(BF16), w_out[l].astype(BF16), tm_out)
    return x2d.reshape(bsz, seq, d)
```

```python
import functools
import math

import numpy as np
import jax
import jax.numpy as jnp
from jax import lax
from jax.experimental import pallas as pl
from jax.experimental.pallas import tpu as pltpu

D_MODEL = 1024
DEPTH = 4
D_A = 512
H_A = 4
DH_A = 128
CHUNK_A = 64
ML_GROUP = 16
D_B = 512
HY_BANDS = 16
HY_HID = 64
D_C = 1024
HEAD_C = 64
H_C = D_C // HEAD_C
LORA = 64
CHUNK_C = 64
RW_GROUP = 8
LNX_EPS = 64e-5
NORM_EPS = 1e-6
HEAD_NORM_EPS = 1e-5

LANES = 128
SUBLANES = 8
VMEM_LIMIT = 56 * 1024 * 1024

F32 = jnp.float32
BF16 = jnp.bfloat16
PROJ_DTYPE = BF16
PROJ_ROWS = 16
HIGHEST = lax.Precision.HIGHEST

N_H = D_A + 3 * D_B + 3 * D_C + 4 * LORA
N_G = D_A + D_B + D_C + 3 * D_MODEL
H_AX, H_BV, H_BX0, H_BX1, H_CR, H_CK, H_CV, H_LW, H_LA = (
    0, 512, 1024, 1536, 2048, 3072, 4096, 5120, 5248)
G_AZ, G_BZ, G_CZ, G_GA, G_GB, G_GC = 0, 512, 1024, 2048, 3072, 4096


def _ref_column_perm():
    r = lambda a, n: np.arange(a, a + n)
    a_x, a_z = r(0, 512), r(512, 512)
    b_v, b_x0, b_x1, b_z = r(1024, 512), r(1536, 512), r(2048, 512), r(2560, 512)
    c_r, c_k, c_v = r(3072, 1024), r(4096, 1024), r(5120, 1024)
    c_lw, c_la = r(6144, 128), r(6272, 128)
    c_z = r(6400, 1024)
    g_a, g_b, g_c = r(7424, 1024), r(8448, 1024), r(9472, 1024)
    h = np.concatenate([a_x, b_v, b_x0, b_x1, c_r, c_k, c_v, c_lw, c_la])
    g = np.concatenate([a_z, b_z, c_z, g_a, g_b, g_c])
    return h, g


def _contiguous_runs(idx):
    cuts = np.flatnonzero(np.diff(idx) != 1) + 1
    return [(int(seg[0]), int(seg[-1]) + 1) for seg in np.split(idx, cuts)]


def _cparams(sem, **extra):
    return pltpu.CompilerParams(dimension_semantics=sem, vmem_limit_bytes=VMEM_LIMIT, **extra)


def _silu(x):
    return x * (1.0 / (1.0 + jnp.exp(-x)))


def _sigmoid(x):
    return 1.0 / (1.0 + jnp.exp(-x))


def _bdot(a, b):
    return jnp.dot(a.astype(BF16), b.astype(BF16), preferred_element_type=F32)


def _bdot_nt(a, b):
    return lax.dot_general(a.astype(BF16), b.astype(BF16), (((1,), (1,)), ((), ())),
                           preferred_element_type=F32)


def _round_robin(gens):
    gens = list(gens)
    while gens:
        alive = []
        for g in gens:
            try:
                next(g)
                alive.append(g)
            except StopIteration:
                pass
        gens = alive


def _hdot(a, b):
    return jnp.dot(a, b, preferred_element_type=F32, precision=HIGHEST)


def _mod_kernel(c_ref, w_ref, b_ref, o_ref):
    cond = _silu(c_ref[...])
    o_ref[0] = _hdot(cond, w_ref[0]) + b_ref[0]


def _modulation(c, ada_w, ada_b):
    depth, d, n3 = ada_w.shape
    bsz = c.shape[0]
    nt = n3 // d
    return pl.pallas_call(
        _mod_kernel,
        grid=(depth, nt),
        in_specs=[
            pl.BlockSpec((bsz, d), lambda l, j: (0, 0)),
            pl.BlockSpec((1, d, d), lambda l, j: (l, 0, j)),
            pl.BlockSpec((1, 1, d), lambda l, j: (l, 0, j)),
        ],
        out_specs=pl.BlockSpec((1, bsz, d), lambda l, j: (l, 0, j)),
        out_shape=jax.ShapeDtypeStruct((depth, bsz, n3), F32),
        compiler_params=_cparams(("arbitrary", "arbitrary")),
        name="adaln_mod",
    )(c, ada_w, ada_b.reshape(depth, 1, n3))


def _inproj_kernel(x_ref, g_ref, shift_ref, scale_ref, w_ref, o_ref, h_ref):
    @pl.when(pl.program_id(1) == 0)
    def _():
        x = x_ref[...]
        y = x * lax.rsqrt(jnp.mean(x * x, axis=-1, keepdims=True) + NORM_EPS)
        h = y * g_ref[...] * (1.0 + scale_ref[0]) + shift_ref[0]
        h_ref[...] = h.astype(BF16)

    o_ref[...] = jnp.dot(h_ref[...], w_ref[...], preferred_element_type=F32).astype(o_ref.dtype)


def _inproj(x2d, seq, pre_g, shift, scale, w_bf16, tm, tn):
    rows, d = x2d.shape
    n = w_bf16.shape[1]
    per_b = seq // tm
    return pl.pallas_call(
        _inproj_kernel,
        grid=(rows // tm, n // tn),
        in_specs=[
            pl.BlockSpec((tm, d), lambda i, j: (i, 0)),
            pl.BlockSpec((1, d), lambda i, j: (0, 0)),
            pl.BlockSpec((1, 1, d), lambda i, j: (i // per_b, 0, 0)),
            pl.BlockSpec((1, 1, d), lambda i, j: (i // per_b, 0, 0)),
            pl.BlockSpec((d, tn), lambda i, j: (0, j)),
        ],
        out_specs=pl.BlockSpec((tm, tn), lambda i, j: (i, j)),
        out_shape=jax.ShapeDtypeStruct((rows, n), PROJ_DTYPE),
        scratch_shapes=[pltpu.VMEM((tm, d), BF16)],
        compiler_params=_cparams(("arbitrary", "arbitrary")),
        name="inproj",
    )(x2d, pre_g.reshape(1, d), shift, scale, w_bf16)


def _log_sigmoid(x):
    return jnp.minimum(x, 0.0) - jnp.log(1.0 + jnp.exp(-jnp.abs(x)))


def _prep_kernel(main_ref, prev_ref, next_ref,
                 mcw_ref, mcb_ref, wq_ref, wk_ref, wv_ref, gw_ref, gb_ref,
                 hcw_ref, hcb_ref, mu_ref,
                 q_ref, k_ref, v_ref, xc_ref, gcol_ref, z_ref, x0_ref, u_ref, ul_ref):
    i = pl.program_id(1)
    ts = main_ref.shape[0]
    has_prev = jnp.where(i > 0, 1.0, 0.0).astype(F32)
    has_next = jnp.where(i < pl.num_programs(1) - 1, 1.0, 0.0).astype(F32)

    def neighbours(c0, cw):
        x = main_ref[:, c0:c0 + cw].astype(F32)
        row = lax.broadcasted_iota(jnp.int32, (SUBLANES, cw), 0)
        p_row = prev_ref[:, c0:c0 + cw].astype(F32)[PROJ_ROWS - 1:PROJ_ROWS] * has_prev
        n_row = next_ref[:, c0:c0 + cw].astype(F32)[0:1] * has_next
        xp = pltpu.roll(x, 1, 0)
        xn = pltpu.roll(x, ts - 1, 0)
        xp = jnp.concatenate([jnp.where(row == 0, p_row, xp[:SUBLANES]), xp[SUBLANES:]], axis=0)
        xn = jnp.concatenate([xn[:ts - SUBLANES],
                              jnp.where(row == SUBLANES - 1, n_row, xn[ts - SUBLANES:])], axis=0)
        return xp, x, xn

    def conv(c0, cw, w_ref, b_ref, w0):
        xp, x, xn = neighbours(c0, cw)
        w = w_ref[:, w0:w0 + cw]
        return xp * w[0:1] + x * w[1:2] + xn * w[2:3] + b_ref[:, w0:w0 + cw], x

    gates = jnp.zeros((ts, LANES), F32) + gb_ref[...]
    for h in range(H_A):
        sl = slice(h * DH_A, (h + 1) * DH_A)
        conv_a, xa = conv(H_AX + h * DH_A, DH_A, mcw_ref, mcb_ref, h * DH_A)
        xc = _silu(conv_a)
        xc_ref[0, :, sl] = xc
        qh = _bdot(xc, wq_ref[h])
        kh = _bdot(xc, wk_ref[h])
        vh = _bdot(xa, wv_ref[h])
        gates += (_bdot(qh, gw_ref[h * DH_A:(h + 1) * DH_A])
                  + _bdot(kh, gw_ref[D_A + h * DH_A:D_A + (h + 1) * DH_A])
                  + _bdot(vh, gw_ref[2 * D_A + h * DH_A:2 * D_A + (h + 1) * DH_A]))
        q_ref[0, :, sl] = qh.astype(BF16)
        k_ref[0, :, sl] = (kh * (DH_A ** -0.5)).astype(BF16)
        v_ref[0, :, sl] = vh.astype(BF16)
    col = lax.broadcasted_iota(jnp.int32, (ts, LANES), 1)
    rmod = lax.broadcasted_iota(jnp.int32, (ts, LANES), 0) % CHUNK_A
    lf = _log_sigmoid(gates)
    cf = lf
    cb = lf
    sh = 1
    while sh < CHUNK_A:
        cf = cf + jnp.where(rmod >= sh, pltpu.roll(cf, sh, 0), 0.0)
        cb = cb + jnp.where(rmod < CHUNK_A - sh, pltpu.roll(cb, ts - sh, 0), 0.0)
        sh *= 2
    is_ff = (col >= H_A) & (col < 2 * H_A)
    is_fb = (col >= 3 * H_A) & (col < 4 * H_A)
    gcol_ref[0] = jnp.where(is_ff, cf, jnp.where(is_fb, cb, gates))

    cw = LANES
    for j in range(D_B // cw):
        c = j * cw
        cv, _ = conv(H_BV + c, cw, hcw_ref, hcb_ref, c)
        cx1, _ = conv(H_BX1 + c, cw, hcw_ref, hcb_ref, 2 * D_B + c)
        z_ref[0, :, c:c + cw] = cv * cx1
        cx0, _ = conv(H_BX0 + c, cw, hcw_ref, hcb_ref, D_B + c)
        x0_ref[0, :, c:c + cw] = cx0

    def shifted(c):
        xp, x, xn = neighbours(H_CR + c, cw)
        mu = mu_ref[:, c:c + cw]
        return x * (1.0 - mu) + (0.5 * mu) * (xp + xn)

    for j in range(3 * D_C // cw):
        u_ref[0, :, j * cw:(j + 1) * cw] = shifted(j * cw)
    for j in range(4 * LORA // cw):
        ul_ref[0, :, j * cw:(j + 1) * cw] = shifted(3 * D_C + j * cw)


def _prep(proj_h, bsz, seq, ts, ml_conv_w, ml_conv_b, wq, wk, wv, gate_w_pad, gate_b_pad,
          hy_conv_w, hy_conv_b, rw_mu):
    ns = seq // ts
    hb = ts // PROJ_ROWS
    last_hb = bsz * seq // PROJ_ROWS - 1
    full = lambda a: pl.BlockSpec(a.shape, lambda b, i: (0,) * a.ndim)
    seq_spec = lambda w: pl.BlockSpec((1, ts, w), lambda b, i: (b, i, 0))
    params = (ml_conv_w, ml_conv_b, wq, wk, wv, gate_w_pad, gate_b_pad, hy_conv_w, hy_conv_b, rw_mu)
    outs = [(D_A, BF16), (D_A, BF16), (D_A, BF16), (D_A, F32), (LANES, F32),
            (D_B, F32), (D_B, F32), (3 * D_C, F32), (4 * LORA, F32)]
    return pl.pallas_call(
        _prep_kernel,
        grid=(bsz, ns),
        in_specs=[
            pl.BlockSpec((ts, N_H), lambda b, i: (b * ns + i, 0)),
            pl.BlockSpec((PROJ_ROWS, N_H), lambda b, i: (jnp.maximum((b * ns + i) * hb - 1, 0), 0)),
            pl.BlockSpec((PROJ_ROWS, N_H), lambda b, i: (jnp.minimum((b * ns + i + 1) * hb, last_hb), 0)),
        ] + [full(p) for p in params],
        out_specs=[seq_spec(w) for w, _ in outs],
        out_shape=[jax.ShapeDtypeStruct((bsz, seq, w), dt) for w, dt in outs],
        compiler_params=_cparams(("arbitrary", "arbitrary")),
        name="prep",
    )(proj_h, proj_h, proj_h, *params)


def _mlstm_kernel(q_ref, k_ref, v_ref, xc_ref, za_ref, gcol_ref, ng_ref, sk_ref, o_ref,
                  h_ref, ct_ref, n_ref, m_ref, num_ref, den_ref, bb_ref, mb_ref, kv_ref, sc_ref,
                  sel_ref):
    head = pl.program_id(1)
    seq = q_ref.shape[1]
    L = CHUNK_A
    nc = seq // L
    row = lax.broadcasted_iota(jnp.int32, (L, L), 0)
    colm = lax.broadcasted_iota(jnp.int32, (L, L), 1)
    lane = lax.broadcasted_iota(jnp.int32, (L, LANES), 1)

    grp = num_ref.shape[1]
    srow = lax.broadcasted_iota(jnp.int32, (SUBLANES, LANES), 0)
    ones = jnp.ones((LANES, LANES), BF16)
    ones2 = jnp.ones((2 * LANES, LANES), BF16)

    def dot(a, b):
        return jnp.dot(a, b, preferred_element_type=F32)

    def split3(x):
        x1 = x.astype(BF16)
        e1 = x - x1.astype(F32)
        x2 = e1.astype(BF16)
        return x1, x2, (e1 - x2.astype(F32)).astype(BF16)

    srow_i = lax.broadcasted_iota(jnp.int32, (3 * LANES, 2 * LANES), 0) % LANES
    scol_i = lax.broadcasted_iota(jnp.int32, (3 * LANES, 2 * LANES), 1)
    for d in range(2):
        li_lane = head + 2 * H_A * d
        want = jnp.where(scol_i < LANES, li_lane + H_A, li_lane)
        sel_ref[d] = jnp.where(srow_i == want, 1.0, 0.0).astype(BF16)

    def prepare(j, d, slot):
        fwd = d == 0
        tri = (colm <= row) if fwd else (colm >= row)
        c0 = pl.multiple_of(j * L, L)
        q = q_ref[0, pl.ds(c0, L), :]
        k = k_ref[0, pl.ds(c0, L), :]
        v = v_ref[0, pl.ds(c0, L), :]
        qk = _bdot_nt(q, k)
        bl = dot(jnp.concatenate(split3(gcol_ref[0, pl.ds(c0, L), :]), axis=1), sel_ref[d])
        yield
        b, li = bl[:, :LANES], bl[:, LANES:]
        x0, x1, x2 = split3(b)
        y0, y1, y2 = split3(li - b)
        xl = jnp.where(lane == 0, x0, jnp.where(lane == 1, x1, jnp.where(
            lane == 2, x2, jnp.where(lane < 6, 1.0, 0.0).astype(BF16))))
        yl = jnp.where(lane == 3, y0, jnp.where(lane == 4, y1, jnp.where(
            lane == 5, y2, jnp.where(lane < 3, 1.0, 0.0).astype(BF16))))
        dm = _bdot_nt(xl, yl)
        gtot = b[L - 1:L, :] if fwd else b[0:1, :]
        a = gtot - b + li
        a_max = jnp.max(a, axis=0, keepdims=True)
        wk = jnp.exp(a - a_max)
        vw = (v.astype(F32) * wk).astype(BF16)
        kv = lax.dot_general(k, vw, (((0,), (0,)), ((), ())), preferred_element_type=F32)
        kn = jnp.sum(k.astype(F32) * wk, axis=0, keepdims=True)
        sc_ref[d, slot] = jnp.where(srow == 0, kn, jnp.where(srow == 1, gtot, a_max))
        bb_ref[d, slot] = b
        yield
        dm = jnp.where(tri, dm, -jnp.inf)
        m_loc = jnp.max(dm, axis=-1, keepdims=True)
        s = qk * jnp.exp(dm - m_loc)
        num = _bdot(s, v)
        den = dot(jnp.concatenate(_split_bf16(s), axis=1), ones)
        mb_ref[d, slot] = jnp.broadcast_to(m_loc, (L, LANES))
        kv_ref[d, slot] = kv
        yield
        num_ref[d, slot] = num
        den_ref[d, slot] = den

    def advance(d, todo):
        ct, n, m = ct_ref[d], n_ref[d], m_ref[d]
        pending = []
        for j, slot in todo:
            c0 = pl.multiple_of(j * L, L)
            q = q_ref[0, pl.ds(c0, L), :]
            qc = _bdot(q, ct)
            qn = dot(jnp.concatenate(_split_bf16(q.astype(F32) * n), axis=1), ones2)
            sc = sc_ref[d, slot]
            kn, gtot, a_max = sc[0:1, :], sc[1:2, :], sc[2:3, :]
            m_new = jnp.maximum(gtot + m, a_max)
            decay = jnp.exp(gtot + m - m_new)
            beta = jnp.exp(a_max - m_new)
            pending.append((c0, slot, qc, qn, m))
            ct = decay * ct + beta * kv_ref[d, slot]
            n = decay * n + beta * kn
            m = m_new
        ct_ref[d], n_ref[d], m_ref[d] = ct, n, m
        yield
        for c0, slot, qc, qn, m_in in pending:
            inter = bb_ref[d, slot] + m_in
            m_loc = mb_ref[d, slot]
            m_t = jnp.maximum(inter, m_loc)
            c_intra = jnp.exp(m_loc - m_t)
            c_inter = jnp.exp(inter - m_t)
            num = c_intra * num_ref[d, slot] + c_inter * qc
            den = c_intra * den_ref[d, slot] + c_inter * qn
            h_ref[pl.ds(c0, L), :] += num / jnp.maximum(jnp.abs(den), jnp.exp(-m_t))

    ct_ref[...] = jnp.zeros_like(ct_ref)
    n_ref[...] = jnp.zeros_like(n_ref)
    m_ref[...] = jnp.full_like(m_ref, -jnp.inf)
    h_ref[...] = jnp.zeros_like(h_ref)
    lock = 8 if grp % 8 == 0 else 1

    def group(gi, carry):
        def phase_a(i, c):
            gens = []
            for u in range(lock):
                jf = gi * grp + i * lock + u
                gens += [prepare(jf, 0, i * lock + u), prepare(nc - 1 - jf, 1, i * lock + u)]
            _round_robin(gens)
            return c

        def phase_b(i, c):
            jf = [gi * grp + i * lock + u for u in range(lock)]
            _round_robin([advance(0, [(j, i * lock + u) for u, j in enumerate(jf)]),
                          advance(1, [(nc - 1 - j, i * lock + u) for u, j in enumerate(jf)])])
            return c

        lax.fori_loop(0, grp // lock, phase_a, 0)
        lax.fori_loop(0, grp // lock, phase_b, 0)
        return carry

    lax.fori_loop(0, nc // grp, group, 0)

    tile = 256
    inv_n = 1.0 / DH_A

    def lane_sum(x):
        return dot(x.astype(BF16), ones)

    def finish(r0):
        hh = h_ref[pl.ds(r0, tile), :]
        mu = lane_sum(hh) * inv_n
        yield
        hc = hh - mu
        var = lane_sum(hc * hc) * inv_n
        yield
        hn = hc * lax.rsqrt(var + HEAD_NORM_EPS) * ng_ref[...]
        out = (hn + sk_ref[...] * xc_ref[0, pl.ds(r0, tile), :]) * _silu(za_ref[pl.ds(r0, tile), :].astype(F32))
        o_ref[0, pl.ds(r0, tile), :] = out

    unroll = 4 if (seq // tile) % 4 == 0 else 1

    def finishes(i, carry):
        _round_robin([finish(pl.multiple_of((i * unroll + u) * tile, tile)) for u in range(unroll)])
        return carry

    lax.fori_loop(0, seq // tile // unroll, finishes, 0)


def _mlstm(q, k, v, xc, proj_g, gcol, norm_g, skip):
    bsz, seq, _ = q.shape
    grp = min(ML_GROUP, seq // CHUNK_A)
    hs = lambda: pl.BlockSpec((1, seq, DH_A), lambda b, h: (b, 0, h))
    return pl.pallas_call(
        _mlstm_kernel,
        grid=(bsz, H_A),
        in_specs=[
            hs(), hs(), hs(), hs(),
            pl.BlockSpec((seq, DH_A), lambda b, h: (b, G_AZ // DH_A + h)),
            pl.BlockSpec((1, seq, LANES), lambda b, h: (b, 0, 0)),
            pl.BlockSpec((1, DH_A), lambda b, h: (0, h)),
            pl.BlockSpec((1, DH_A), lambda b, h: (0, h)),
        ],
        out_specs=pl.BlockSpec((1, seq, DH_A), lambda b, h: (b, 0, h)),
        out_shape=jax.ShapeDtypeStruct((bsz, seq, D_A), F32),
        scratch_shapes=[pltpu.VMEM((seq, DH_A), F32), pltpu.VMEM((2, DH_A, DH_A), F32),
                        pltpu.VMEM((2, 1, DH_A), F32), pltpu.VMEM((2, 1, LANES), F32),
                        pltpu.VMEM((2, grp, CHUNK_A, DH_A), F32),
                        pltpu.VMEM((2, grp, CHUNK_A, LANES), F32),
                        pltpu.VMEM((2, grp, CHUNK_A, LANES), F32),
                        pltpu.VMEM((2, grp, CHUNK_A, LANES), F32),
                        pltpu.VMEM((2, grp, DH_A, DH_A), F32),
                        pltpu.VMEM((2, grp, SUBLANES, LANES), F32),
                        pltpu.VMEM((2, 3 * LANES, 2 * LANES), BF16)],
        compiler_params=_cparams(("arbitrary", "arbitrary")),
        name="mlstm",
    )(q, k, v, xc, proj_g, gcol, norm_g.reshape(1, D_A), skip.reshape(1, D_A))


FFT_N2 = 128
HY_CT = 128
HY_PASSES = 1
FFT_ROWS_UNROLL = 32
FFT_SLAB_UNROLL = 8


def _split_bf16(x):
    hi = x.astype(BF16)
    lo = (x - hi.astype(F32)).astype(BF16)
    return hi, lo


def _dot3(a_hi, a_lo, x):
    x_hi, x_lo = _split_bf16(x)
    d = lambda a, b: jnp.dot(a, b, preferred_element_type=F32)
    if HY_PASSES == 1:
        return d(a_hi, x_hi)
    return d(a_hi, x_hi) + (d(a_lo, x_hi) + d(a_hi, x_lo))


def _dft_constants(seq):
    n = 2 * seq
    n2 = FFT_N2
    n1 = n // n2
    f1 = np.arange(n1)[:, None]
    s1 = np.arange(n1)[None, :]
    th1 = 2.0 * np.pi * f1 * s1 / n1
    fwd1 = np.concatenate([np.cos(th1), -np.sin(th1)], axis=0)
    inv1 = np.concatenate([np.cos(th1), -np.sin(th1)], axis=1)[:n1 // 2] / n
    a = np.arange(n2)
    th2 = 2.0 * np.pi * a[:, None] * a[None, :] / n2
    c2, s2 = np.cos(th2), np.sin(th2)
    fwd2 = np.block([[c2, s2], [-s2, c2]])
    inv2 = np.block([[c2, -s2], [s2, c2]])
    tht = 2.0 * np.pi * a[:, None] * np.arange(n1)[None, :] / n
    tw = np.zeros((2, n2, LANES), np.float64)
    tw[0, :, :n1] = np.cos(tht)
    tw[1, :, :n1] = -np.sin(tht)

    def hl(m):
        m32 = jnp.asarray(m, F32)
        hi = m32.astype(BF16)
        lo = (m32 - hi.astype(F32)).astype(BF16)
        return jnp.stack([hi, lo])

    nf = n1 // 2 + 1
    nfp = -(-nf // SUBLANES) * SUBLANES
    keep = np.zeros((nfp, 1))
    keep[:nf] = 1.0
    thh = 2.0 * np.pi * np.arange(nfp)[:, None] * s1 / n1
    fwd1h = np.concatenate([np.cos(thh) * keep, -np.sin(thh) * keep], axis=0)
    wgt = 2.0 * keep
    wgt[0] = wgt[nf - 1] = 1.0
    inv1h = np.concatenate([(np.cos(thh) * wgt).T, (-np.sin(thh) * wgt).T], axis=1)[:n1 // 2] / n
    return dict(fwd1=hl(fwd1), inv1=hl(inv1), fwd2=hl(fwd2), inv2=hl(inv2),
                fwd1h=hl(fwd1h), inv1h=hl(inv1h), tw=jnp.asarray(tw, F32), n1=n1)


def _fft_stage_a(load_rows, k_rows, fwd1_ref, a_ref):
    n2 = FFT_N2
    rows = fwd1_ref.shape[1]
    f_hi = fwd1_ref[0, :, :k_rows]
    f_lo = fwd1_ref[1, :, :k_rows]

    def one(s2):
        m = _dot3(f_hi, f_lo, load_rows(s2))
        yield
        a_ref[pl.ds(pl.multiple_of(s2 * rows, SUBLANES), rows), :] = m

    def body(i, carry):
        _round_robin([one(i * FFT_ROWS_UNROLL + u) for u in range(FFT_ROWS_UNROLL)])
        return carry

    lax.fori_loop(0, n2 // FFT_ROWS_UNROLL, body, 0)


def _twiddle_cols(tw_ref, f1):
    lane = lax.broadcasted_iota(jnp.int32, (FFT_N2, LANES), 1)
    sel = lane == f1
    twr = jnp.sum(jnp.where(sel, tw_ref[0], 0.0), axis=-1, keepdims=True)
    twi = jnp.sum(jnp.where(sel, tw_ref[1], 0.0), axis=-1, keepdims=True)
    return twr, twi


def _hyena_filter_kernel(feat_ref, w1_ref, b1_ref, w2_ref, b2_ref, w3_ref, b3_ref, fr_ref,
                         wo_ref, dec_ref, fwd1_ref, fwd2_ref, tw_ref, o_ref, kt_ref, a_ref, hid_ref):
    n = feat_ref.shape[0]
    seq = n // 2
    n2 = FFT_N2
    n1 = n // n2
    tile = 512
    freq = fr_ref[...]

    @pl.when(pl.program_id(0) == 0)
    def _():
        def hidden(i, carry):
            r0 = pl.multiple_of(i * tile, tile)
            hid = jnp.sin(freq * (_hdot(feat_ref[pl.ds(r0, tile), :], w1_ref[...]) + b1_ref[...]))
            hid = jnp.sin(freq * (_hdot(hid, w2_ref[...]) + b2_ref[...]))
            hid_ref[pl.ds(r0, tile), :] = jnp.sin(freq * (_hdot(hid, w3_ref[...]) + b3_ref[...]))
            return carry

        lax.fori_loop(0, n // tile, hidden, 0)

    def gen(i, carry):
        r0 = pl.multiple_of(i * tile, tile)
        second = r0 >= seq
        wo = jnp.where(second, wo_ref[1, 0], wo_ref[0, 0])
        dec = jnp.where(second, dec_ref[1, 0], dec_ref[0, 0])
        t = feat_ref[pl.ds(r0, tile), 0:1]
        filt = _hdot(hid_ref[pl.ds(r0, tile), :], wo) * jnp.exp(-t * jnp.abs(dec))
        rows = r0 + lax.broadcasted_iota(jnp.int32, (tile, 1), 0)
        kt_ref[pl.ds(r0, tile), :] = jnp.where(rows == seq, 0.0, filt)
        return carry

    lax.fori_loop(0, n // tile, gen, 0)

    _fft_stage_a(lambda s2: kt_ref[pl.ds(s2, n1, stride=n2), :], n1, fwd1_ref, a_ref)

    nf = n1 // 2 + 1
    nfp = fwd1_ref.shape[1] // 2

    def slab(f1):
        r0 = f1 * 2 * n2 if isinstance(f1, int) else pl.multiple_of(f1 * 2 * n2, 2 * n2)
        ar = a_ref[pl.ds(f1, n2, stride=2 * nfp), :]
        ai = a_ref[pl.ds(nfp + f1, n2, stride=2 * nfp), :]
        twr, twi = _twiddle_cols(tw_ref, f1)
        x = jnp.concatenate([ar * twr - ai * twi, ar * twi + ai * twr], axis=0)
        xf = _dot3(fwd2_ref[0], fwd2_ref[1], x)
        yield
        o_ref[0, pl.ds(r0, 2 * n2), :] = xf

    def slabs(i, carry):
        _round_robin([slab(i * FFT_SLAB_UNROLL + u) for u in range(FFT_SLAB_UNROLL)])
        return carry

    lax.fori_loop(0, nf // FFT_SLAB_UNROLL, slabs, 0)
    _round_robin([slab(f1) for f1 in range(nf - nf % FFT_SLAB_UNROLL, nf)])


def _hyena_filter_spectrum(consts, feats_full, w1p, b1, w2, b2, w3, b3, freq, w_out, decay):
    n = feats_full.shape[0]
    nct = D_B // HY_CT
    spec_rows = (consts['n1'] // 2 + 1) * 2 * FFT_N2
    full = lambda a: pl.BlockSpec(a.shape, lambda j: (0,) * a.ndim)
    wo = w_out.reshape(HY_HID, 2, nct, HY_CT).transpose(1, 2, 0, 3)
    dec = decay.reshape(2, nct, 1, HY_CT)
    small = (w1p, b1.reshape(1, -1), w2, b2.reshape(1, -1), w3, b3.reshape(1, -1), freq.reshape(1, -1))
    return pl.pallas_call(
        _hyena_filter_kernel,
        grid=(nct,),
        in_specs=[full(feats_full)] + [full(a) for a in small] + [
            pl.BlockSpec((2, 1, HY_HID, HY_CT), lambda j: (0, j, 0, 0)),
            pl.BlockSpec((2, 1, 1, HY_CT), lambda j: (0, j, 0, 0)),
            full(consts['fwd1h']), full(consts['fwd2']), full(consts['tw']),
        ],
        out_specs=pl.BlockSpec((1, spec_rows, HY_CT), lambda j: (j, 0, 0)),
        out_shape=jax.ShapeDtypeStruct((nct, spec_rows, HY_CT), F32),
        scratch_shapes=[pltpu.VMEM((n, HY_CT), F32), pltpu.VMEM((2 * n, HY_CT), F32),
                        pltpu.VMEM((n, HY_HID), F32)],
        compiler_params=_cparams(("arbitrary",)),
        name="hyena_filter",
    )(feats_full, *small, wo, dec, consts['fwd1h'], consts['fwd2'], consts['tw'])


def _hyena_conv_kernel(z_ref, x0_ref, zb_ref, ks_ref, bias_ref, fwd1_ref, inv1_ref, fwd2_ref,
                       inv2_ref, tw_ref, o_ref, a_ref, b_ref, y_ref):
    seq = z_ref.shape[1]
    n2 = FFT_N2
    n1 = 2 * seq // n2
    nh = n1 // 2
    nf = nh + 1
    nfp = fwd1_ref.shape[1] // 2

    _fft_stage_a(lambda s2: z_ref[0, pl.ds(s2, nh, stride=n2), :], nh, fwd1_ref, a_ref)
    b_ref[pl.ds(nf * 2 * n2, (nfp - nf) * 2 * n2), :] = jnp.zeros(((nfp - nf) * 2 * n2, b_ref.shape[1]), F32)

    def slab(f1):
        r0 = f1 * 2 * n2 if isinstance(f1, int) else pl.multiple_of(f1 * 2 * n2, 2 * n2)
        ar = a_ref[pl.ds(f1, n2, stride=2 * nfp), :]
        ai = a_ref[pl.ds(nfp + f1, n2, stride=2 * nfp), :]
        twr, twi = _twiddle_cols(tw_ref, f1)
        x = jnp.concatenate([ar * twr - ai * twi, ar * twi + ai * twr], axis=0)
        xf = _dot3(fwd2_ref[0], fwd2_ref[1], x)
        yield
        xr, xi = xf[:n2], xf[n2:]
        kr = ks_ref[0, pl.ds(r0, n2), :]
        ki = ks_ref[0, pl.ds(r0 + n2, n2), :]
        y = jnp.concatenate([xr * kr - xi * ki, xr * ki + xi * kr], axis=0)
        bb = _dot3(inv2_ref[0], inv2_ref[1], y)
        yield
        br, bi = bb[:n2], bb[n2:]
        b_ref[pl.ds(r0, n2), :] = br * twr + bi * twi
        b_ref[pl.ds(r0 + n2, n2), :] = bi * twr - br * twi

    def slabs(i, carry):
        _round_robin([slab(i * FFT_SLAB_UNROLL + u) for u in range(FFT_SLAB_UNROLL)])
        return carry

    lax.fori_loop(0, nf // FFT_SLAB_UNROLL, slabs, 0)
    _round_robin([slab(f1) for f1 in range(nf - nf % FFT_SLAB_UNROLL, nf)])

    i_hi = inv1_ref[0]
    i_lo = inv1_ref[1]

    def stage_d(s2):
        br = b_ref[pl.ds(s2, nfp, stride=2 * n2), :]
        bi = b_ref[pl.ds(n2 + s2, nfp, stride=2 * n2), :]
        y = _dot3(i_hi, i_lo, jnp.concatenate([br, bi], axis=0))
        yield
        y_ref[pl.ds(s2, nh, stride=n2), :] = y

    def stage_ds(i, carry):
        _round_robin([stage_d(i * FFT_ROWS_UNROLL + u) for u in range(FFT_ROWS_UNROLL)])
        return carry

    lax.fori_loop(0, n2 // FFT_ROWS_UNROLL, stage_ds, 0)

    tile = 256
    def finish(i, carry):
        r0 = pl.multiple_of(i * tile, tile)
        z = z_ref[0, pl.ds(r0, tile), :]
        y = y_ref[pl.ds(r0, tile), :] + bias_ref[...] * z
        o_ref[0, pl.ds(r0, tile), :] = x0_ref[0, pl.ds(r0, tile), :] * y * _silu(zb_ref[pl.ds(r0, tile), :].astype(F32))
        return carry

    lax.fori_loop(0, seq // tile, finish, 0)


def _hyena_conv(consts, z, x0, proj_g, kspec, bias):
    bsz, seq, _ = z.shape
    nct = D_B // HY_CT
    full = lambda a: pl.BlockSpec(a.shape, lambda j, b: (0,) * a.ndim)
    cs = pl.BlockSpec((1, seq, HY_CT), lambda j, b: (b, 0, j))
    mats = (consts['fwd1h'], consts['inv1h'], consts['fwd2'], consts['inv2'], consts['tw'])
    stage_rows = consts['fwd1h'].shape[1]
    return pl.pallas_call(
        _hyena_conv_kernel,
        grid=(nct, bsz),
        in_specs=[
            cs, cs,
            pl.BlockSpec((seq, HY_CT), lambda j, b: (b, G_BZ // HY_CT + j)),
            pl.BlockSpec((1, kspec.shape[1], HY_CT), lambda j, b: (j, 0, 0)),
            pl.BlockSpec((1, HY_CT), lambda j, b: (0, j)),
        ] + [full(m) for m in mats],
        out_specs=cs,
        out_shape=jax.ShapeDtypeStruct((bsz, seq, D_B), F32),
        scratch_shapes=[pltpu.VMEM((FFT_N2 * stage_rows, HY_CT), F32),
                        pltpu.VMEM((stage_rows * FFT_N2, HY_CT), F32),
                        pltpu.VMEM((seq, HY_CT), F32)],
        compiler_params=_cparams(("arbitrary", "arbitrary")),
        name="hyena_conv",
    )(z, x0, proj_g, kspec, bias.reshape(1, D_B), *mats)


def _rwkv_kernel(r_ref, k_ref, v_ref, ul_ref, cz_ref, w0_ref, w2_ref, a0_ref, a2_ref,
                 kks_ref, ka_ref, rk_ref, lg_ref, lb_ref, o_ref, y_ref, kb_ref, s_ref,
                 wr_ref, u_ref, arb_ref, y0_ref, bt_ref, kv_ref, egc_ref,
                 kk_ref, lw_ref, be_ref, kd_ref):
    seq = r_ref.shape[1]
    L = CHUNK_C
    nc = seq // L
    P2 = 2 * L
    lane = lax.broadcasted_iota(jnp.int32, (L, LANES), 1)
    head0 = lane < HEAD_C
    prow = lax.broadcasted_iota(jnp.int32, (P2, P2), 0)
    pcol = lax.broadcasted_iota(jnp.int32, (P2, P2), 1)
    bd = (prow // HEAD_C) == (pcol // HEAD_C)
    blk = jnp.where(bd, 1.0, 0.0).astype(F32)
    trow = lax.broadcasted_iota(jnp.int32, (L, P2), 0)
    scol = lax.broadcasted_iota(jnp.int32, (L, P2), 1) % L
    eye = jnp.where(trow == scol, 1.0, 0.0).astype(F32)
    crow = lax.broadcasted_iota(jnp.int32, (L, L), 0)
    ccol = lax.broadcasted_iota(jnp.int32, (L, L), 1)

    def stack(x):
        return jnp.concatenate([jnp.where(head0, x, 0.0), jnp.where(head0, 0.0, x)], axis=0)

    blk_b = blk.astype(BF16)
    grp = wr_ref.shape[1]

    def dot(a, b):
        return jnp.dot(a, b, preferred_element_type=F32)

    def each(f, *cols):
        return [f(*xs) for xs in zip(*cols)]

    def prepare(probs):
        dirs = [d for _, d, _ in probs]
        fwd = [d == 0 for d in dirs]
        strict = [(scol < trow) if f else (scol > trow) for f in fwd]
        incl = [(scol <= trow) if f else (scol >= trow) for f in fwd]
        cum = [jnp.where((ccol <= crow) if f else (ccol >= crow), 1.0, 0.0).astype(BF16) for f in fwd]
        c0 = [pl.multiple_of(j * L, L) for j, _, _ in probs]
        r = [r_ref[0, pl.ds(c, L), :] for c in c0]
        v = [v_ref[0, pl.ds(c, L), :] for c in c0]
        kk = [kk_ref[pl.ds(c, L), :] for c in c0]
        lwt = [lw_ref[d, pl.ds(c, L), :] for c, d in zip(c0, dirs)]
        be = [be_ref[d, pl.ds(c, L), :] for c, d in zip(c0, dirs)]
        kd = [kd_ref[d, pl.ds(c, L), :] for c, d in zip(c0, dirs)]

        def cumsum(cm, x):
            l1 = x.astype(BF16)
            e1 = x - l1.astype(F32)
            l2 = e1.astype(BF16)
            l3 = (e1 - l2.astype(F32)).astype(BF16)
            return dot(jnp.concatenate([cm, cm, cm], axis=1), jnp.concatenate([l1, l2, l3], axis=0))

        g = each(cumsum, cum, lwt)
        yield
        eng = [jnp.exp(-x) for x in g]
        egc = [jnp.exp(x[L - 1:L, :] if f else x[0:1, :]) for x, f in zip(g, fwd)]
        at = each(lambda x, y, z: (-x * jnp.exp(y - z)).astype(BF16), kk, g, lwt)
        rt_ = each(lambda x, y: (x * jnp.exp(y)).astype(BF16), r, g)
        bh = each(lambda x, y: x * y, be, eng)
        kh = each(lambda x, y: x * y, kd, eng)
        lhs = each(lambda x, y: jnp.concatenate([x, y], axis=0), at, rt_)
        rhs = each(lambda x, y: jnp.concatenate([stack(x), stack(y)], axis=0).astype(BF16), bh, kh)
        g1 = each(lambda x, y: lax.dot_general(x, y, (((1,), (1,)), ((), ())),
                                               preferred_element_type=F32), lhs, rhs)
        yield
        aab = each(lambda m, x: jnp.where(m, x[:L, :P2], 0.0), strict, g1)
        aak_ark = each(lambda ms, mi, x: jnp.concatenate(
            [jnp.where(ms, x[:L, P2:], 0.0), jnp.where(mi, x[L:, P2:], 0.0)], axis=0).astype(BF16),
            strict, incl, g1)
        for (_, d, slot), mi, x in zip(probs, incl, g1):
            arb_ref[d, slot] = jnp.where(mi, x[L:, :P2], 0.0).astype(BF16)
        t = [eye + x for x in aab]
        p = [x.astype(BF16) for x in aab]
        vs = [stack(x).astype(BF16) for x in v]
        av = each(dot, aak_ark, vs)
        p = [dot(x, stack(x)) for x in p]
        yield
        p = [x.astype(BF16) for x in p]
        sq = 2
        while sq < L:
            if 2 * sq >= L:
                pt = each(lambda x, y: dot(x, stack(y.astype(BF16))), p, t)
                yield
                t = each(lambda y, x: y + x, t, pt)
            else:
                pt = each(lambda x, y: dot(x, jnp.concatenate([stack(x), stack(y.astype(BF16))], axis=1)),
                          p, t)
                yield
                p = [x[:, :P2].astype(BF16) for x in pt]
                t = each(lambda y, x: y + x[:, P2:], t, pt)
            sq *= 2
        wu = each(lambda x, y, z: dot(x.astype(BF16), jnp.concatenate(
            [stack(y), stack(z[:L].astype(BF16))], axis=1)), t, at, av)
        kv = each(lambda x, y, z: lax.dot_general(
            x.astype(BF16), (y * z).astype(BF16), (((0,), (0,)), ((), ())), preferred_element_type=F32),
            v, kh, egc)
        yield
        for i, (_, d, slot) in enumerate(probs):
            wr_ref[d, slot] = jnp.concatenate([wu[i][:, :P2].astype(BF16), rt_[i]], axis=0)
            u_ref[d, slot] = wu[i][:, P2:].astype(BF16)
            y0_ref[d, slot] = av[i][L:]
            bt_ref[d, slot] = (bh[i] * egc[i]).astype(BF16)
            kv_ref[d, slot] = jnp.where(bd, kv[i], 0.0).astype(BF16)
            egc_ref[d, slot] = egc[i]

    def advance(steps):
        s = [s_ref[d] for d in range(2)]
        for probs in steps:
            x = [lax.dot_general(wr_ref[d, slot], s[d].astype(BF16), (((1,), (1,)), ((), ())),
                                 preferred_element_type=F32) for _, d, slot in probs]
            yield
            sa = [(y[:L] + u_ref[d, slot].astype(F32)).astype(BF16) for (_, d, slot), y in zip(probs, x)]
            upd = [lax.dot_general(y, bt_ref[d, slot], (((0,), (0,)), ((), ())),
                                   preferred_element_type=F32) for (_, d, slot), y in zip(probs, sa)]
            z = [dot(arb_ref[d, slot], stack(y)) for (_, d, slot), y in zip(probs, sa)]
            yield
            for i, (j, d, slot) in enumerate(probs):
                s[d] = (s[d] * egc_ref[d, slot] + kv_ref[d, slot].astype(F32)
                        + jnp.where(bd, upd[i], 0.0))
                c0 = pl.multiple_of(j * L, L)
                y_ref[pl.ds(c0, L), :] += x[i][L:] + z[i] + y0_ref[d, slot]
        for d in range(2):
            s_ref[d] = s[d]

    s_ref[...] = jnp.zeros_like(s_ref)
    y_ref[...] = jnp.zeros_like(y_ref)
    tile = 256

    def pointwise(r0):
        k = k_ref[0, pl.ds(r0, tile), :]
        lwla = ul_ref[0, pl.ds(r0, tile), :]
        th = jnp.tanh(lwla[:, :LANES]).astype(BF16)
        la = lwla[:, LANES:].astype(BF16)
        xw = [w0_ref[d:d + 1, :] + dot(th, w2_ref[d].astype(BF16)) for d in range(2)]
        aa = [a0_ref[d:d + 1, :] + dot(la, a2_ref[d].astype(BF16)) for d in range(2)]
        kkv = k * kks_ref[...]
        ss = dot((kkv * kkv).astype(BF16), blk_b)
        yield
        kk = kkv * lax.rsqrt(jnp.maximum(ss, 1e-24))
        kk_ref[pl.ds(r0, tile), :] = kk
        kb = None
        for d in range(2):
            lw_ref[d, pl.ds(r0, tile), :] = -math.exp(-0.5) * _sigmoid(xw[d])
            a = _sigmoid(aa[d])
            kd = k * (1.0 + (a - 1.0) * ka_ref[...])
            be_ref[d, pl.ds(r0, tile), :] = kk * a
            kd_ref[d, pl.ds(r0, tile), :] = kd
            kb = kd if kb is None else kb + kd
        kb_ref[pl.ds(r0, tile), :] = kb

    pw_unroll = 2 if (seq // tile) % 2 == 0 else 1

    def pointwise_loop(i, carry):
        _round_robin([pointwise(pl.multiple_of((i * pw_unroll + u) * tile, tile))
                      for u in range(pw_unroll)])
        return carry

    lax.fori_loop(0, seq // tile // pw_unroll, pointwise_loop, 0)

    lock = grp // 2
    nsets = nc // lock

    def chunk_set(i):
        base = (i % 2) * lock
        return [[(i * lock + q, 0, base + q), (nc - 1 - (i * lock + q), 1, base + q)]
                for q in range(lock)]

    _round_robin([prepare(sum(chunk_set(0), []))])

    def pipelined(i, carry):
        _round_robin([prepare(sum(chunk_set(i + 1), [])), advance(chunk_set(i))])
        return carry

    lax.fori_loop(0, nsets - 1, pipelined, 0)
    _round_robin([advance(chunk_set(nsets - 1))])

    tile = 256
    inv_n = 1.0 / HEAD_C
    def head_sum(x):
        return dot(x.astype(BF16), blk_b)

    def finish(r0):
        y = y_ref[pl.ds(r0, tile), :]
        mu = head_sum(y) * inv_n
        r = r_ref[0, pl.ds(r0, tile), :]
        v = v_ref[0, pl.ds(r0, tile), :]
        kbon = 0.5 * kb_ref[pl.ds(r0, tile), :]
        bonus = head_sum(r * kbon * rk_ref[...]) * v
        yield
        yc = y - mu
        var = head_sum(yc * yc) * inv_n
        yield
        yn = yc * lax.rsqrt(var + LNX_EPS) * lg_ref[...] + lb_ref[...]
        o_ref[0, pl.ds(r0, tile), :] = (yn + bonus) * _silu(cz_ref[pl.ds(r0, tile), :].astype(F32))

    unroll = 4 if (seq // tile) % 4 == 0 else 1

    def finishes(i, carry):
        _round_robin([finish(pl.multiple_of((i * unroll + u) * tile, tile)) for u in range(unroll)])
        return carry

    lax.fori_loop(0, seq // tile // unroll, finishes, 0)


def _rwkv(u, ul, proj_g, w0, w2pad, a0, a2pad, kk_s, ka, rk, lnx_g, lnx_b):
    bsz, seq, _ = u.shape
    npair = D_C // LANES
    grp = min(RW_GROUP, seq // CHUNK_C)
    col = lambda off: pl.BlockSpec((1, seq, LANES), lambda b, p: (b, 0, off + p))
    vec = pl.BlockSpec((1, LANES), lambda b, p: (0, p))
    two = pl.BlockSpec((2, LANES), lambda b, p: (0, p))
    lora = pl.BlockSpec((2, LANES, LANES), lambda b, p: (0, 0, p))
    row = lambda a: a.reshape(1, D_C)
    return pl.pallas_call(
        _rwkv_kernel,
        grid=(bsz, npair),
        in_specs=[
            col(0), col(npair), col(2 * npair),
            pl.BlockSpec((1, seq, 2 * LANES), lambda b, p: (b, 0, 0)),
            pl.BlockSpec((seq, LANES), lambda b, p: (b, G_CZ // LANES + p)),
            two, lora, two, lora, vec, vec, vec, vec, vec,
        ],
        out_specs=pl.BlockSpec((1, seq, LANES), lambda b, p: (b, 0, p)),
        out_shape=jax.ShapeDtypeStruct((bsz, seq, D_C), F32),
        scratch_shapes=[pltpu.VMEM((seq, LANES), F32), pltpu.VMEM((seq, LANES), F32),
                        pltpu.VMEM((2, LANES, LANES), F32),
                        pltpu.VMEM((2, grp, 2 * CHUNK_C, LANES), BF16),
                        pltpu.VMEM((2, grp, CHUNK_C, LANES), BF16),
                        pltpu.VMEM((2, grp, CHUNK_C, LANES), BF16),
                        pltpu.VMEM((2, grp, CHUNK_C, LANES), F32),
                        pltpu.VMEM((2, grp, CHUNK_C, LANES), BF16),
                        pltpu.VMEM((2, grp, LANES, LANES), BF16),
                        pltpu.VMEM((2, grp, 1, LANES), F32),
                        pltpu.VMEM((seq, LANES), F32), pltpu.VMEM((2, seq, LANES), F32),
                        pltpu.VMEM((2, seq, LANES), F32), pltpu.VMEM((2, seq, LANES), F32)],
        compiler_params=_cparams(("arbitrary", "arbitrary")),
        name="rwkv7",
    )(u, u, u, ul, proj_g, w0, w2pad, a0, a2pad, row(kk_s), row(ka), row(rk), row(lnx_g), row(lnx_b))


def _out_kernel(ya_ref, yb_ref, yc_ref, ga_ref, gb_ref, gc_ref, x_ref, gate_ref, pg_ref,
                wa_ref, wb_ref, wc_ref, wo_ref, o_ref):
    merged = (_sigmoid(ga_ref[...].astype(F32)) * _bdot(ya_ref[...], wa_ref[...])
              + _sigmoid(gb_ref[...].astype(F32)) * _bdot(yb_ref[...], wb_ref[...])
              + _sigmoid(gc_ref[...].astype(F32)) * _bdot(yc_ref[...], wc_ref[...]))
    out = _bdot(merged, wo_ref[...])
    y = out * lax.rsqrt(jnp.mean(out * out, axis=-1, keepdims=True) + NORM_EPS) * pg_ref[...]
    o_ref[...] = x_ref[...] + gate_ref[0] * y


def _merge_out(ya, yb, yc, proj_g, x2d, seq, gate, post_g, wa, wb, wc, wo, tm):
    rows, d = x2d.shape
    per_b = seq // tm
    full = lambda a: pl.BlockSpec(a.shape, lambda i: (0,) * a.ndim)
    gcol = lambda off: pl.BlockSpec((tm, d), lambda i: (i, off // d))
    return pl.pallas_call(
        _out_kernel,
        grid=(rows // tm,),
        in_specs=[
            pl.BlockSpec((tm, D_A), lambda i: (i, 0)),
            pl.BlockSpec((tm, D_B), lambda i: (i, 0)),
            pl.BlockSpec((tm, D_C), lambda i: (i, 0)),
            gcol(G_GA), gcol(G_GB), gcol(G_GC),
            pl.BlockSpec((tm, d), lambda i: (i, 0)),
            pl.BlockSpec((1, 1, d), lambda i: (i // per_b, 0, 0)),
            pl.BlockSpec((1, d), lambda i: (0, 0)),
            full(wa), full(wb), full(wc), full(wo),
        ],
        out_specs=pl.BlockSpec((tm, d), lambda i: (i, 0)),
        out_shape=jax.ShapeDtypeStruct((rows, d), F32),
        compiler_params=_cparams(("arbitrary",)),
        name="merge_out",
    )(ya.reshape(rows, D_A), yb.reshape(rows, D_B), yc.reshape(rows, D_C), proj_g, proj_g, proj_g,
      x2d, gate, post_g.reshape(1, d), wa, wb, wc, wo)


def _hyena_feats(seq):
    n = 2 * seq
    t = jnp.linspace(0.0, 1.0, seq, dtype=F32)
    w = 2.0 * math.pi * jnp.arange(seq, dtype=F32) / seq
    f = jnp.linspace(1e-4, HY_BANDS - 1, HY_BANDS, dtype=F32)
    zz = w[:, None] * f[None, :]
    feats = jnp.concatenate([t[:, None], jnp.cos(zz), -jnp.sin(zz)], axis=-1)
    pos = np.concatenate([np.arange(seq), [0], np.arange(seq - 1, 0, -1)])
    full = jnp.zeros((n, LANES), F32).at[:, :feats.shape[1]].set(feats[pos])
    return full


def _tiles(seq):
    return min(seq, 1024), min(seq, 256), min(seq, 512)


def kernel(x, c, ada_w, ada_b, pre_g, post_g, w_in, ml_conv_w, ml_conv_b, ml_wq, ml_wk, ml_wv, ml_gate_w, ml_gate_b, ml_norm_g, ml_skip, hy_conv_w, hy_conv_b, hy_w1, hy_b1, hy_w2, hy_b2, hy_w3, hy_b3, hy_freq, hy_w_out, hy_decay, hy_bias, rw_mu, rw_w0, rw_w2, rw_a0, rw_a2, rw_kk, rw_ka, rw_rk, rw_lnx_g, rw_lnx_b, w_branch_a, w_branch_b, w_branch_c, w_out):
    bsz, seq, d = x.shape
    depth = ada_w.shape[0]
    tm_in, ts_prep, tm_out = _tiles(seq)
    runs_h, runs_g = (_contiguous_runs(p) for p in _ref_column_perm())
    consts = _dft_constants(seq)
    feats_full = _hyena_feats(seq)
    mod = _modulation(c, ada_w, ada_b)
    x2d = x.reshape(bsz * seq, d)
    n_gate = 4 * H_A
    for l in range(depth):
        shift = mod[l, :, None, 0:d]
        scale = mod[l, :, None, d:2 * d]
        gate = mod[l, :, None, 2 * d:3 * d]
        w_h = jnp.concatenate([w_in[l][:, a:b] for a, b in runs_h], axis=1).astype(BF16)
        w_g = jnp.concatenate([w_in[l][:, a:b] for a, b in runs_g], axis=1).astype(BF16)
        proj_h = _inproj(x2d, seq, pre_g[l], shift, scale, w_h, tm_in, N_H // 3)
        proj_g = _inproj(x2d, seq, pre_g[l], shift, scale, w_g, tm_in, N_G // 4)

        gw = jnp.zeros((3 * D_A, LANES), F32).at[:, :n_gate].set(ml_gate_w[l])
        gb = jnp.zeros((1, LANES), F32).at[0, :n_gate].set(ml_gate_b[l])
        q, k, v, xc, gcol, z, x0, u, ul = _prep(
            proj_h, bsz, seq, ts_prep, ml_conv_w[l], ml_conv_b[l].reshape(1, D_A),
            ml_wq[l], ml_wk[l], ml_wv[l], gw, gb,
            hy_conv_w[l], hy_conv_b[l].reshape(1, 3 * D_B), rw_mu[l].reshape(1, -1))

        y_a = _mlstm(q, k, v, xc, proj_g, gcol, ml_norm_g[l], ml_skip[l])

        w1p = jnp.zeros((LANES, HY_HID), F32).at[:hy_w1.shape[1]].set(hy_w1[l])
        kspec = _hyena_filter_spectrum(consts, feats_full, w1p, hy_b1[l], hy_w2[l], hy_b2[l],
                                       hy_w3[l], hy_b3[l], hy_freq[l], hy_w_out[l], hy_decay[l])
        y_b = _hyena_conv(consts, z, x0, proj_g, kspec, hy_bias[l])

        w2pad = (jnp.zeros((2, LANES, D_C), F32).at[0, :LORA].set(rw_w2[l, 0])
                 .at[1, LORA:].set(rw_w2[l, 1]))
        a2pad = (jnp.zeros((2, LANES, D_C), F32).at[0, :LORA].set(rw_a2[l, 0])
                 .at[1, LORA:].set(rw_a2[l, 1]))
        y_c = _rwkv(u, ul, proj_g, rw_w0[l], w2pad, rw_a0[l], a2pad, rw_kk[l], rw_ka[l],
                    rw_rk[l].reshape(-1), rw_lnx_g[l], rw_lnx_b[l])

        x2d = _merge_out(y_a, y_b, y_c, proj_g, x2d, seq, gate, post_g[l],
                         w_branch_a[l].astype(BF16), w_branch_b[l].astype(BF16),
                         w_branch_c[l].astype(BF16), w_out[l].astype(BF16), tm_out)
    return x2d.reshape(bsz, seq, d)
```

```python
import functools
import math

import numpy as np
import jax
import jax.numpy as jnp
from jax import lax
from jax.experimental import pallas as pl
from jax.experimental.pallas import tpu as pltpu

D_MODEL = 1024
DEPTH = 4
D_A = 512
H_A = 4
DH_A = 128
CHUNK_A = 64
ML_GROUP = 16
D_B = 512
HY_BANDS = 16
HY_HID = 64
D_C = 1024
HEAD_C = 64
H_C = D_C // HEAD_C
LORA = 64
CHUNK_C = 64
RW_GROUP = 8
LNX_EPS = 64e-5
NORM_EPS = 1e-6
HEAD_NORM_EPS = 1e-5

LANES = 128
SUBLANES = 8
VMEM_LIMIT = 56 * 1024 * 1024

F32 = jnp.float32
BF16 = jnp.bfloat16
PROJ_DTYPE = BF16
PROJ_ROWS = 16
HIGHEST = lax.Precision.HIGHEST

N_H = D_A + 3 * D_B + 3 * D_C + 4 * LORA
N_G = D_A + D_B + D_C + 3 * D_MODEL
H_AX, H_BV, H_BX0, H_BX1, H_CR, H_CK, H_CV, H_LW, H_LA = (
    0, 512, 1024, 1536, 2048, 3072, 4096, 5120, 5248)
G_AZ, G_BZ, G_CZ, G_GA, G_GB, G_GC = 0, 512, 1024, 2048, 3072, 4096


def _ref_column_perm():
    r = lambda a, n: np.arange(a, a + n)
    a_x, a_z = r(0, 512), r(512, 512)
    b_v, b_x0, b_x1, b_z = r(1024, 512), r(1536, 512), r(2048, 512), r(2560, 512)
    c_r, c_k, c_v = r(3072, 1024), r(4096, 1024), r(5120, 1024)
    c_lw, c_la = r(6144, 128), r(6272, 128)
    c_z = r(6400, 1024)
    g_a, g_b, g_c = r(7424, 1024), r(8448, 1024), r(9472, 1024)
    h = np.concatenate([a_x, b_v, b_x0, b_x1, c_r, c_k, c_v, c_lw, c_la])
    g = np.concatenate([a_z, b_z, c_z, g_a, g_b, g_c])
    return h, g


def _contiguous_runs(idx):
    cuts = np.flatnonzero(np.diff(idx) != 1) + 1
    return [(int(seg[0]), int(seg[-1]) + 1) for seg in np.split(idx, cuts)]


def _cparams(sem, **extra):
    return pltpu.CompilerParams(dimension_semantics=sem, vmem_limit_bytes=VMEM_LIMIT, **extra)


def _silu(x):
    return x * (1.0 / (1.0 + jnp.exp(-x)))


def _sigmoid(x):
    return 1.0 / (1.0 + jnp.exp(-x))


def _bdot(a, b):
    return jnp.dot(a.astype(BF16), b.astype(BF16), preferred_element_type=F32)


def _bdot_nt(a, b):
    return lax.dot_general(a.astype(BF16), b.astype(BF16), (((1,), (1,)), ((), ())),
                           preferred_element_type=F32)


def _round_robin(gens):
    gens = list(gens)
    while gens:
        alive = []
        for g in gens:
            try:
                next(g)
                alive.append(g)
            except StopIteration:
                pass
        gens = alive


def _hdot(a, b):
    return jnp.dot(a, b, preferred_element_type=F32, precision=HIGHEST)


def _mod_kernel(c_ref, w_ref, b_ref, o_ref):
    cond = _silu(c_ref[...])
    o_ref[0] = _hdot(cond, w_ref[0]) + b_ref[0]


def _modulation(c, ada_w, ada_b):
    depth, d, n3 = ada_w.shape
    bsz = c.shape[0]
    nt = n3 // d
    return pl.pallas_call(
        _mod_kernel,
        grid=(depth, nt),
        in_specs=[
            pl.BlockSpec((bsz, d), lambda l, j: (0, 0)),
            pl.BlockSpec((1, d, d), lambda l, j: (l, 0, j)),
            pl.BlockSpec((1, 1, d), lambda l, j: (l, 0, j)),
        ],
        out_specs=pl.BlockSpec((1, bsz, d), lambda l, j: (l, 0, j)),
        out_shape=jax.ShapeDtypeStruct((depth, bsz, n3), F32),
        compiler_params=_cparams(("arbitrary", "arbitrary")),
        name="adaln_mod",
    )(c, ada_w, ada_b.reshape(depth, 1, n3))


def _inproj_kernel(x_ref, g_ref, shift_ref, scale_ref, w_ref, o_ref, h_ref):
    @pl.when(pl.program_id(1) == 0)
    def _():
        x = x_ref[...]
        y = x * lax.rsqrt(jnp.mean(x * x, axis=-1, keepdims=True) + NORM_EPS)
        h = y * g_ref[...] * (1.0 + scale_ref[0]) + shift_ref[0]
        h_ref[...] = h.astype(BF16)

    o_ref[...] = jnp.dot(h_ref[...], w_ref[...], preferred_element_type=F32).astype(o_ref.dtype)


def _inproj(x2d, seq, pre_g, shift, scale, w_bf16, tm, tn):
    rows, d = x2d.shape
    n = w_bf16.shape[1]
    per_b = seq // tm
    return pl.pallas_call(
        _inproj_kernel,
        grid=(rows // tm, n // tn),
        in_specs=[
            pl.BlockSpec((tm, d), lambda i, j: (i, 0)),
            pl.BlockSpec((1, d), lambda i, j: (0, 0)),
            pl.BlockSpec((1, 1, d), lambda i, j: (i // per_b, 0, 0)),
            pl.BlockSpec((1, 1, d), lambda i, j: (i // per_b, 0, 0)),
            pl.BlockSpec((d, tn), lambda i, j: (0, j)),
        ],
        out_specs=pl.BlockSpec((tm, tn), lambda i, j: (i, j)),
        out_shape=jax.ShapeDtypeStruct((rows, n), PROJ_DTYPE),
        scratch_shapes=[pltpu.VMEM((tm, d), BF16)],
        compiler_params=_cparams(("arbitrary", "arbitrary")),
        name="inproj",
    )(x2d, pre_g.reshape(1, d), shift, scale, w_bf16)


def _log_sigmoid(x):
    return jnp.minimum(x, 0.0) - jnp.log(1.0 + jnp.exp(-jnp.abs(x)))


def _prep_kernel(main_ref, prev_ref, next_ref,
                 mcw_ref, mcb_ref, wq_ref, wk_ref, wv_ref, gw_ref, gb_ref,
                 hcw_ref, hcb_ref, mu_ref,
                 q_ref, k_ref, v_ref, xc_ref, gcol_ref, z_ref, x0_ref, u_ref, ul_ref):
    i = pl.program_id(1)
    ts = main_ref.shape[0]
    has_prev = jnp.where(i > 0, 1.0, 0.0).astype(F32)
    has_next = jnp.where(i < pl.num_programs(1) - 1, 1.0, 0.0).astype(F32)

    def neighbours(c0, cw):
        x = main_ref[:, c0:c0 + cw].astype(F32)
        row = lax.broadcasted_iota(jnp.int32, (SUBLANES, cw), 0)
        p_row = prev_ref[:, c0:c0 + cw].astype(F32)[PROJ_ROWS - 1:PROJ_ROWS] * has_prev
        n_row = next_ref[:, c0:c0 + cw].astype(F32)[0:1] * has_next
        xp = pltpu.roll(x, 1, 0)
        xn = pltpu.roll(x, ts - 1, 0)
        xp = jnp.concatenate([jnp.where(row == 0, p_row, xp[:SUBLANES]), xp[SUBLANES:]], axis=0)
        xn = jnp.concatenate([xn[:ts - SUBLANES],
                              jnp.where(row == SUBLANES - 1, n_row, xn[ts - SUBLANES:])], axis=0)
        return xp, x, xn

    def conv(c0, cw, w_ref, b_ref, w0):
        xp, x, xn = neighbours(c0, cw)
        w = w_ref[:, w0:w0 + cw]
        return xp * w[0:1] + x * w[1:2] + xn * w[2:3] + b_ref[:, w0:w0 + cw], x

    gates = jnp.zeros((ts, LANES), F32) + gb_ref[...]
    for h in range(H_A):
        sl = slice(h * DH_A, (h + 1) * DH_A)
        conv_a, xa = conv(H_AX + h * DH_A, DH_A, mcw_ref, mcb_ref, h * DH_A)
        xc = _silu(conv_a)
        xc_ref[0, :, sl] = xc
        qh = _bdot(xc, wq_ref[h])
        kh = _bdot(xc, wk_ref[h])
        vh = _bdot(xa, wv_ref[h])
        gates += (_bdot(qh, gw_ref[h * DH_A:(h + 1) * DH_A])
                  + _bdot(kh, gw_ref[D_A + h * DH_A:D_A + (h + 1) * DH_A])
                  + _bdot(vh, gw_ref[2 * D_A + h * DH_A:2 * D_A + (h + 1) * DH_A]))
        q_ref[0, :, sl] = qh.astype(BF16)
        k_ref[0, :, sl] = (kh * (DH_A ** -0.5)).astype(BF16)
        v_ref[0, :, sl] = vh.astype(BF16)
    col = lax.broadcasted_iota(jnp.int32, (ts, LANES), 1)
    rmod = lax.broadcasted_iota(jnp.int32, (ts, LANES), 0) % CHUNK_A
    lf = _log_sigmoid(gates)
    cf = lf
    cb = lf
    sh = 1
    while sh < CHUNK_A:
        cf = cf + jnp.where(rmod >= sh, pltpu.roll(cf, sh, 0), 0.0)
        cb = cb + jnp.where(rmod < CHUNK_A - sh, pltpu.roll(cb, ts - sh, 0), 0.0)
        sh *= 2
    is_ff = (col >= H_A) & (col < 2 * H_A)
    is_fb = (col >= 3 * H_A) & (col < 4 * H_A)
    gcol_ref[0] = jnp.where(is_ff, cf, jnp.where(is_fb, cb, gates))

    cw = LANES
    for j in range(D_B // cw):
        c = j * cw
        cv, _ = conv(H_BV + c, cw, hcw_ref, hcb_ref, c)
        cx1, _ = conv(H_BX1 + c, cw, hcw_ref, hcb_ref, 2 * D_B + c)
        z_ref[0, :, c:c + cw] = cv * cx1
        cx0, _ = conv(H_BX0 + c, cw, hcw_ref, hcb_ref, D_B + c)
        x0_ref[0, :, c:c + cw] = cx0

    def shifted(c):
        xp, x, xn = neighbours(H_CR + c, cw)
        mu = mu_ref[:, c:c + cw]
        return x * (1.0 - mu) + (0.5 * mu) * (xp + xn)

    for j in range(3 * D_C // cw):
        u_ref[0, :, j * cw:(j + 1) * cw] = shifted(j * cw)
    for j in range(4 * LORA // cw):
        ul_ref[0, :, j * cw:(j + 1) * cw] = shifted(3 * D_C + j * cw)


def _prep(proj_h, bsz, seq, ts, ml_conv_w, ml_conv_b, wq, wk, wv, gate_w_pad, gate_b_pad,
          hy_conv_w, hy_conv_b, rw_mu):
    ns = seq // ts
    hb = ts // PROJ_ROWS
    last_hb = bsz * seq // PROJ_ROWS - 1
    full = lambda a: pl.BlockSpec(a.shape, lambda b, i: (0,) * a.ndim)
    seq_spec = lambda w: pl.BlockSpec((1, ts, w), lambda b, i: (b, i, 0))
    params = (ml_conv_w, ml_conv_b, wq, wk, wv, gate_w_pad, gate_b_pad, hy_conv_w, hy_conv_b, rw_mu)
    outs = [(D_A, BF16), (D_A, BF16), (D_A, BF16), (D_A, F32), (LANES, F32),
            (D_B, F32), (D_B, F32), (3 * D_C, F32), (4 * LORA, F32)]
    return pl.pallas_call(
        _prep_kernel,
        grid=(bsz, ns),
        in_specs=[
            pl.BlockSpec((ts, N_H), lambda b, i: (b * ns + i, 0)),
            pl.BlockSpec((PROJ_ROWS, N_H), lambda b, i: (jnp.maximum((b * ns + i) * hb - 1, 0), 0)),
            pl.BlockSpec((PROJ_ROWS, N_H), lambda b, i: (jnp.minimum((b * ns + i + 1) * hb, last_hb), 0)),
        ] + [full(p) for p in params],
        out_specs=[seq_spec(w) for w, _ in outs],
        out_shape=[jax.ShapeDtypeStruct((bsz, seq, w), dt) for w, dt in outs],
        compiler_params=_cparams(("arbitrary", "arbitrary")),
        name="prep",
    )(proj_h, proj_h, proj_h, *params)


def _mlstm_kernel(q_ref, k_ref, v_ref, xc_ref, za_ref, gcol_ref, ng_ref, sk_ref, o_ref,
                  h_ref, ct_ref, n_ref, m_ref, num_ref, den_ref, bb_ref, mb_ref, kv_ref, sc_ref,
                  sel_ref):
    head = pl.program_id(1)
    seq = q_ref.shape[1]
    L = CHUNK_A
    nc = seq // L
    row = lax.broadcasted_iota(jnp.int32, (L, L), 0)
    colm = lax.broadcasted_iota(jnp.int32, (L, L), 1)
    lane = lax.broadcasted_iota(jnp.int32, (L, LANES), 1)

    grp = num_ref.shape[1]
    srow = lax.broadcasted_iota(jnp.int32, (SUBLANES, LANES), 0)
    ones = jnp.ones((LANES, LANES), BF16)
    ones2 = jnp.ones((2 * LANES, LANES), BF16)

    def dot(a, b):
        return jnp.dot(a, b, preferred_element_type=F32)

    def split3(x):
        x1 = x.astype(BF16)
        e1 = x - x1.astype(F32)
        x2 = e1.astype(BF16)
        return x1, x2, (e1 - x2.astype(F32)).astype(BF16)

    srow_i = lax.broadcasted_iota(jnp.int32, (3 * LANES, 2 * LANES), 0) % LANES
    scol_i = lax.broadcasted_iota(jnp.int32, (3 * LANES, 2 * LANES), 1)
    for d in range(2):
        li_lane = head + 2 * H_A * d
        want = jnp.where(scol_i < LANES, li_lane + H_A, li_lane)
        sel_ref[d] = jnp.where(srow_i == want, 1.0, 0.0).astype(BF16)

    def prepare(j, d, slot):
        fwd = d == 0
        tri = (colm <= row) if fwd else (colm >= row)
        c0 = pl.multiple_of(j * L, L)
        q = q_ref[0, pl.ds(c0, L), :]
        k = k_ref[0, pl.ds(c0, L), :]
        v = v_ref[0, pl.ds(c0, L), :]
        qk = _bdot_nt(q, k)
        bl = dot(jnp.concatenate(split3(gcol_ref[0, pl.ds(c0, L), :]), axis=1), sel_ref[d])
        yield
        b, li = bl[:, :LANES], bl[:, LANES:]
        x0, x1, x2 = split3(b)
        y0, y1, y2 = split3(li - b)
        xl = jnp.where(lane == 0, x0, jnp.where(lane == 1, x1, jnp.where(
            lane == 2, x2, jnp.where(lane < 6, 1.0, 0.0).astype(BF16))))
        yl = jnp.where(lane == 3, y0, jnp.where(lane == 4, y1, jnp.where(
            lane == 5, y2, jnp.where(lane < 3, 1.0, 0.0).astype(BF16))))
        dm = _bdot_nt(xl, yl)
        gtot = b[L - 1:L, :] if fwd else b[0:1, :]
        a = gtot - b + li
        a_max = jnp.max(a, axis=0, keepdims=True)
        wk = jnp.exp(a - a_max)
        vw = (v.astype(F32) * wk).astype(BF16)
        kv = lax.dot_general(k, vw, (((0,), (0,)), ((), ())), preferred_element_type=F32)
        kn = jnp.sum(k.astype(F32) * wk, axis=0, keepdims=True)
        sc_ref[d, slot] = jnp.where(srow == 0, kn, jnp.where(srow == 1, gtot, a_max))
        bb_ref[d, slot] = b
        yield
        dm = jnp.where(tri, dm, -jnp.inf)
        m_loc = jnp.max(dm, axis=-1, keepdims=True)
        s = qk * jnp.exp(dm - m_loc)
        num = _bdot(s, v)
        den = dot(jnp.concatenate(_split_bf16(s), axis=1), ones)
        mb_ref[d, slot] = jnp.broadcast_to(m_loc, (L, LANES))
        kv_ref[d, slot] = kv
        yield
        num_ref[d, slot] = num
        den_ref[d, slot] = den

    def advance(d, todo):
        ct, n, m = ct_ref[d], n_ref[d], m_ref[d]
        pending = []
        for j, slot in todo:
            c0 = pl.multiple_of(j * L, L)
            q = q_ref[0, pl.ds(c0, L), :]
            qc = _bdot(q, ct)
            qn = dot(jnp.concatenate(_split_bf16(q.astype(F32) * n), axis=1), ones2)
            sc = sc_ref[d, slot]
            kn, gtot, a_max = sc[0:1, :], sc[1:2, :], sc[2:3, :]
            m_new = jnp.maximum(gtot + m, a_max)
            decay = jnp.exp(gtot + m - m_new)
            beta = jnp.exp(a_max - m_new)
            pending.append((c0, slot, qc, qn, m))
            ct = decay * ct + beta * kv_ref[d, slot]
            n = decay * n + beta * kn
            m = m_new
        ct_ref[d], n_ref[d], m_ref[d] = ct, n, m
        yield
        for c0, slot, qc, qn, m_in in pending:
            inter = bb_ref[d, slot] + m_in
            m_loc = mb_ref[d, slot]
            m_t = jnp.maximum(inter, m_loc)
            c_intra = jnp.exp(m_loc - m_t)
            c_inter = jnp.exp(inter - m_t)
            num = c_intra * num_ref[d, slot] + c_inter * qc
            den = c_intra * den_ref[d, slot] + c_inter * qn
            h_ref[pl.ds(c0, L), :] += num / jnp.maximum(jnp.abs(den), jnp.exp(-m_t))

    ct_ref[...] = jnp.zeros_like(ct_ref)
    n_ref[...] = jnp.zeros_like(n_ref)
    m_ref[...] = jnp.full_like(m_ref, -jnp.inf)
    h_ref[...] = jnp.zeros_like(h_ref)
    lock = 8 if grp % 8 == 0 else 1

    def group(gi, carry):
        def phase_a(i, c):
            gens = []
            for u in range(lock):
                jf = gi * grp + i * lock + u
                gens += [prepare(jf, 0, i * lock + u), prepare(nc - 1 - jf, 1, i * lock + u)]
            _round_robin(gens)
            return c

        def phase_b(i, c):
            jf = [gi * grp + i * lock + u for u in range(lock)]
            _round_robin([advance(0, [(j, i * lock + u) for u, j in enumerate(jf)]),
                          advance(1, [(nc - 1 - j, i * lock + u) for u, j in enumerate(jf)])])
            return c

        lax.fori_loop(0, grp // lock, phase_a, 0)
        lax.fori_loop(0, grp // lock, phase_b, 0)
        return carry

    lax.fori_loop(0, nc // grp, group, 0)

    tile = 256
    inv_n = 1.0 / DH_A

    def lane_sum(x):
        return dot(x.astype(BF16), ones)

    def finish(r0):
        hh = h_ref[pl.ds(r0, tile), :]
        mu = lane_sum(hh) * inv_n
        yield
        hc = hh - mu
        var = lane_sum(hc * hc) * inv_n
        yield
        hn = hc * lax.rsqrt(var + HEAD_NORM_EPS) * ng_ref[...]
        out = (hn + sk_ref[...] * xc_ref[0, pl.ds(r0, tile), :]) * _silu(za_ref[pl.ds(r0, tile), :].astype(F32))
        o_ref[0, pl.ds(r0, tile), :] = out

    unroll = 4 if (seq // tile) % 4 == 0 else 1

    def finishes(i, carry):
        _round_robin([finish(pl.multiple_of((i * unroll + u) * tile, tile)) for u in range(unroll)])
        return carry

    lax.fori_loop(0, seq // tile // unroll, finishes, 0)


def _mlstm(q, k, v, xc, proj_g, gcol, norm_g, skip):
    bsz, seq, _ = q.shape
    grp = min(ML_GROUP, seq // CHUNK_A)
    hs = lambda: pl.BlockSpec((1, seq, DH_A), lambda b, h: (b, 0, h))
    return pl.pallas_call(
        _mlstm_kernel,
        grid=(bsz, H_A),
        in_specs=[
            hs(), hs(), hs(), hs(),
            pl.BlockSpec((seq, DH_A), lambda b, h: (b, G_AZ // DH_A + h)),
            pl.BlockSpec((1, seq, LANES), lambda b, h: (b, 0, 0)),
            pl.BlockSpec((1, DH_A), lambda b, h: (0, h)),
            pl.BlockSpec((1, DH_A), lambda b, h: (0, h)),
        ],
        out_specs=pl.BlockSpec((1, seq, DH_A), lambda b, h: (b, 0, h)),
        out_shape=jax.ShapeDtypeStruct((bsz, seq, D_A), F32),
        scratch_shapes=[pltpu.VMEM((seq, DH_A), F32), pltpu.VMEM((2, DH_A, DH_A), F32),
                        pltpu.VMEM((2, 1, DH_A), F32), pltpu.VMEM((2, 1, LANES), F32),
                        pltpu.VMEM((2, grp, CHUNK_A, DH_A), F32),
                        pltpu.VMEM((2, grp, CHUNK_A, LANES), F32),
                        pltpu.VMEM((2, grp, CHUNK_A, LANES), F32),
                        pltpu.VMEM((2, grp, CHUNK_A, LANES), F32),
                        pltpu.VMEM((2, grp, DH_A, DH_A), F32),
                        pltpu.VMEM((2, grp, SUBLANES, LANES), F32),
                        pltpu.VMEM((2, 3 * LANES, 2 * LANES), BF16)],
        compiler_params=_cparams(("arbitrary", "arbitrary")),
        name="mlstm",
    )(q, k, v, xc, proj_g, gcol, norm_g.reshape(1, D_A), skip.reshape(1, D_A))


FFT_N2 = 128
HY_CT = 128
HY_PASSES = 1
FFT_ROWS_UNROLL = 32
FFT_SLAB_UNROLL = 8


def _split_bf16(x):
    hi = x.astype(BF16)
    lo = (x - hi.astype(F32)).astype(BF16)
    return hi, lo


def _dot3(a_hi, a_lo, x):
    x_hi, x_lo = _split_bf16(x)
    d = lambda a, b: jnp.dot(a, b, preferred_element_type=F32)
    if HY_PASSES == 1:
        return d(a_hi, x_hi)
    return d(a_hi, x_hi) + (d(a_lo, x_hi) + d(a_hi, x_lo))


def _dft_constants(seq):
    n = 2 * seq
    n2 = FFT_N2
    n1 = n // n2
    f1 = np.arange(n1)[:, None]
    s1 = np.arange(n1)[None, :]
    th1 = 2.0 * np.pi * f1 * s1 / n1
    fwd1 = np.concatenate([np.cos(th1), -np.sin(th1)], axis=0)
    inv1 = np.concatenate([np.cos(th1), -np.sin(th1)], axis=1)[:n1 // 2] / n
    a = np.arange(n2)
    th2 = 2.0 * np.pi * a[:, None] * a[None, :] / n2
    c2, s2 = np.cos(th2), np.sin(th2)
    fwd2 = np.block([[c2, s2], [-s2, c2]])
    inv2 = np.block([[c2, -s2], [s2, c2]])
    tht = 2.0 * np.pi * a[:, None] * np.arange(n1)[None, :] / n
    tw = np.zeros((2, n2, LANES), np.float64)
    tw[0, :, :n1] = np.cos(tht)
    tw[1, :, :n1] = -np.sin(tht)

    def hl(m):
        m32 = jnp.asarray(m, F32)
        hi = m32.astype(BF16)
        lo = (m32 - hi.astype(F32)).astype(BF16)
        return jnp.stack([hi, lo])

    nf = n1 // 2 + 1
    nfp = -(-nf // SUBLANES) * SUBLANES
    keep = np.zeros((nfp, 1))
    keep[:nf] = 1.0
    thh = 2.0 * np.pi * np.arange(nfp)[:, None] * s1 / n1
    fwd1h = np.concatenate([np.cos(thh) * keep, -np.sin(thh) * keep], axis=0)
    wgt = 2.0 * keep
    wgt[0] = wgt[nf - 1] = 1.0
    inv1h = np.concatenate([(np.cos(thh) * wgt).T, (-np.sin(thh) * wgt).T], axis=1)[:n1 // 2] / n
    return dict(fwd1=hl(fwd1), inv1=hl(inv1), fwd2=hl(fwd2), inv2=hl(inv2),
                fwd1h=hl(fwd1h), inv1h=hl(inv1h), tw=jnp.asarray(tw, F32), n1=n1)


def _fft_stage_a(load_rows, k_rows, fwd1_ref, a_ref):
    n2 = FFT_N2
    rows = fwd1_ref.shape[1]
    f_hi = fwd1_ref[0, :, :k_rows]
    f_lo = fwd1_ref[1, :, :k_rows]

    def one(s2):
        m = _dot3(f_hi, f_lo, load_rows(s2))
        yield
        a_ref[pl.ds(pl.multiple_of(s2 * rows, SUBLANES), rows), :] = m

    def body(i, carry):
        _round_robin([one(i * FFT_ROWS_UNROLL + u) for u in range(FFT_ROWS_UNROLL)])
        return carry

    lax.fori_loop(0, n2 // FFT_ROWS_UNROLL, body, 0)


def _twiddle_cols(tw_ref, f1):
    lane = lax.broadcasted_iota(jnp.int32, (FFT_N2, LANES), 1)
    sel = lane == f1
    twr = jnp.sum(jnp.where(sel, tw_ref[0], 0.0), axis=-1, keepdims=True)
    twi = jnp.sum(jnp.where(sel, tw_ref[1], 0.0), axis=-1, keepdims=True)
    return twr, twi


def _hyena_filter_kernel(feat_ref, w1_ref, b1_ref, w2_ref, b2_ref, w3_ref, b3_ref, fr_ref,
                         wo_ref, dec_ref, fwd1_ref, fwd2_ref, tw_ref, o_ref, kt_ref, a_ref, hid_ref):
    n = feat_ref.shape[0]
    seq = n // 2
    n2 = FFT_N2
    n1 = n // n2
    tile = 512
    freq = fr_ref[...]

    @pl.when(pl.program_id(0) == 0)
    def _():
        def hidden(i, carry):
            r0 = pl.multiple_of(i * tile, tile)
            hid = jnp.sin(freq * (_hdot(feat_ref[pl.ds(r0, tile), :], w1_ref[...]) + b1_ref[...]))
            hid = jnp.sin(freq * (_hdot(hid, w2_ref[...]) + b2_ref[...]))
            hid_ref[pl.ds(r0, tile), :] = jnp.sin(freq * (_hdot(hid, w3_ref[...]) + b3_ref[...]))
            return carry

        lax.fori_loop(0, n // tile, hidden, 0)

    def gen(i, carry):
        r0 = pl.multiple_of(i * tile, tile)
        second = r0 >= seq
        wo = jnp.where(second, wo_ref[1, 0], wo_ref[0, 0])
        dec = jnp.where(second, dec_ref[1, 0], dec_ref[0, 0])
        t = feat_ref[pl.ds(r0, tile), 0:1]
        filt = _hdot(hid_ref[pl.ds(r0, tile), :], wo) * jnp.exp(-t * jnp.abs(dec))
        rows = r0 + lax.broadcasted_iota(jnp.int32, (tile, 1), 0)
        kt_ref[pl.ds(r0, tile), :] = jnp.where(rows == seq, 0.0, filt)
        return carry

    lax.fori_loop(0, n // tile, gen, 0)

    _fft_stage_a(lambda s2: kt_ref[pl.ds(s2, n1, stride=n2), :], n1, fwd1_ref, a_ref)

    nf = n1 // 2 + 1
    nfp = fwd1_ref.shape[1] // 2

    def slab(f1):
        r0 = f1 * 2 * n2 if isinstance(f1, int) else pl.multiple_of(f1 * 2 * n2, 2 * n2)
        ar = a_ref[pl.ds(f1, n2, stride=2 * nfp), :]
        ai = a_ref[pl.ds(nfp + f1, n2, stride=2 * nfp), :]
        twr, twi = _twiddle_cols(tw_ref, f1)
        x = jnp.concatenate([ar * twr - ai * twi, ar * twi + ai * twr], axis=0)
        xf = _dot3(fwd2_ref[0], fwd2_ref[1], x)
        yield
        o_ref[0, pl.ds(r0, 2 * n2), :] = xf

    def slabs(i, carry):
        _round_robin([slab(i * FFT_SLAB_UNROLL + u) for u in range(FFT_SLAB_UNROLL)])
        return carry

    lax.fori_loop(0, nf // FFT_SLAB_UNROLL, slabs, 0)
    _round_robin([slab(f1) for f1 in range(nf - nf % FFT_SLAB_UNROLL, nf)])


def _hyena_filter_spectrum(consts, feats_full, w1p, b1, w2, b2, w3, b3, freq, w_out, decay):
    n = feats_full.shape[0]
    nct = D_B // HY_CT
    spec_rows = (consts['n1'] // 2 + 1) * 2 * FFT_N2
    full = lambda a: pl.BlockSpec(a.shape, lambda j: (0,) * a.ndim)
    wo = w_out.reshape(HY_HID, 2, nct, HY_CT).transpose(1, 2, 0, 3)
    dec = decay.reshape(2, nct, 1, HY_CT)
    small = (w1p, b1.reshape(1, -1), w2, b2.reshape(1, -1), w3, b3.reshape(1, -1), freq.reshape(1, -1))
    return pl.pallas_call(
        _hyena_filter_kernel,
        grid=(nct,),
        in_specs=[full(feats_full)] + [full(a) for a in small] + [
            pl.BlockSpec((2, 1, HY_HID, HY_CT), lambda j: (0, j, 0, 0)),
            pl.BlockSpec((2, 1, 1, HY_CT), lambda j: (0, j, 0, 0)),
            full(consts['fwd1h']), full(consts['fwd2']), full(consts['tw']),
        ],
        out_specs=pl.BlockSpec((1, spec_rows, HY_CT), lambda j: (j, 0, 0)),
        out_shape=jax.ShapeDtypeStruct((nct, spec_rows, HY_CT), F32),
        scratch_shapes=[pltpu.VMEM((n, HY_CT), F32), pltpu.VMEM((2 * n, HY_CT), F32),
                        pltpu.VMEM((n, HY_HID), F32)],
        compiler_params=_cparams(("arbitrary",)),
        name="hyena_filter",
    )(feats_full, *small, wo, dec, consts['fwd1h'], consts['fwd2'], consts['tw'])


def _hyena_conv_kernel(z_ref, x0_ref, zb_ref, ks_ref, bias_ref, fwd1_ref, inv1_ref, fwd2_ref,
                       inv2_ref, tw_ref, o_ref, a_ref, b_ref, y_ref):
    seq = z_ref.shape[1]
    n2 = FFT_N2
    n1 = 2 * seq // n2
    nh = n1 // 2
    nf = nh + 1
    nfp = fwd1_ref.shape[1] // 2

    _fft_stage_a(lambda s2: z_ref[0, pl.ds(s2, nh, stride=n2), :], nh, fwd1_ref, a_ref)
    b_ref[pl.ds(nf * 2 * n2, (nfp - nf) * 2 * n2), :] = jnp.zeros(((nfp - nf) * 2 * n2, b_ref.shape[1]), F32)

    def slab(f1):
        r0 = f1 * 2 * n2 if isinstance(f1, int) else pl.multiple_of(f1 * 2 * n2, 2 * n2)
        ar = a_ref[pl.ds(f1, n2, stride=2 * nfp), :]
        ai = a_ref[pl.ds(nfp + f1, n2, stride=2 * nfp), :]
        twr, twi = _twiddle_cols(tw_ref, f1)
        x = jnp.concatenate([ar * twr - ai * twi, ar * twi + ai * twr], axis=0)
        xf = _dot3(fwd2_ref[0], fwd2_ref[1], x)
        yield
        xr, xi = xf[:n2], xf[n2:]
        kr = ks_ref[0, pl.ds(r0, n2), :]
        ki = ks_ref[0, pl.ds(r0 + n2, n2), :]
        y = jnp.concatenate([xr * kr - xi * ki, xr * ki + xi * kr], axis=0)
        bb = _dot3(inv2_ref[0], inv2_ref[1], y)
        yield
        br, bi = bb[:n2], bb[n2:]
        b_ref[pl.ds(r0, n2), :] = br * twr + bi * twi
        b_ref[pl.ds(r0 + n2, n2), :] = bi * twr - br * twi

    def slabs(i, carry):
        _round_robin([slab(i * FFT_SLAB_UNROLL + u) for u in range(FFT_SLAB_UNROLL)])
        return carry

    lax.fori_loop(0, nf // FFT_SLAB_UNROLL, slabs, 0)
    _round_robin([slab(f1) for f1 in range(nf - nf % FFT_SLAB_UNROLL, nf)])

    i_hi = inv1_ref[0]
    i_lo = inv1_ref[1]

    def stage_d(s2):
        br = b_ref[pl.ds(s2, nfp, stride=2 * n2), :]
        bi = b_ref[pl.ds(n2 + s2, nfp, stride=2 * n2), :]
        y = _dot3(i_hi, i_lo, jnp.concatenate([br, bi], axis=0))
        yield
        y_ref[pl.ds(s2, nh, stride=n2), :] = y

    def stage_ds(i, carry):
        _round_robin([stage_d(i * FFT_ROWS_UNROLL + u) for u in range(FFT_ROWS_UNROLL)])
        return carry

    lax.fori_loop(0, n2 // FFT_ROWS_UNROLL, stage_ds, 0)

    tile = 256
    def finish(i, carry):
        r0 = pl.multiple_of(i * tile, tile)
        z = z_ref[0, pl.ds(r0, tile), :]
        y = y_ref[pl.ds(r0, tile), :] + bias_ref[...] * z
        o_ref[0, pl.ds(r0, tile), :] = x0_ref[0, pl.ds(r0, tile), :] * y * _silu(zb_ref[pl.ds(r0, tile), :].astype(F32))
        return carry

    lax.fori_loop(0, seq // tile, finish, 0)


def _hyena_conv(consts, z, x0, proj_g, kspec, bias):
    bsz, seq, _ = z.shape
    nct = D_B // HY_CT
    full = lambda a: pl.BlockSpec(a.shape, lambda j, b: (0,) * a.ndim)
    cs = pl.BlockSpec((1, seq, HY_CT), lambda j, b: (b, 0, j))
    mats = (consts['fwd1h'], consts['inv1h'], consts['fwd2'], consts['inv2'], consts['tw'])
    stage_rows = consts['fwd1h'].shape[1]
    return pl.pallas_call(
        _hyena_conv_kernel,
        grid=(nct, bsz),
        in_specs=[
            cs, cs,
            pl.BlockSpec((seq, HY_CT), lambda j, b: (b, G_BZ // HY_CT + j)),
            pl.BlockSpec((1, kspec.shape[1], HY_CT), lambda j, b: (j, 0, 0)),
            pl.BlockSpec((1, HY_CT), lambda j, b: (0, j)),
        ] + [full(m) for m in mats],
        out_specs=cs,
        out_shape=jax.ShapeDtypeStruct((bsz, seq, D_B), F32),
        scratch_shapes=[pltpu.VMEM((FFT_N2 * stage_rows, HY_CT), F32),
                        pltpu.VMEM((stage_rows * FFT_N2, HY_CT), F32),
                        pltpu.VMEM((seq, HY_CT), F32)],
        compiler_params=_cparams(("arbitrary", "arbitrary")),
        name="hyena_conv",
    )(z, x0, proj_g, kspec, bias.reshape(1, D_B), *mats)


def _rwkv_kernel(r_ref, k_ref, v_ref, ul_ref, cz_ref, w0_ref, w2_ref, a0_ref, a2_ref,
                 kks_ref, ka_ref, rk_ref, lg_ref, lb_ref, o_ref, y_ref, kb_ref, s_ref,
                 wr_ref, u_ref, arb_ref, y0_ref, bt_ref, kv_ref, egc_ref,
                 kk_ref, lw_ref, be_ref, kd_ref):
    seq = r_ref.shape[1]
    L = CHUNK_C
    nc = seq // L
    P2 = 2 * L
    lane = lax.broadcasted_iota(jnp.int32, (L, LANES), 1)
    head0 = lane < HEAD_C
    prow = lax.broadcasted_iota(jnp.int32, (P2, P2), 0)
    pcol = lax.broadcasted_iota(jnp.int32, (P2, P2), 1)
    bd = (prow // HEAD_C) == (pcol // HEAD_C)
    blk = jnp.where(bd, 1.0, 0.0).astype(F32)
    trow = lax.broadcasted_iota(jnp.int32, (L, P2), 0)
    scol = lax.broadcasted_iota(jnp.int32, (L, P2), 1) % L
    eye = jnp.where(trow == scol, 1.0, 0.0).astype(F32)
    crow = lax.broadcasted_iota(jnp.int32, (L, L), 0)
    ccol = lax.broadcasted_iota(jnp.int32, (L, L), 1)

    def stack(x):
        return jnp.concatenate([jnp.where(head0, x, 0.0), jnp.where(head0, 0.0, x)], axis=0)

    blk_b = blk.astype(BF16)
    grp = wr_ref.shape[1]

    def dot(a, b):
        return jnp.dot(a, b, preferred_element_type=F32)

    def each(f, *cols):
        return [f(*xs) for xs in zip(*cols)]

    def prepare(probs):
        dirs = [d for _, d, _ in probs]
        fwd = [d == 0 for d in dirs]
        strict = [(scol < trow) if f else (scol > trow) for f in fwd]
        incl = [(scol <= trow) if f else (scol >= trow) for f in fwd]
        cum = [jnp.where((ccol <= crow) if f else (ccol >= crow), 1.0, 0.0).astype(BF16) for f in fwd]
        c0 = [pl.multiple_of(j * L, L) for j, _, _ in probs]
        r = [r_ref[0, pl.ds(c, L), :] for c in c0]
        v = [v_ref[0, pl.ds(c, L), :] for c in c0]
        kk = [kk_ref[pl.ds(c, L), :] for c in c0]
        lwt = [lw_ref[d, pl.ds(c, L), :] for c, d in zip(c0, dirs)]
        be = [be_ref[d, pl.ds(c, L), :] for c, d in zip(c0, dirs)]
        kd = [kd_ref[d, pl.ds(c, L), :] for c, d in zip(c0, dirs)]

        def cumsum(cm, x):
            l1 = x.astype(BF16)
            e1 = x - l1.astype(F32)
            l2 = e1.astype(BF16)
            l3 = (e1 - l2.astype(F32)).astype(BF16)
            return dot(jnp.concatenate([cm, cm, cm], axis=1), jnp.concatenate([l1, l2, l3], axis=0))

        g = each(cumsum, cum, lwt)
        yield
        eng = [jnp.exp(-x) for x in g]
        egc = [jnp.exp(x[L - 1:L, :] if f else x[0:1, :]) for x, f in zip(g, fwd)]
        at = each(lambda x, y, z: (-x * jnp.exp(y - z)).astype(BF16), kk, g, lwt)
        rt_ = each(lambda x, y: (x * jnp.exp(y)).astype(BF16), r, g)
        bh = each(lambda x, y: x * y, be, eng)
        kh = each(lambda x, y: x * y, kd, eng)
        lhs = each(lambda x, y: jnp.concatenate([x, y], axis=0), at, rt_)
        rhs = each(lambda x, y: jnp.concatenate([stack(x), stack(y)], axis=0).astype(BF16), bh, kh)
        g1 = each(lambda x, y: lax.dot_general(x, y, (((1,), (1,)), ((), ())),
                                               preferred_element_type=F32), lhs, rhs)
        yield
        aab = each(lambda m, x: jnp.where(m, x[:L, :P2], 0.0), strict, g1)
        aak_ark = each(lambda ms, mi, x: jnp.concatenate(
            [jnp.where(ms, x[:L, P2:], 0.0), jnp.where(mi, x[L:, P2:], 0.0)], axis=0).astype(BF16),
            strict, incl, g1)
        for (_, d, slot), mi, x in zip(probs, incl, g1):
            arb_ref[d, slot] = jnp.where(mi, x[L:, :P2], 0.0).astype(BF16)
        t = [eye + x for x in aab]
        p = [x.astype(BF16) for x in aab]
        vs = [stack(x).astype(BF16) for x in v]
        av = each(dot, aak_ark, vs)
        p = [dot(x, stack(x)) for x in p]
        yield
        p = [x.astype(BF16) for x in p]
        sq = 2
        while sq < L:
            if 2 * sq >= L:
                pt = each(lambda x, y: dot(x, stack(y.astype(BF16))), p, t)
                yield
                t = each(lambda y, x: y + x, t, pt)
            else:
                pt = each(lambda x, y: dot(x, jnp.concatenate([stack(x), stack(y.astype(BF16))], axis=1)),
                          p, t)
                yield
                p = [x[:, :P2].astype(BF16) for x in pt]
                t = each(lambda y, x: y + x[:, P2:], t, pt)
            sq *= 2
        wu = each(lambda x, y, z: dot(x.astype(BF16), jnp.concatenate(
            [stack(y), stack(z[:L].astype(BF16))], axis=1)), t, at, av)
        kv = each(lambda x, y, z: lax.dot_general(
            x.astype(BF16), (y * z).astype(BF16), (((0,), (0,)), ((), ())), preferred_element_type=F32),
            v, kh, egc)
        yield
        for i, (_, d, slot) in enumerate(probs):
            wr_ref[d, slot] = jnp.concatenate([wu[i][:, :P2].astype(BF16), rt_[i]], axis=0)
            u_ref[d, slot] = wu[i][:, P2:].astype(BF16)
            y0_ref[d, slot] = av[i][L:]
            bt_ref[d, slot] = (bh[i] * egc[i]).astype(BF16)
            kv_ref[d, slot] = jnp.where(bd, kv[i], 0.0).astype(BF16)
            egc_ref[d, slot] = egc[i]

    def advance(steps):
        s = [s_ref[d] for d in range(2)]
        for probs in steps:
            x = [lax.dot_general(wr_ref[d, slot], s[d].astype(BF16), (((1,), (1,)), ((), ())),
                                 preferred_element_type=F32) for _, d, slot in probs]
            yield
            sa = [(y[:L] + u_ref[d, slot].astype(F32)).astype(BF16) for (_, d, slot), y in zip(probs, x)]
            upd = [lax.dot_general(y, bt_ref[d, slot], (((0,), (0,)), ((), ())),
                                   preferred_element_type=F32) for (_, d, slot), y in zip(probs, sa)]
            z = [dot(arb_ref[d, slot], stack(y)) for (_, d, slot), y in zip(probs, sa)]
            yield
            for i, (j, d, slot) in enumerate(probs):
                s[d] = (s[d] * egc_ref[d, slot] + kv_ref[d, slot].astype(F32)
                        + jnp.where(bd, upd[i], 0.0))
                c0 = pl.multiple_of(j * L, L)
                y_ref[pl.ds(c0, L), :] += x[i][L:] + z[i] + y0_ref[d, slot]
        for d in range(2):
            s_ref[d] = s[d]

    s_ref[...] = jnp.zeros_like(s_ref)
    y_ref[...] = jnp.zeros_like(y_ref)
    tile = 256

    def pointwise(r0):
        k = k_ref[0, pl.ds(r0, tile), :]
        lwla = ul_ref[0, pl.ds(r0, tile), :]
        th = jnp.tanh(lwla[:, :LANES]).astype(BF16)
        la = lwla[:, LANES:].astype(BF16)
        xw = [w0_ref[d:d + 1, :] + dot(th, w2_ref[d].astype(BF16)) for d in range(2)]
        aa = [a0_ref[d:d + 1, :] + dot(la, a2_ref[d].astype(BF16)) for d in range(2)]
        kkv = k * kks_ref[...]
        ss = dot((kkv * kkv).astype(BF16), blk_b)
        yield
        kk = kkv * lax.rsqrt(jnp.maximum(ss, 1e-24))
        kk_ref[pl.ds(r0, tile), :] = kk
        kb = None
        for d in range(2):
            lw_ref[d, pl.ds(r0, tile), :] = -math.exp(-0.5) * _sigmoid(xw[d])
            a = _sigmoid(aa[d])
            kd = k * (1.0 + (a - 1.0) * ka_ref[...])
            be_ref[d, pl.ds(r0, tile), :] = kk * a
            kd_ref[d, pl.ds(r0, tile), :] = kd
            kb = kd if kb is None else kb + kd
        kb_ref[pl.ds(r0, tile), :] = kb

    pw_unroll = 2 if (seq // tile) % 2 == 0 else 1

    def pointwise_loop(i, carry):
        _round_robin([pointwise(pl.multiple_of((i * pw_unroll + u) * tile, tile))
                      for u in range(pw_unroll)])
        return carry

    lax.fori_loop(0, seq // tile // pw_unroll, pointwise_loop, 0)

    lock = grp // 2
    nsets = nc // lock

    def chunk_set(i):
        base = (i % 2) * lock
        return [[(i * lock + q, 0, base + q), (nc - 1 - (i * lock + q), 1, base + q)]
                for q in range(lock)]

    _round_robin([prepare(sum(chunk_set(0), []))])

    def pipelined(i, carry):
        _round_robin([prepare(sum(chunk_set(i + 1), [])), advance(chunk_set(i))])
        return carry

    lax.fori_loop(0, nsets - 1, pipelined, 0)
    _round_robin([advance(chunk_set(nsets - 1))])

    tile = 256
    inv_n = 1.0 / HEAD_C
    def head_sum(x):
        return dot(x.astype(BF16), blk_b)

    def finish(r0):
        y = y_ref[pl.ds(r0, tile), :]
        mu = head_sum(y) * inv_n
        r = r_ref[0, pl.ds(r0, tile), :]
        v = v_ref[0, pl.ds(r0, tile), :]
        kbon = 0.5 * kb_ref[pl.ds(r0, tile), :]
        bonus = head_sum(r * kbon * rk_ref[...]) * v
        yield
        yc = y - mu
        var = head_sum(yc * yc) * inv_n
        yield
        yn = yc * lax.rsqrt(var + LNX_EPS) * lg_ref[...] + lb_ref[...]
        o_ref[0, pl.ds(r0, tile), :] = (yn + bonus) * _silu(cz_ref[pl.ds(r0, tile), :].astype(F32))

    unroll = 4 if (seq // tile) % 4 == 0 else 1

    def finishes(i, carry):
        _round_robin([finish(pl.multiple_of((i * unroll + u) * tile, tile)) for u in range(unroll)])
        return carry

    lax.fori_loop(0, seq // tile // unroll, finishes, 0)


def _rwkv(u, ul, proj_g, w0, w2pad, a0, a2pad, kk_s, ka, rk, lnx_g, lnx_b):
    bsz, seq, _ = u.shape
    npair = D_C // LANES
    grp = min(RW_GROUP, seq // CHUNK_C)
    col = lambda off: pl.BlockSpec((1, seq, LANES), lambda b, p: (b, 0, off + p))
    vec = pl.BlockSpec((1, LANES), lambda b, p: (0, p))
    two = pl.BlockSpec((2, LANES), lambda b, p: (0, p))
    lora = pl.BlockSpec((2, LANES, LANES), lambda b, p: (0, 0, p))
    row = lambda a: a.reshape(1, D_C)
    return pl.pallas_call(
        _rwkv_kernel,
        grid=(bsz, npair),
        in_specs=[
            col(0), col(npair), col(2 * npair),
            pl.BlockSpec((1, seq, 2 * LANES), lambda b, p: (b, 0, 0)),
            pl.BlockSpec((seq, LANES), lambda b, p: (b, G_CZ // LANES + p)),
            two, lora, two, lora, vec, vec, vec, vec, vec,
        ],
        out_specs=pl.BlockSpec((1, seq, LANES), lambda b, p: (b, 0, p)),
        out_shape=jax.ShapeDtypeStruct((bsz, seq, D_C), F32),
        scratch_shapes=[pltpu.VMEM((seq, LANES), F32), pltpu.VMEM((seq, LANES), F32),
                        pltpu.VMEM((2, LANES, LANES), F32),
                        pltpu.VMEM((2, grp, 2 * CHUNK_C, LANES), BF16),
                        pltpu.VMEM((2, grp, CHUNK_C, LANES), BF16),
                        pltpu.VMEM((2, grp, CHUNK_C, LANES), BF16),
                        pltpu.VMEM((2, grp, CHUNK_C, LANES), F32),
                        pltpu.VMEM((2, grp, CHUNK_C, LANES), BF16),
                        pltpu.VMEM((2, grp, LANES, LANES), BF16),
                        pltpu.VMEM((2, grp, 1, LANES), F32),
                        pltpu.VMEM((seq, LANES), F32), pltpu.VMEM((2, seq, LANES), F32),
                        pltpu.VMEM((2, seq, LANES), F32), pltpu.VMEM((2, seq, LANES), F32)],
        compiler_params=_cparams(("arbitrary", "arbitrary")),
        name="rwkv7",
    )(u, u, u, ul, proj_g, w0, w2pad, a0, a2pad, row(kk_s), row(ka), row(rk), row(lnx_g), row(lnx_b))


def _out_kernel(ya_ref, yb_ref, yc_ref, ga_ref, gb_ref, gc_ref, x_ref, gate_ref, pg_ref,
                wa_ref, wb_ref, wc_ref, wo_ref, o_ref):
    merged = (_sigmoid(ga_ref[...].astype(F32)) * _bdot(ya_ref[...], wa_ref[...])
              + _sigmoid(gb_ref[...].astype(F32)) * _bdot(yb_ref[...], wb_ref[...])
              + _sigmoid(gc_ref[...].astype(F32)) * _bdot(yc_ref[...], wc_ref[...]))
    out = _bdot(merged, wo_ref[...])
    y = out * lax.rsqrt(jnp.mean(out * out, axis=-1, keepdims=True) + NORM_EPS) * pg_ref[...]
    o_ref[...] = x_ref[...] + gate_ref[0] * y


def _merge_out(ya, yb, yc, proj_g, x2d, seq, gate, post_g, wa, wb, wc, wo, tm):
    rows, d = x2d.shape
    per_b = seq // tm
    full = lambda a: pl.BlockSpec(a.shape, lambda i: (0,) * a.ndim)
    gcol = lambda off: pl.BlockSpec((tm, d), lambda i: (i, off // d))
    return pl.pallas_call(
        _out_kernel,
        grid=(rows // tm,),
        in_specs=[
            pl.BlockSpec((tm, D_A), lambda i: (i, 0)),
            pl.BlockSpec((tm, D_B), lambda i: (i, 0)),
            pl.BlockSpec((tm, D_C), lambda i: (i, 0)),
            gcol(G_GA), gcol(G_GB), gcol(G_GC),
            pl.BlockSpec((tm, d), lambda i: (i, 0)),
            pl.BlockSpec((1, 1, d), lambda i: (i // per_b, 0, 0)),
            pl.BlockSpec((1, d), lambda i: (0, 0)),
            full(wa), full(wb), full(wc), full(wo),
        ],
        out_specs=pl.BlockSpec((tm, d), lambda i: (i, 0)),
        out_shape=jax.ShapeDtypeStruct((rows, d), F32),
        compiler_params=_cparams(("arbitrary",)),
        name="merge_out",
    )(ya.reshape(rows, D_A), yb.reshape(rows, D_B), yc.reshape(rows, D_C), proj_g, proj_g, proj_g,
      x2d, gate, post_g.reshape(1, d), wa, wb, wc, wo)


def _hyena_feats(seq):
    n = 2 * seq
    t = jnp.linspace(0.0, 1.0, seq, dtype=F32)
    w = 2.0 * math.pi * jnp.arange(seq, dtype=F32) / seq
    f = jnp.linspace(1e-4, HY_BANDS - 1, HY_BANDS, dtype=F32)
    zz = w[:, None] * f[None, :]
    feats = jnp.concatenate([t[:, None], jnp.cos(zz), -jnp.sin(zz)], axis=-1)
    pos = np.concatenate([np.arange(seq), [0], np.arange(seq - 1, 0, -1)])
    full = jnp.zeros((n, LANES), F32).at[:, :feats.shape[1]].set(feats[pos])
    return full


def _tiles(seq):
    return min(seq, 2048), min(seq, 256), min(seq, 512)


def kernel(x, c, ada_w, ada_b, pre_g, post_g, w_in, ml_conv_w, ml_conv_b, ml_wq, ml_wk, ml_wv, ml_gate_w, ml_gate_b, ml_norm_g, ml_skip, hy_conv_w, hy_conv_b, hy_w1, hy_b1, hy_w2, hy_b2, hy_w3, hy_b3, hy_freq, hy_w_out, hy_decay, hy_bias, rw_mu, rw_w0, rw_w2, rw_a0, rw_a2, rw_kk, rw_ka, rw_rk, rw_lnx_g, rw_lnx_b, w_branch_a, w_branch_b, w_branch_c, w_out):
    bsz, seq, d = x.shape
    depth = ada_w.shape[0]
    tm_in, ts_prep, tm_out = _tiles(seq)
    runs_h, runs_g = (_contiguous_runs(p) for p in _ref_column_perm())
    consts = _dft_constants(seq)
    feats_full = _hyena_feats(seq)
    mod = _modulation(c, ada_w, ada_b)
    x2d = x.reshape(bsz * seq, d)
    n_gate = 4 * H_A
    for l in range(depth):
        shift = mod[l, :, None, 0:d]
        scale = mod[l, :, None, d:2 * d]
        gate = mod[l, :, None, 2 * d:3 * d]
        w_h = jnp.concatenate([w_in[l][:, a:b] for a, b in runs_h], axis=1).astype(BF16)
        w_g = jnp.concatenate([w_in[l][:, a:b] for a, b in runs_g], axis=1).astype(BF16)
        proj_h = _inproj(x2d, seq, pre_g[l], shift, scale, w_h, tm_in, N_H // 3)
        proj_g = _inproj(x2d, seq, pre_g[l], shift, scale, w_g, tm_in, N_G // 4)

        gw = jnp.zeros((3 * D_A, LANES), F32).at[:, :n_gate].set(ml_gate_w[l])
        gb = jnp.zeros((1, LANES), F32).at[0, :n_gate].set(ml_gate_b[l])
        q, k, v, xc, gcol, z, x0, u, ul = _prep(
            proj_h, bsz, seq, ts_prep, ml_conv_w[l], ml_conv_b[l].reshape(1, D_A),
            ml_wq[l], ml_wk[l], ml_wv[l], gw, gb,
            hy_conv_w[l], hy_conv_b[l].reshape(1, 3 * D_B), rw_mu[l].reshape(1, -1))

        y_a = _mlstm(q, k, v, xc, proj_g, gcol, ml_norm_g[l], ml_skip[l])

        w1p = jnp.zeros((LANES, HY_HID), F32).at[:hy_w1.shape[1]].set(hy_w1[l])
        kspec = _hyena_filter_spectrum(consts, feats_full, w1p, hy_b1[l], hy_w2[l], hy_b2[l],
                                       hy_w3[l], hy_b3[l], hy_freq[l], hy_w_out[l], hy_decay[l])
        y_b = _hyena_conv(consts, z, x0, proj_g, kspec, hy_bias[l])

        w2pad = (jnp.zeros((2, LANES, D_C), F32).at[0, :LORA].set(rw_w2[l, 0])
                 .at[1, LORA:].set(rw_w2[l, 1]))
        a2pad = (jnp.zeros((2, LANES, D_C), F32).at[0, :LORA].set(rw_a2[l, 0])
                 .at[1, LORA:].set(rw_a2[l, 1]))
        y_c = _rwkv(u, ul, proj_g, rw_w0[l], w2pad, rw_a0[l], a2pad, rw_kk[l], rw_ka[l],
                    rw_rk[l].reshape(-1), rw_lnx_g[l], rw_lnx_b[l])

        x2d = _merge_out(y_a, y_b, y_c, proj_g, x2d, seq, gate, post_g[l],
                         w_branch_a[l].astype(BF16), w_branch_b[l].astype(BF16),
                         w_branch_c[l].astype(BF16), w_out[l].astype(BF16), tm_out)
    return x2d.reshape(bsz, seq, d)
```
